```python
import jax
import jax.numpy as jnp
from jax import lax
import numpy as np

D_MODEL = 1024
BATCH = 16
SEQ = 4096
DEPTH = 2
DEC_BATCH = 16
DEC_SEQ = 64
PAST_LEN = 4096

CHUNK = 64
HEAD_DIM = 64
N_SB_HEADS = 8
N_FOX_HEADS = 8
N_SWA_HEADS = 16
N_SWA_KV_HEADS = 2
SWA_GROUP = N_SWA_HEADS // N_SWA_KV_HEADS
WINDOW = 128
WINDOW_CHUNKS = WINDOW // CHUNK
BAND = (WINDOW_CHUNKS + 1) * CHUNK
SWA_CACHE = WINDOW
QUERY_BLOCK = 128
D_FF = 2816
N_EXPERTS = 8
TOP_K = 2
D_EXPERT = 3584
MOE_BLOCK = 256
N_AB_LAYERS = (DEPTH + 1) // 2
N_C_LAYERS = DEPTH // 2
SB_W = N_SB_HEADS * HEAD_DIM
FOX_W = N_FOX_HEADS * HEAD_DIM
MIX_AB_IN = 3 * SB_W + 3 * FOX_W + N_FOX_HEADS
SWA_Q_W = N_SWA_HEADS * HEAD_DIM
SWA_KV_W = N_SWA_KV_HEADS * HEAD_DIM
MIX_C_IN = SWA_Q_W + 2 * SWA_KV_W
FORGET_BIAS_MEAN = 2.0
RMS_EPS = 1e-6
NEG_INF = -1e30
ATTN_SCALE = HEAD_DIM ** -0.5

kernel_name = 'hybrid_stream_encoder_step'


def rms_norm(x, g):
    x32 = x.astype(jnp.float32)
    y = x32 * lax.rsqrt(jnp.mean(x32 * x32, axis=-1, keepdims=True) + RMS_EPS)
    return (y * g.astype(jnp.float32)).astype(x.dtype)


def swiglu(x, w_gate, w_up, w_down):
    return (jax.nn.silu(x @ w_gate) * (x @ w_up)) @ w_down


def alibi_slopes(n_heads):
    return 2.0 ** (-8.0 * (jnp.arange(n_heads, dtype=jnp.float32) + 1.0) / n_heads)


def to_blocks(a, blk=QUERY_BLOCK):
    b, t = a.shape[:2]
    return jnp.moveaxis(a.reshape((b, t // blk, blk) + a.shape[2:]), 1, 0)


def from_blocks(a):
    a = jnp.moveaxis(a, 0, 1)
    return a.reshape((a.shape[0], a.shape[1] * a.shape[2]) + a.shape[3:])


def stick_breaking_attend(q, k, v, q_pos, k_pos):
    z = jnp.einsum('bqhd,bkhd->bhqk', q.astype(jnp.float32), k.astype(jnp.float32)) * ATTN_SCALE
    vis = k_pos[None, :] < q_pos[:, None]
    log_keep = jnp.where(vis, jax.nn.log_sigmoid(-z), 0.0)
    log_keep_after = lax.cumsum(log_keep, axis=3, reverse=True) - log_keep
    w = jnp.where(vis, jnp.exp(jax.nn.log_sigmoid(z) + log_keep_after), 0.0)
    return jnp.einsum('bhqk,bkhd->bqhd', w, v.astype(jnp.float32)).astype(v.dtype)


def forgetting_attend(q, k, v, c_q, c_k, q_pos, k_pos):
    s = jnp.einsum('bqhd,bkhd->bhqk', q.astype(jnp.float32), k.astype(jnp.float32)) * ATTN_SCALE
    s = s + jnp.swapaxes(c_q, 1, 2)[:, :, :, None] - jnp.swapaxes(c_k, 1, 2)[:, :, None, :]
    vis = k_pos[None, :] <= q_pos[:, None]
    p = jax.nn.softmax(jnp.where(vis, s, NEG_INF), axis=-1)
    return jnp.einsum('bhqk,bkhd->bqhd', p, v.astype(jnp.float32)).astype(v.dtype)


def swa_attend(q, k, v, q_pos, k_pos, sinks):
    s = jnp.einsum('...qhgd,...khd->...hgqk', q.astype(jnp.float32), k.astype(jnp.float32)) * ATTN_SCALE
    tq = q_pos[..., :, None]
    sk = k_pos[..., None, :]
    vis = (sk >= 0) & (sk // CHUNK <= tq // CHUNK) & (sk // CHUNK >= tq // CHUNK - WINDOW_CHUNKS)
    dist = jnp.abs(tq - sk).astype(jnp.float32)
    slopes = alibi_slopes(N_SWA_HEADS).reshape(N_SWA_KV_HEADS, SWA_GROUP, 1, 1)
    s = s - slopes * dist[..., None, None, :, :]
    s = jnp.where(vis[..., None, None, :, :], s, NEG_INF)
    sink = jnp.broadcast_to(sinks.astype(jnp.float32).reshape(N_SWA_KV_HEADS, SWA_GROUP, 1, 1), s.shape[:-1] + (1,))
    p = jax.nn.softmax(jnp.concatenate([s, sink], axis=-1), axis=-1)[..., :-1]
    return jnp.einsum('...hgqk,...khd->...qhgd', p, v.astype(jnp.float32)).astype(v.dtype)


def mix_ab_project(h, w_in, b_forget):
    b, t = h.shape[:2]
    p = h @ w_in
    q_sb, k_sb, v_sb, q_fx, k_fx, v_fx, f_logit = jnp.split(
        p, [SB_W, 2 * SB_W, 3 * SB_W, 3 * SB_W + FOX_W, 3 * SB_W + 2 * FOX_W, 3 * SB_W + 3 * FOX_W], axis=-1)
    sb = lambda a: a.reshape(b, t, N_SB_HEADS, HEAD_DIM)
    fx = lambda a: a.reshape(b, t, N_FOX_HEADS, HEAD_DIM)
    logf = jax.nn.log_sigmoid((f_logit + b_forget).astype(jnp.float32))
    return sb(q_sb), sb(k_sb), sb(v_sb), fx(q_fx), fx(k_fx), fx(v_fx), logf


def mixer_ab(h, past, past_len, w_in, b_forget, w_out):
    b, t = h.shape[:2]
    q_sb, k_sb, v_sb, q_fx, k_fx, v_fx, logf = mix_ab_project(h, w_in, b_forget)
    if past is None:
        pos = jnp.arange(t)
        c = jnp.cumsum(logf, axis=1)

        def block(args):
            qs, qf, cq, qp = args
            return (stick_breaking_attend(qs, k_sb, v_sb, qp, pos),
                    forgetting_attend(qf, k_fx, v_fx, cq, c, qp, pos))

        o_sb, o_fx = lax.map(block, (to_blocks(q_sb), to_blocks(q_fx), to_blocks(c), pos.reshape(-1, QUERY_BLOCK)))
        o_sb, o_fx = from_blocks(o_sb), from_blocks(o_fx)
    else:
        p_sb_k, p_sb_v, p_fx_k, p_fx_v, p_logf = past
        k_pos = jnp.arange(past_len + t)
        q_pos = past_len + jnp.arange(t)
        c = jnp.cumsum(jnp.concatenate([p_logf.astype(jnp.float32), logf], axis=1), axis=1)
        o_sb = stick_breaking_attend(q_sb, jnp.concatenate([p_sb_k, k_sb], axis=1),
                                     jnp.concatenate([p_sb_v, v_sb], axis=1), q_pos, k_pos)
        o_fx = forgetting_attend(q_fx, jnp.concatenate([p_fx_k, k_fx], axis=1),
                                 jnp.concatenate([p_fx_v, v_fx], axis=1), c[:, past_len:], c, q_pos, k_pos)
    o = jnp.concatenate([o_sb, o_fx], axis=2).reshape(b, t, SB_W + FOX_W)
    return o @ w_out, (k_sb, v_sb, k_fx, v_fx, logf)


def mixer_c(h, past, past_len, w_in, sinks, w_out):
    b, t = h.shape[:2]
    q, k, v = jnp.split(h @ w_in, [SWA_Q_W, SWA_Q_W + SWA_KV_W], axis=-1)
    q = q.reshape(b, t, N_SWA_KV_HEADS, SWA_GROUP, HEAD_DIM)
    k = k.reshape(b, t, N_SWA_KV_HEADS, HEAD_DIM)
    v = v.reshape(b, t, N_SWA_KV_HEADS, HEAD_DIM)
    if past is None:
        nc = t // CHUNK

        def band(a):
            a = a.reshape(b, nc, CHUNK, N_SWA_KV_HEADS, HEAD_DIM)
            a = jnp.concatenate([jnp.zeros((b, WINDOW_CHUNKS, CHUNK, N_SWA_KV_HEADS, HEAD_DIM), a.dtype), a], axis=1)
            return jnp.concatenate([a[:, j:j + nc] for j in range(WINDOW_CHUNKS + 1)], axis=2)

        chunk_start = jnp.arange(nc)[:, None] * CHUNK
        q_pos = chunk_start + jnp.arange(CHUNK)[None, :]
        k_pos = chunk_start + (jnp.arange(BAND) - WINDOW_CHUNKS * CHUNK)[None, :]
        qc = q.reshape(b, nc, CHUNK, N_SWA_KV_HEADS, SWA_GROUP, HEAD_DIM)
        o = lax.map(lambda a: swa_attend(a[0], a[1], a[2], q_pos, k_pos, sinks), (qc, band(k), band(v)))
        new_k, new_v = k[:, t - SWA_CACHE:], v[:, t - SWA_CACHE:]
    else:
        p_k, p_v = past
        k_all = jnp.concatenate([p_k, k], axis=1)
        v_all = jnp.concatenate([p_v, v], axis=1)
        q_pos = past_len + jnp.arange(t)
        k_pos = past_len - SWA_CACHE + jnp.arange(SWA_CACHE + t)
        o = swa_attend(q, k_all, v_all, q_pos, k_pos, sinks)
        new_k, new_v = k_all[:, t:], v_all[:, t:]
    return o.reshape(b, t, SWA_Q_W) @ w_out, (new_k, new_v)


def moe_swiglu(x, w_router, w_gate, w_up, w_down):
    shape = x.shape
    d = shape[-1]
    xf = x.reshape(-1, d)
    n_tok = xf.shape[0]
    logits = (xf @ w_router).astype(jnp.float32)
    top_logits, top_idx = lax.top_k(logits, TOP_K)
    gates = jax.nn.softmax(top_logits, axis=-1).reshape(-1)
    expert = top_idx.reshape(-1)
    token = jnp.repeat(jnp.arange(n_tok, dtype=jnp.int32), TOP_K)
    n_assign = n_tok * TOP_K
    order = jnp.argsort(expert)
    expert_sorted = expert[order]
    counts = jnp.bincount(expert, length=N_EXPERTS)
    starts = jnp.cumsum(counts) - counts
    padded = (counts + MOE_BLOCK - 1) // MOE_BLOCK * MOE_BLOCK
    padded_end = jnp.cumsum(padded)
    padded_start = padded_end - padded
    dest = padded_start[expert_sorted] + jnp.arange(n_assign) - starts[expert_sorted]
    n_blocks = -(-n_assign // MOE_BLOCK) + N_EXPERTS
    cap = n_blocks * MOE_BLOCK
    slot_token = jnp.full((cap,), n_tok, jnp.int32).at[dest].set(token[order])
    slot_gate = jnp.zeros((cap,), jnp.float32).at[dest].set(gates[order])
    block_expert = jnp.minimum(jnp.searchsorted(padded_end, jnp.arange(n_blocks) * MOE_BLOCK, side='right'), N_EXPERTS - 1)
    x_pad = jnp.concatenate([xf, jnp.zeros((1, d), xf.dtype)], axis=0)
    x_slots = x_pad[slot_token].reshape(n_blocks, MOE_BLOCK, d)

    def expert_block(args):
        xb, e = args
        return swiglu(xb, w_gate[e], w_up[e], w_down[e])

    y_slots = lax.map(expert_block, (x_slots, block_expert)).reshape(cap, d)
    y_slots = (y_slots.astype(jnp.float32) * slot_gate[:, None]).astype(x.dtype)
    y = jnp.zeros((n_tok + 1, d), x.dtype).at[slot_token].add(y_slots)
    return y[:n_tok].reshape(shape)


def setup_inputs(seed: int = 0) -> dict:
    key = jax.random.key(seed)
    ks = jax.random.split(key, 27)
    d = D_MODEL

    def nrm(i, shape, scale=1.0):
        return scale * jax.random.normal(ks[i], shape, jnp.float32)

    return {
        'x_prompt': nrm(0, (BATCH, SEQ, d)),
        'x_sample': nrm(1, (DEC_BATCH, DEC_SEQ, d)),
        'cache_sb_k': nrm(2, (N_AB_LAYERS, DEC_BATCH, PAST_LEN, N_SB_HEADS, HEAD_DIM)),
        'cache_sb_v': nrm(3, (N_AB_LAYERS, DEC_BATCH, PAST_LEN, N_SB_HEADS, HEAD_DIM)),
        'cache_fox_k': nrm(4, (N_AB_LAYERS, DEC_BATCH, PAST_LEN, N_FOX_HEADS, HEAD_DIM)),
        'cache_fox_v': nrm(5, (N_AB_LAYERS, DEC_BATCH, PAST_LEN, N_FOX_HEADS, HEAD_DIM)),
        'cache_fox_logf': jax.nn.log_sigmoid(FORGET_BIAS_MEAN + nrm(6, (N_AB_LAYERS, DEC_BATCH, PAST_LEN, N_FOX_HEADS), 0.5)),
        'cache_swa_k': nrm(7, (N_C_LAYERS, DEC_BATCH, SWA_CACHE, N_SWA_KV_HEADS, HEAD_DIM)),
        'cache_swa_v': nrm(8, (N_C_LAYERS, DEC_BATCH, SWA_CACHE, N_SWA_KV_HEADS, HEAD_DIM)),
        'norm_mix_ab': 1.0 + nrm(9, (N_AB_LAYERS, d), 0.02),
        'w_in_ab': nrm(10, (N_AB_LAYERS, d, MIX_AB_IN), d ** -0.5),
        'b_forget': FORGET_BIAS_MEAN + nrm(11, (N_AB_LAYERS, N_FOX_HEADS), 0.1),
        'w_out_ab': nrm(12, (N_AB_LAYERS, SB_W + FOX_W, d), (SB_W + FOX_W) ** -0.5),
        'norm_ffn_dense': 1.0 + nrm(13, (N_AB_LAYERS, d), 0.02),
        'w_gate_dense': nrm(14, (N_AB_LAYERS, d, D_FF), d ** -0.5),
        'w_up_dense': nrm(15, (N_AB_LAYERS, d, D_FF), d ** -0.5),
        'w_down_dense': nrm(16, (N_AB_LAYERS, D_FF, d), D_FF ** -0.5),
        'norm_mix_c': 1.0 + nrm(17, (N_C_LAYERS, d), 0.02),
        'w_in_c': nrm(18, (N_C_LAYERS, d, MIX_C_IN), d ** -0.5),
        'sinks': nrm(19, (N_C_LAYERS, N_SWA_HEADS), 0.5),
        'w_out_c': nrm(20, (N_C_LAYERS, SWA_Q_W, d), SWA_Q_W ** -0.5),
        'norm_ffn_moe': 1.0 + nrm(21, (N_C_LAYERS, d), 0.02),
        'w_router': nrm(22, (N_C_LAYERS, d, N_EXPERTS), d ** -0.5),
        'w_gate_moe': nrm(23, (N_C_LAYERS, N_EXPERTS, d, D_EXPERT), d ** -0.5),
        'w_up_moe': nrm(24, (N_C_LAYERS, N_EXPERTS, d, D_EXPERT), d ** -0.5),
        'w_down_moe': nrm(25, (N_C_LAYERS, N_EXPERTS, D_EXPERT, d), D_EXPERT ** -0.5),
        'norm_final': 1.0 + nrm(26, (d,), 0.02),
    }


def reference(x_prompt, x_sample, cache_sb_k, cache_sb_v, cache_fox_k, cache_fox_v, cache_fox_logf,
              cache_swa_k, cache_swa_v, norm_mix_ab, w_in_ab, b_forget, w_out_ab, norm_ffn_dense,
              w_gate_dense, w_up_dense, w_down_dense, norm_mix_c, w_in_c, sinks, w_out_c, norm_ffn_moe,
              w_router, w_gate_moe, w_up_moe, w_down_moe, norm_final):
    past_len = cache_sb_k.shape[2]

    def run(x, use_cache):
        ab_states, c_states = [], []
        for layer in range(DEPTH):
            i = layer // 2
            if layer % 2 == 0:
                past = (cache_sb_k[i], cache_sb_v[i], cache_fox_k[i], cache_fox_v[i], cache_fox_logf[i]) if use_cache else None
                o, st = mixer_ab(rms_norm(x, norm_mix_ab[i]), past, past_len, w_in_ab[i], b_forget[i], w_out_ab[i])
                x = x + o
                x = x + swiglu(rms_norm(x, norm_ffn_dense[i]), w_gate_dense[i], w_up_dense[i], w_down_dense[i])
                ab_states.append(st)
            else:
                past = (cache_swa_k[i], cache_swa_v[i]) if use_cache else None
                o, st = mixer_c(rms_norm(x, norm_mix_c[i]), past, past_len, w_in_c[i], sinks[i], w_out_c[i])
                x = x + o
                x = x + moe_swiglu(rms_norm(x, norm_ffn_moe[i]), w_router[i], w_gate_moe[i], w_up_moe[i], w_down_moe[i])
                c_states.append(st)
        return rms_norm(x, norm_final), ab_states, c_states

    y_prompt, p_ab, p_c = run(x_prompt, False)
    y_sample, s_ab, s_c = run(x_sample, True)

    def stack(states, j):
        return jnp.stack([st[j] for st in states])

    return (y_prompt, y_sample,
            stack(p_ab, 0), stack(p_ab, 1), stack(p_ab, 2), stack(p_ab, 3), stack(p_ab, 4), stack(p_c, 0), stack(p_c, 1),
            stack(s_ab, 0), stack(s_ab, 1), stack(s_ab, 2), stack(s_ab, 3), stack(s_ab, 4), stack(s_c, 0), stack(s_c, 1))
```

```python
import functools

import jax
import jax.numpy as jnp
from jax import lax
from jax.experimental import pallas as pl
from jax.experimental.pallas import tpu as pltpu

F32 = jnp.float32
BF16 = jnp.bfloat16

HEAD_DIM = 64
LANES = 128
N_SB_HEADS = 8
N_FOX_HEADS = 8
N_SWA_HEADS = 16
N_SWA_KV_HEADS = 2
SWA_GROUP = N_SWA_HEADS // N_SWA_KV_HEADS
CHUNK = 64
WINDOW_CHUNKS = 2
SWA_CACHE = 128
N_EXPERTS = 8
RMS_EPS = 1e-6
NEG_INF = -1e30
ATTN_SCALE = HEAD_DIM ** -0.5
SB_W = N_SB_HEADS * HEAD_DIM
FOX_W = N_FOX_HEADS * HEAD_DIM
SWA_Q_W = N_SWA_HEADS * HEAD_DIM
SWA_KV_W = N_SWA_KV_HEADS * HEAD_DIM

VMEM_LIMIT_BYTES = 56 * 1024 * 1024
ROW_TILE = 512
ATTN_TILE = 256
EXPERT_ROWS = 1024
GATHER_ROWS = 1024
NT_DIMS = (((1,), (1,)), ((), ()))


def _params(sem):
    return pltpu.CompilerParams(dimension_semantics=sem, vmem_limit_bytes=VMEM_LIMIT_BYTES)


def _resident(shape):
    n = len(shape)
    return pl.BlockSpec(shape, lambda *_: (0,) * n, pipeline_mode=pl.Buffered(1))


def _rms(x, g):
    return x * lax.rsqrt(jnp.mean(x * x, axis=-1, keepdims=True) + RMS_EPS) * g


def _log_sigmoid(z):
    return jnp.minimum(z, 0.0) - jnp.log(1.0 + jnp.exp(-jnp.abs(z)))


def _silu(a):
    return a * (1.0 / (1.0 + jnp.exp(-a)))


def _dot(a, b):
    return jnp.dot(a, b, preferred_element_type=F32)


def _dot_nt(a, b):
    return lax.dot_general(a, b, NT_DIMS, preferred_element_type=F32)


def _split3(x):
    hi = x.astype(BF16)
    r = x - hi.astype(F32)
    mid = r.astype(BF16)
    lo = (r - mid.astype(F32)).astype(BF16)
    return hi, mid, lo


def _proj_ab_kernel(x_ref, g_ref, w_ref, wf_ref, bf_ref,
                    qsb_ref, ksb_ref, vsb_ref, qfx_ref, kfx_ref, vfx_ref,
                    ksb16_ref, vsb16_ref, kfx16_ref, vfx16_ref, logf_ref):
    hn = _rms(x_ref[...], g_ref[...]).astype(BF16)
    f32_outs = (None, ksb_ref, vsb_ref, None, kfx_ref, vfx_ref)
    b16_outs = (qsb_ref, ksb16_ref, vsb16_ref, qfx_ref, kfx16_ref, vfx16_ref)
    for c in range(6):
        p = _dot(hn, w_ref[:, c * SB_W:(c + 1) * SB_W])
        if f32_outs[c] is None:
            b16_outs[c][...] = (p * ATTN_SCALE).astype(BF16)
        else:
            f32_outs[c][...] = p
            b16_outs[c][...] = p.astype(BF16)
    f = _dot(hn, wf_ref[...])[:, :N_FOX_HEADS] + bf_ref[...]
    logf_ref[...] = _log_sigmoid(f)


def _proj_ab(x, g, w, wf, bf):
    n, d = x.shape
    tm = min(ROW_TILE, n)
    row = lambda c: pl.BlockSpec((tm, c), lambda i: (i, 0))
    f32o = jax.ShapeDtypeStruct((n, SB_W), F32)
    b16o = jax.ShapeDtypeStruct((n, SB_W), BF16)
    return pl.pallas_call(
        _proj_ab_kernel,
        grid=(n // tm,),
        in_specs=[row(d), _resident((1, d)), _resident(w.shape), _resident(wf.shape), _resident((1, N_FOX_HEADS))],
        out_specs=[row(SB_W)] * 10 + [row(N_FOX_HEADS)],
        out_shape=[b16o, f32o, f32o, b16o, f32o, f32o, b16o, b16o, b16o, b16o,
                   jax.ShapeDtypeStruct((n, N_FOX_HEADS), F32)],
        compiler_params=_params(("parallel",)),
        name="proj_ab",
    )(x, g, w, wf, bf)


def _cumsum_kernel(x_ref, o_ref):
    t = x_ref.shape[-1]
    r = lax.broadcasted_iota(jnp.int32, (LANES, LANES), 0)
    c = lax.broadcasted_iota(jnp.int32, (LANES, LANES), 1)
    tri = (r <= c).astype(BF16)
    carry = jnp.zeros((x_ref.shape[1], 1), F32)
    for i in range(t // LANES):
        hi, mid, lo = _split3(x_ref[0, :, i * LANES:(i + 1) * LANES])
        cs = _dot(hi, tri) + _dot(mid, tri) + _dot(lo, tri) + carry
        o_ref[0, :, i * LANES:(i + 1) * LANES] = cs
        carry = cs[:, LANES - 1:LANES]


def _cumsum_time(x):
    b, h, t = x.shape
    spec = pl.BlockSpec((1, h, t), lambda i: (i, 0, 0))
    return pl.pallas_call(
        _cumsum_kernel, grid=(b,), in_specs=[spec], out_specs=spec,
        out_shape=jax.ShapeDtypeStruct(x.shape, F32),
        compiler_params=_params(("parallel",)), name="cumsum_time",
    )(x)


def _head_halves(q):
    lane = lax.broadcasted_iota(jnp.int32, q.shape, 1)
    zero = jnp.zeros_like(q)
    return lane, (jnp.where(lane < HEAD_DIM, q, zero), jnp.where(lane >= HEAD_DIM, q, zero))


def _strict_upper(n):
    j = lax.broadcasted_iota(jnp.int32, (n, n), 0)
    s = lax.broadcasted_iota(jnp.int32, (n, n), 1)
    return (j > s).astype(BF16)


def _sb_kernel(q_ref, kd_ref, vd_ref, kp_ref, vp_ref, o_ref, *, tk, base, per_q):
    tq = q_ref.shape[1]
    lane, qh = _head_halves(q_ref[0])
    row = lax.broadcasted_iota(jnp.int32, (tq, tq), 0)
    col = lax.broadcasted_iota(jnp.int32, (tq, tq), 1)
    vis = col < row
    u_diag = _strict_upper(tq)
    u_past = u_diag if tk == tq else _strict_upper(tk)
    kd = kd_ref[0]
    vd = vd_ref[0]
    accs, sums = [], []
    for h in range(2):
        z = _dot_nt(qh[h], kd)
        ls = _log_sigmoid(z)
        lk = jnp.where(vis, ls - z, 0.0).astype(BF16)
        after = _dot(lk, u_diag)
        w = jnp.where(vis, jnp.exp(ls + after), 0.0)
        sums.append(after[:, 0:1] + lk[:, 0:1].astype(F32))
        accs.append(_dot(w.astype(BF16), vd))

    n_past = base + pl.program_id(2) * per_q

    def body(jj, carry):
        accs, sums = carry
        start = pl.multiple_of((n_past - 1 - jj) * tk, tk)
        k = kp_ref[0, pl.ds(start, tk), :].astype(BF16)
        v = vp_ref[0, pl.ds(start, tk), :].astype(BF16)
        new_accs, new_sums = [], []
        for h in range(2):
            z = _dot_nt(qh[h], k)
            ls = _log_sigmoid(z)
            lk = (ls - z).astype(BF16)
            after = _dot(lk, u_past)
            w = jnp.exp(ls + after + sums[h])
            new_sums.append(sums[h] + after[:, 0:1] + lk[:, 0:1].astype(F32))
            new_accs.append(accs[h] + _dot(w.astype(BF16), v))
        return tuple(new_accs), tuple(new_sums)

    accs, _ = lax.fori_loop(0, n_past, body, (tuple(accs), tuple(sums)))
    o_ref[0] = jnp.where(lane < HEAD_DIM, accs[0], accs[1]).astype(o_ref.dtype)


def _fox_kernel(q_ref, kd_ref, vd_ref, kp_ref, vp_ref, cq_ref, ckd_ref, ckp_ref, o_ref, *, tk, base, per_q):
    tq = q_ref.shape[1]
    lane, qh = _head_halves(q_ref[0])
    row = lax.broadcasted_iota(jnp.int32, (tq, tq), 0)
    col = lax.broadcasted_iota(jnp.int32, (tq, tq), 1)
    vis = col <= row
    kd = kd_ref[0]
    vd = vd_ref[0]
    cq = [cq_ref[0, 0][:, h:h + 1] for h in range(2)]
    ms, ls, accs = [], [], []
    for h in range(2):
        s = _dot_nt(qh[h], kd) + (cq[h] - ckd_ref[0, 0][h:h + 1, :])
        s = jnp.where(vis, s, NEG_INF)
        m = jnp.max(s, axis=1, keepdims=True)
        p = jnp.exp(s - m)
        ms.append(m)
        ls.append(jnp.sum(p, axis=1, keepdims=True))
        accs.append(_dot(p.astype(BF16), vd))

    n_past = base + pl.program_id(2) * per_q

    def body(j, carry):
        ms, ls, accs = carry
        start = pl.multiple_of(j * tk, tk)
        k = kp_ref[0, pl.ds(start, tk), :].astype(BF16)
        v = vp_ref[0, pl.ds(start, tk), :].astype(BF16)
        ck = ckp_ref[0, 0, j]
        new_ms, new_ls, new_accs = [], [], []
        for h in range(2):
            s = _dot_nt(qh[h], k) + (cq[h] - ck[h:h + 1, :])
            m = jnp.maximum(ms[h], jnp.max(s, axis=1, keepdims=True))
            alpha = jnp.exp(ms[h] - m)
            p = jnp.exp(s - m)
            new_ms.append(m)
            new_ls.append(alpha * ls[h] + jnp.sum(p, axis=1, keepdims=True))
            new_accs.append(alpha * accs[h] + _dot(p.astype(BF16), v))
        return tuple(new_ms), tuple(new_ls), tuple(new_accs)

    _, ls, accs = lax.fori_loop(0, n_past, body, (tuple(ms), tuple(ls), tuple(accs)))
    out = jnp.where(lane < HEAD_DIM, accs[0] / ls[0], accs[1] / ls[1])
    o_ref[0] = out.astype(o_ref.dtype)


def _attn_ab(kernel_fn, q, kd, vd, kp, vp, extra, extra_specs, *, tq, tk, base, per_q, name):
    b, t, w = q.shape
    tp = kp.shape[1]
    grid = (b, w // LANES, t // tq)
    qspec = pl.BlockSpec((1, tq, LANES), lambda bi, hp, i: (bi, i, hp))
    pspec = pl.BlockSpec((1, tp, LANES), lambda bi, hp, i: (bi, 0, hp))
    return pl.pallas_call(
        functools.partial(kernel_fn, tk=tk, base=base, per_q=per_q),
        grid=grid,
        in_specs=[qspec, qspec, qspec, pspec, pspec] + extra_specs,
        out_specs=qspec,
        out_shape=jax.ShapeDtypeStruct(q.shape, BF16),
        compiler_params=_params(("parallel", "parallel", "arbitrary")),
        name=name,
    )(q, kd, vd, kp, vp, *extra)


def _fox_bias_operands(c_q, c_p, tq, tk):
    b, t, h = c_q.shape
    tp = c_p.shape[1]
    hp = h // 2
    cq = c_q.reshape(b, t, hp, 2).transpose(0, 2, 1, 3)
    ckd = c_q.reshape(b, t, hp, 2).transpose(0, 2, 3, 1)
    ckp = c_p.reshape(b, tp // tk, tk, hp, 2).transpose(0, 3, 1, 4, 2)
    specs = [pl.BlockSpec((1, 1, tq, 2), lambda bi, p, i: (bi, p, i, 0)),
             pl.BlockSpec((1, 1, 2, tq), lambda bi, p, i: (bi, p, 0, i)),
             pl.BlockSpec((1, 1, tp // tk, 2, tk), lambda bi, p, i: (bi, p, 0, 0, 0))]
    return [cq, ckd, ckp], specs


def _ffn_dense_kernel(x_ref, osb_ref, ofx_ref, wo_ref, g_ref, wg_ref, wu_ref, wd_ref, o_ref, *, tf):
    o_ref[...] = x_ref[...] + _dot(osb_ref[...], wo_ref[:SB_W, :]) + _dot(ofx_ref[...], wo_ref[SB_W:, :])
    hn = _rms(o_ref[...], g_ref[...]).astype(BF16)
    for c in range(wg_ref.shape[1] // tf):
        a = _dot(hn, wg_ref[:, c * tf:(c + 1) * tf])
        u = _dot(hn, wu_ref[:, c * tf:(c + 1) * tf])
        o_ref[...] += _dot((_silu(a) * u).astype(BF16), wd_ref[c * tf:(c + 1) * tf, :])


def _ffn_dense(x, osb, ofx, wo, g, wg, wu, wd):
    n, d = x.shape
    tm = min(ROW_TILE, n)
    row = lambda c: pl.BlockSpec((tm, c), lambda i: (i, 0))
    return pl.pallas_call(
        functools.partial(_ffn_dense_kernel, tf=256),
        grid=(n // tm,),
        in_specs=[row(d), row(SB_W), row(FOX_W), _resident(wo.shape), _resident((1, d)),
                  _resident(wg.shape), _resident(wu.shape), _resident(wd.shape)],
        out_specs=row(d),
        out_shape=jax.ShapeDtypeStruct((n, d), F32),
        compiler_params=_params(("parallel",)),
        name="ffn_dense",
    )(x, osb, ofx, wo, g, wg, wu, wd)


def _proj_c_kernel(x_ref, g_ref, w_ref, q_ref, k_ref, v_ref, k16_ref, v16_ref):
    hn = _rms(x_ref[...], g_ref[...]).astype(BF16)
    q_ref[...] = (_dot(hn, w_ref[:, :SWA_Q_W]) * ATTN_SCALE).astype(BF16)
    k = _dot(hn, w_ref[:, SWA_Q_W:SWA_Q_W + SWA_KV_W])
    v = _dot(hn, w_ref[:, SWA_Q_W + SWA_KV_W:])
    k_ref[...] = k
    v_ref[...] = v
    k16_ref[...] = k.astype(BF16)
    v16_ref[...] = v.astype(BF16)


def _proj_c(x, g, w):
    n, d = x.shape
    tm = min(ROW_TILE, n)
    row = lambda c: pl.BlockSpec((tm, c), lambda i: (i, 0))
    kv32 = jax.ShapeDtypeStruct((n, SWA_KV_W), F32)
    kv16 = jax.ShapeDtypeStruct((n, SWA_KV_W), BF16)
    return pl.pallas_call(
        _proj_c_kernel,
        grid=(n // tm,),
        in_specs=[row(d), _resident((1, d)), _resident(w.shape)],
        out_specs=[row(SWA_Q_W)] + [row(SWA_KV_W)] * 4,
        out_shape=[jax.ShapeDtypeStruct((n, SWA_Q_W), BF16), kv32, kv32, kv16, kv16],
        compiler_params=_params(("parallel",)),
        name="proj_c",
    )(x, g, w)


def _swap_halves(a):
    return jnp.concatenate([a[:, HEAD_DIM:], a[:, :HEAD_DIM]], axis=1)


def _swa_kernel(sink_ref, q_ref, kprev_ref, vprev_ref, kcur_ref, vcur_ref, o_ref, *, pos0):
    tq = q_ref.shape[1]
    tw = SWA_CACHE + tq
    first = pos0 + pl.program_id(1) * tq
    kw = jnp.concatenate([kprev_ref[0].astype(BF16), kcur_ref[0].astype(BF16)], axis=0)
    vw = jnp.concatenate([vprev_ref[0].astype(BF16), vcur_ref[0].astype(BF16)], axis=0)
    k_by_half = (kw, _swap_halves(kw))
    v_by_half = (vw, _swap_halves(vw))
    qpos = first + lax.broadcasted_iota(jnp.int32, (tq, tw), 0)
    kpos = first - SWA_CACHE + lax.broadcasted_iota(jnp.int32, (tq, tw), 1)
    qc = jnp.right_shift(qpos, 6)
    kc = jnp.right_shift(kpos, 6)
    vis = (kpos >= 0) & (kc <= qc) & (kc >= qc - WINDOW_CHUNKS)
    dist = jnp.abs(qpos - kpos).astype(F32)
    lane = lax.broadcasted_iota(jnp.int32, (tq, LANES), 1)
    outs = []
    for h in range(N_SWA_HEADS):
        g, half = h // SWA_GROUP, h % 2
        q128 = q_ref[0, :, (h // 2) * LANES:(h // 2 + 1) * LANES]
        keep = (lane < HEAD_DIM) if half == 0 else (lane >= HEAD_DIM)
        qm = jnp.where(keep, q128, jnp.zeros_like(q128))
        slope = 2.0 ** (-8.0 * (h + 1) / N_SWA_HEADS)
        s = _dot_nt(qm, k_by_half[g ^ half]) - slope * dist
        s = jnp.where(vis, s, NEG_INF)
        sink = sink_ref[h]
        m = jnp.maximum(jnp.max(s, axis=1, keepdims=True), sink)
        p = jnp.exp(s - m)
        l = jnp.sum(p, axis=1, keepdims=True) + jnp.exp(sink - m)
        outs.append(_dot(p.astype(BF16), v_by_half[g ^ half]) / l)
        if half == 1:
            o_ref[0, :, (h // 2) * LANES:(h // 2 + 1) * LANES] = jnp.where(
                lane < HEAD_DIM, outs[-2], outs[-1]).astype(o_ref.dtype)


def _swa(sinks, q, kprev, vprev, kcur, vcur, *, tq, pos0, prev_is_cache):
    b, t, _ = q.shape
    per = tq // SWA_CACHE
    prev_map = ((lambda bi, i, s: (bi, 0, 0)) if prev_is_cache
                else (lambda bi, i, s: (bi, jnp.maximum(i * per - 1, 0), 0)))
    cur = lambda w: pl.BlockSpec((1, tq, w), lambda bi, i, s: (bi, i, 0))
    prev = pl.BlockSpec((1, SWA_CACHE, SWA_KV_W), prev_map)
    return pl.pallas_call(
        functools.partial(_swa_kernel, pos0=pos0),
        grid_spec=pltpu.PrefetchScalarGridSpec(
            num_scalar_prefetch=1, grid=(b, t // tq),
            in_specs=[cur(SWA_Q_W), prev, prev, cur(SWA_KV_W), cur(SWA_KV_W)],
            out_specs=cur(SWA_Q_W)),
        out_shape=jax.ShapeDtypeStruct(q.shape, BF16),
        compiler_params=_params(("parallel", "arbitrary")),
        name="swa",
    )(sinks, q, kprev, vprev, kcur, vcur)


def _route_kernel(x_ref, o_ref, wo_ref, g_ref, wr_ref, x2_ref, xn_ref, route_ref):
    x2 = x_ref[...] + _dot(o_ref[...], wo_ref[...])
    xn = _rms(x2, g_ref[...])
    x2_ref[...] = x2
    xn_ref[...] = xn
    hi = xn.astype(BF16)
    mid = (xn - hi.astype(F32)).astype(BF16)
    d = xn.shape[1]
    logits = _dot(hi, wr_ref[:d, :]) + _dot(hi, wr_ref[d:, :]) + _dot(mid, wr_ref[:d, :])
    lane = lax.broadcasted_iota(jnp.int32, logits.shape, 1).astype(F32)
    lg = jnp.where(lane < N_EXPERTS, logits, -jnp.inf)
    m1 = jnp.max(lg, axis=1, keepdims=True)
    i1 = jnp.min(jnp.where(lg == m1, lane, float(LANES)), axis=1, keepdims=True)
    lg2 = jnp.where(lane == i1, -jnp.inf, lg)
    m2 = jnp.max(lg2, axis=1, keepdims=True)
    i2 = jnp.min(jnp.where(lg2 == m2, lane, float(LANES)), axis=1, keepdims=True)
    e = jnp.exp(m2 - m1)
    g1 = 1.0 / (1.0 + e)
    g2 = e / (1.0 + e)
    route = jnp.where(lane == 0, i1, jnp.where(lane == 1, i2, jnp.where(lane == 2, g1, jnp.where(lane == 3, g2, 0.0))))
    route_ref[...] = route


def _route(x, o, wo, g, wr):
    n, d = x.shape
    tm = min(ROW_TILE, n)
    row = lambda c: pl.BlockSpec((tm, c), lambda i: (i, 0))
    xs = jax.ShapeDtypeStruct((n, d), F32)
    return pl.pallas_call(
        _route_kernel,
        grid=(n // tm,),
        in_specs=[row(d), row(SWA_Q_W), _resident(wo.shape), _resident((1, d)), _resident(wr.shape)],
        out_specs=[row(d), row(d), row(LANES)],
        out_shape=[xs, xs, jax.ShapeDtypeStruct((n, LANES), F32)],
        compiler_params=_params(("parallel",)),
        name="route",
    )(x, o, wo, g, wr)


def _row_copy(src_hbm, dst_ref, src_row, dst_row, sem):
    return pltpu.make_async_copy(src_hbm.at[pl.ds(src_row, 1), :], dst_ref.at[pl.ds(dst_row, 1), :], sem)


def _gather_kernel(idx_hbm, x_hbm, o_ref, idx_smem, idx_sem, row_sem):
    rows = o_ref.shape[0]
    stage = pltpu.make_async_copy(idx_hbm.at[pl.ds(pl.program_id(0) * rows, rows)], idx_smem, idx_sem)
    stage.start()
    stage.wait()

    @pl.loop(0, rows)
    def _(r):
        _row_copy(x_hbm, o_ref, idx_smem[r], r, row_sem).start()

    @pl.loop(0, rows)
    def _(r):
        _row_copy(x_hbm, o_ref, 0, r, row_sem).wait()


def _gather_rows(idx, x):
    n = idx.shape[0]
    d = x.shape[1]
    return pl.pallas_call(
        _gather_kernel,
        grid=(n // GATHER_ROWS,),
        in_specs=[pl.BlockSpec(memory_space=pl.ANY), pl.BlockSpec(memory_space=pl.ANY)],
        out_specs=pl.BlockSpec((GATHER_ROWS, d), lambda i: (i, 0)),
        out_shape=jax.ShapeDtypeStruct((n, d), x.dtype),
        scratch_shapes=[pltpu.SMEM((GATHER_ROWS,), jnp.int32), pltpu.SemaphoreType.DMA, pltpu.SemaphoreType.DMA],
        compiler_params=_params(("arbitrary",)),
        name="moe_gather",
    )(idx, x)


def _expert_kernel(be_ref, nv_ref, x_ref, wg_ref, wu_ref, wd_ref, o_ref, xb_ref, acc_ref):
    blk, f = pl.program_id(0), pl.program_id(1)
    last = pl.num_programs(1) - 1
    valid = blk < nv_ref[0]

    @pl.when(valid & (f == 0))
    def _():
        xb_ref[...] = x_ref[...].astype(BF16)
        acc_ref[...] = jnp.zeros_like(acc_ref)

    @pl.when(valid)
    def _():
        xb = xb_ref[...]
        h = (_silu(_dot(xb, wg_ref[0])) * _dot(xb, wu_ref[0])).astype(BF16)
        acc_ref[...] += _dot(h, wd_ref[0])

    @pl.when(valid & (f == last))
    def _():
        o_ref[...] = acc_ref[...]

    @pl.when(jnp.logical_not(valid) & (f == last))
    def _():
        o_ref[...] = jnp.zeros_like(o_ref)


def _experts(block_expert, n_valid, x_slots, wg, wu, wd, *, tm, tf):
    cap, d = x_slots.shape
    nf = wg.shape[2] // tf
    fsel = lambda b, f, nv: jnp.where(b < nv[0], f, nf - 1)
    return pl.pallas_call(
        _expert_kernel,
        grid_spec=pltpu.PrefetchScalarGridSpec(
            num_scalar_prefetch=2, grid=(cap // tm, nf),
            in_specs=[pl.BlockSpec((tm, d), lambda b, f, be, nv: (b, 0)),
                      pl.BlockSpec((1, d, tf), lambda b, f, be, nv: (be[b], 0, fsel(b, f, nv))),
                      pl.BlockSpec((1, d, tf), lambda b, f, be, nv: (be[b], 0, fsel(b, f, nv))),
                      pl.BlockSpec((1, tf, d), lambda b, f, be, nv: (be[b], fsel(b, f, nv), 0))],
            out_specs=pl.BlockSpec((tm, d), lambda b, f, be, nv: (b, 0)),
            scratch_shapes=[pltpu.VMEM((tm, d), BF16), pltpu.VMEM((tm, d), F32)]),
        out_shape=jax.ShapeDtypeStruct((cap, d), F32),
        compiler_params=_params(("arbitrary", "arbitrary")),
        name="moe_experts",
    )(block_expert, n_valid, x_slots, wg, wu, wd)


def _combine_kernel(dest_hbm, y_hbm, x_ref, route_ref, g_ref, o_ref, idx_smem, y1_ref, y2_ref, idx_sem, row_sem):
    tm = x_ref.shape[0]
    stage = pltpu.make_async_copy(dest_hbm.at[pl.ds(pl.program_id(0) * 2 * tm, 2 * tm)], idx_smem, idx_sem)
    stage.start()
    stage.wait()

    @pl.loop(0, tm)
    def _(r):
        _row_copy(y_hbm, y1_ref, idx_smem[2 * r], r, row_sem).start()
        _row_copy(y_hbm, y2_ref, idx_smem[2 * r + 1], r, row_sem).start()

    @pl.loop(0, tm)
    def _(r):
        _row_copy(y_hbm, y1_ref, 0, r, row_sem).wait()
        _row_copy(y_hbm, y2_ref, 0, r, row_sem).wait()

    route = route_ref[...]
    y = x_ref[...] + (y1_ref[...] * route[:, 2:3] + y2_ref[...] * route[:, 3:4])
    o_ref[...] = _rms(y, g_ref[...])


def _combine(dest, y_slots, x2, route, g):
    n, d = x2.shape
    tm = min(GATHER_ROWS // 2, n)
    row = lambda c: pl.BlockSpec((tm, c), lambda i: (i, 0))
    return pl.pallas_call(
        _combine_kernel,
        grid=(n // tm,),
        in_specs=[pl.BlockSpec(memory_space=pl.ANY), pl.BlockSpec(memory_space=pl.ANY),
                  row(d), row(LANES), _resident((1, d))],
        out_specs=row(d),
        out_shape=jax.ShapeDtypeStruct((n, d), F32),
        scratch_shapes=[pltpu.SMEM((2 * tm,), jnp.int32), pltpu.VMEM((tm, d), F32), pltpu.VMEM((tm, d), F32),
                        pltpu.SemaphoreType.DMA, pltpu.SemaphoreType.DMA],
        compiler_params=_params(("arbitrary",)),
        name="moe_combine",
    )(dest, y_slots, x2, route, g)


def _dispatch(route, tm):
    n = route.shape[0]
    expert = route[:, :2].astype(jnp.int32).reshape(-1)
    onehot = (expert[:, None] == jnp.arange(N_EXPERTS, dtype=jnp.int32)[None, :]).astype(jnp.int32)
    csum = jnp.cumsum(onehot, axis=0)
    rank = jnp.sum(onehot * csum, axis=1) - 1
    counts = csum[-1]
    padded = (counts + tm - 1) // tm * tm
    pend = jnp.cumsum(padded)
    pstart = pend - padded
    dest = (jnp.sum(onehot * pstart[None, :], axis=1) + rank).astype(jnp.int32)
    step = max(tm, GATHER_ROWS)
    cap = -(-(2 * n + N_EXPERTS * (tm - 1)) // step) * step
    n_blocks = cap // tm
    token = jnp.arange(2 * n, dtype=jnp.int32) // 2
    slot_token = jnp.zeros((n_blocks * tm,), jnp.int32).at[dest].set(token, unique_indices=True)
    block_expert = jnp.minimum(
        jnp.searchsorted(pend, jnp.arange(n_blocks, dtype=jnp.int32) * tm, side="right"), N_EXPERTS - 1
    ).astype(jnp.int32)
    n_valid = (pend[-1:] // tm).astype(jnp.int32)
    return slot_token, dest, block_expert, n_valid


def _run(x, caches, wts, *, tq_ab, tq_c, expert_rows):
    b, t, d = x.shape
    n = b * t
    xf = x.reshape(n, d)
    (qsb, ksb, vsb, qfx, kfx, vfx, ksb16, vsb16, kfx16, vfx16, logf) = _proj_ab(
        xf, wts["g_ab"], wts["w_in_ab"], wts["w_f"], wts["b_f"])
    r3 = lambda a: a.reshape(b, t, -1)
    logf3 = r3(logf)
    if caches is None:
        c = _cumsum_time(logf3.transpose(0, 2, 1)).transpose(0, 2, 1)
        c_q, c_p = c, c
        kp_sb, vp_sb, kp_fx, vp_fx = r3(ksb16), r3(vsb16), r3(kfx16), r3(vfx16)
        tk, base, per_q = tq_ab, 0, 1
    else:
        p_sb_k, p_sb_v, p_fx_k, p_fx_v, p_logf = caches[:5]
        past = p_logf.shape[1]
        pad = (-(past + t)) % LANES
        allf = jnp.concatenate([p_logf.astype(F32), logf3, jnp.zeros((b, pad, N_FOX_HEADS), F32)], axis=1)
        c = _cumsum_time(allf.transpose(0, 2, 1)).transpose(0, 2, 1)
        c_q, c_p = c[:, past:past + t], c[:, :past]
        kp_sb, vp_sb, kp_fx, vp_fx = (a.reshape(b, past, -1) for a in (p_sb_k, p_sb_v, p_fx_k, p_fx_v))
        tk = ATTN_TILE
        base, per_q = past // tk, 0
    kw = dict(tq=tq_ab, tk=tk, base=base, per_q=per_q)
    o_sb = _attn_ab(_sb_kernel, r3(qsb), r3(ksb16), r3(vsb16), kp_sb, vp_sb, [], [], name="attn_sb", **kw)
    extra, extra_specs = _fox_bias_operands(c_q, c_p, tq_ab, tk)
    o_fx = _attn_ab(_fox_kernel, r3(qfx), r3(kfx16), r3(vfx16), kp_fx, vp_fx, extra, extra_specs,
                    name="attn_fox", **kw)
    x1 = _ffn_dense(xf, o_sb.reshape(n, SB_W), o_fx.reshape(n, FOX_W), wts["w_out_ab"], wts["g_ffn"],
                    wts["w_gate"], wts["w_up"], wts["w_down"])

    qc, kc, vc, kc16, vc16 = _proj_c(x1, wts["g_c"], wts["w_in_c"])
    if caches is None:
        kprev, vprev, pos0 = r3(kc16), r3(vc16), 0
        new_k, new_v = r3(kc)[:, t - SWA_CACHE:], r3(vc)[:, t - SWA_CACHE:]
    else:
        p_k, p_v = (a.reshape(b, SWA_CACHE, SWA_KV_W) for a in caches[5:])
        kprev, vprev, pos0 = p_k, p_v, caches[0].shape[1]
        new_k = jnp.concatenate([p_k, r3(kc)], axis=1)[:, t:]
        new_v = jnp.concatenate([p_v, r3(vc)], axis=1)[:, t:]
    o_c = _swa(wts["sinks"], r3(qc), kprev, vprev, r3(kc16), r3(vc16), tq=tq_c, pos0=pos0,
               prev_is_cache=caches is not None)

    x2, xn, route = _route(x1, o_c.reshape(n, SWA_Q_W), wts["w_out_c"], wts["g_moe"], wts["w_router"])
    slot_token, dest, block_expert, n_valid = _dispatch(route, expert_rows)
    x_slots = _gather_rows(slot_token, xn)
    y_slots = _experts(block_expert, n_valid, x_slots, wts["w_gate_moe"], wts["w_up_moe"], wts["w_down_moe"],
                       tm=expert_rows, tf=512)
    y = _combine(dest, y_slots, x2, route, wts["g_final"])

    heads = lambda a, h: a.reshape(1, b, -1, h, HEAD_DIM)
    states = (heads(ksb, N_SB_HEADS), heads(vsb, N_SB_HEADS), heads(kfx, N_FOX_HEADS), heads(vfx, N_FOX_HEADS),
              logf3[None], heads(new_k, N_SWA_KV_HEADS), heads(new_v, N_SWA_KV_HEADS))
    return y.reshape(b, t, d), states


def kernel(x_prompt, x_sample, cache_sb_k, cache_sb_v, cache_fox_k, cache_fox_v, cache_fox_logf, cache_swa_k,
           cache_swa_v, norm_mix_ab, w_in_ab, b_forget, w_out_ab, norm_ffn_dense, w_gate_dense, w_up_dense,
           w_down_dense, norm_mix_c, w_in_c, sinks, w_out_c, norm_ffn_moe, w_router, w_gate_moe, w_up_moe,
           w_down_moe, norm_final):
    d = x_prompt.shape[-1]
    main_w = 3 * SB_W + 3 * FOX_W
    w_f = jnp.zeros((d, LANES), F32).at[:, :N_FOX_HEADS].set(w_in_ab[0][:, main_w:])
    wr = jnp.zeros((d, LANES), F32).at[:, :N_EXPERTS].set(w_router[0])
    wr_hi = wr.astype(BF16)
    wr_mid = (wr - wr_hi.astype(F32)).astype(BF16)
    wts = dict(
        g_ab=norm_mix_ab[0][None], w_in_ab=w_in_ab[0][:, :main_w].astype(BF16), w_f=w_f.astype(BF16),
        b_f=b_forget[0][None], w_out_ab=w_out_ab[0].astype(BF16), g_ffn=norm_ffn_dense[0][None],
        w_gate=w_gate_dense[0].astype(BF16), w_up=w_up_dense[0].astype(BF16), w_down=w_down_dense[0].astype(BF16),
        g_c=norm_mix_c[0][None], w_in_c=w_in_c[0].astype(BF16), sinks=sinks[0], w_out_c=w_out_c[0].astype(BF16),
        g_moe=norm_ffn_moe[0][None], w_router=jnp.concatenate([wr_hi, wr_mid], axis=0),
        w_gate_moe=w_gate_moe[0].astype(BF16), w_up_moe=w_up_moe[0].astype(BF16),
        w_down_moe=w_down_moe[0].astype(BF16), g_final=norm_final[None])
    t_p, t_s = x_prompt.shape[1], x_sample.shape[1]
    y_p, st_p = _run(x_prompt, None, wts, tq_ab=min(ATTN_TILE, t_p), tq_c=min(ATTN_TILE, t_p),
                     expert_rows=EXPERT_ROWS)
    caches = (cache_sb_k[0], cache_sb_v[0], cache_fox_k[0], cache_fox_v[0], cache_fox_logf[0],
              cache_swa_k[0], cache_swa_v[0])
    y_s, st_s = _run(x_sample, caches, wts, tq_ab=t_s, tq_c=t_s, expert_rows=ATTN_TILE)
    return (y_p, y_s) + st_p + st_s
```

```python
import functools

import jax
import jax.numpy as jnp
from jax import lax
from jax.experimental import pallas as pl
from jax.experimental.pallas import tpu as pltpu

F32 = jnp.float32
BF16 = jnp.bfloat16

HEAD_DIM = 64
LANES = 128
N_SB_HEADS = 8
N_FOX_HEADS = 8
N_SWA_HEADS = 16
N_SWA_KV_HEADS = 2
SWA_GROUP = N_SWA_HEADS // N_SWA_KV_HEADS
CHUNK = 64
WINDOW_CHUNKS = 2
SWA_CACHE = 128
N_EXPERTS = 8
RMS_EPS = 1e-6
NEG_INF = -1e30
ATTN_SCALE = HEAD_DIM ** -0.5
SB_W = N_SB_HEADS * HEAD_DIM
FOX_W = N_FOX_HEADS * HEAD_DIM
SWA_Q_W = N_SWA_HEADS * HEAD_DIM
SWA_KV_W = N_SWA_KV_HEADS * HEAD_DIM

VMEM_LIMIT_BYTES = 56 * 1024 * 1024
ROW_TILE = 512
ATTN_TILE = 512
CUMSUM_SUB = 256
SWA_TILE = 256
SAMPLE_EXPERT_ROWS = 256
LOG2_E = 1.4426950408889634
EXPERT_ROWS = 1024
GATHER_ROWS = 1024
NT_DIMS = (((1,), (1,)), ((), ()))


def _params(sem):
    return pltpu.CompilerParams(dimension_semantics=sem, vmem_limit_bytes=VMEM_LIMIT_BYTES)


def _resident(shape):
    n = len(shape)
    return pl.BlockSpec(shape, lambda *_: (0,) * n, pipeline_mode=pl.Buffered(1))


def _rms(x, g):
    return x * lax.rsqrt(jnp.mean(x * x, axis=-1, keepdims=True) + RMS_EPS) * g


def _log_sigmoid(z):
    return jnp.minimum(z, 0.0) - jnp.log(1.0 + jnp.exp(-jnp.abs(z)))


def _silu(a):
    return a * (1.0 / (1.0 + jnp.exp(-a)))


def _dot(a, b):
    return jnp.dot(a, b, preferred_element_type=F32)


def _dot_nt(a, b):
    return lax.dot_general(a, b, NT_DIMS, preferred_element_type=F32)


def _split3(x):
    hi = x.astype(BF16)
    r = x - hi.astype(F32)
    mid = r.astype(BF16)
    lo = (r - mid.astype(F32)).astype(BF16)
    return hi, mid, lo


def _proj_ab_kernel(x_ref, g_ref, w_ref, wf_ref, bf_ref,
                    qsb_ref, ksb_ref, vsb_ref, qfx_ref, kfx_ref, vfx_ref,
                    ksb16_ref, vsb16_ref, kfx16_ref, vfx16_ref, logf_ref):
    hn = _rms(x_ref[...], g_ref[...]).astype(BF16)
    f32_outs = (None, ksb_ref, vsb_ref, None, kfx_ref, vfx_ref)
    b16_outs = (qsb_ref, ksb16_ref, vsb16_ref, qfx_ref, kfx16_ref, vfx16_ref)
    for c in range(6):
        p = _dot(hn, w_ref[:, c * SB_W:(c + 1) * SB_W])
        if f32_outs[c] is None:
            b16_outs[c][...] = (p * ATTN_SCALE).astype(BF16)
        else:
            f32_outs[c][...] = p
            b16_outs[c][...] = p.astype(BF16)
    f = _dot(hn, wf_ref[...])[:, :N_FOX_HEADS] + bf_ref[...]
    logf_ref[...] = _log_sigmoid(f)


def _proj_ab(x, g, w, wf, bf):
    n, d = x.shape
    tm = min(ROW_TILE, n)
    row = lambda c: pl.BlockSpec((tm, c), lambda i: (i, 0))
    f32o = jax.ShapeDtypeStruct((n, SB_W), F32)
    b16o = jax.ShapeDtypeStruct((n, SB_W), BF16)
    return pl.pallas_call(
        _proj_ab_kernel,
        grid=(n // tm,),
        in_specs=[row(d), _resident((1, d)), _resident(w.shape), _resident(wf.shape), _resident((1, N_FOX_HEADS))],
        out_specs=[row(SB_W)] * 10 + [row(N_FOX_HEADS)],
        out_shape=[b16o, f32o, f32o, b16o, f32o, f32o, b16o, b16o, b16o, b16o,
                   jax.ShapeDtypeStruct((n, N_FOX_HEADS), F32)],
        compiler_params=_params(("parallel",)),
        name="proj_ab",
    )(x, g, w, wf, bf)


def _cumsum_kernel(x_ref, o_ref):
    t = x_ref.shape[-1]
    r = lax.broadcasted_iota(jnp.int32, (LANES, LANES), 0)
    c = lax.broadcasted_iota(jnp.int32, (LANES, LANES), 1)
    tri = (r <= c).astype(BF16)
    carry = jnp.zeros((x_ref.shape[1], 1), F32)
    for i in range(t // LANES):
        hi, mid, lo = _split3(x_ref[0, :, i * LANES:(i + 1) * LANES])
        cs = _dot(hi, tri) + _dot(mid, tri) + _dot(lo, tri) + carry
        o_ref[0, :, i * LANES:(i + 1) * LANES] = cs
        carry = cs[:, LANES - 1:LANES]


def _cumsum_time(x):
    b, h, t = x.shape
    spec = pl.BlockSpec((1, h, t), lambda i: (i, 0, 0))
    return pl.pallas_call(
        _cumsum_kernel, grid=(b,), in_specs=[spec], out_specs=spec,
        out_shape=jax.ShapeDtypeStruct(x.shape, F32),
        compiler_params=_params(("parallel",)), name="cumsum_time",
    )(x)


def _head_halves(q):
    lane = lax.broadcasted_iota(jnp.int32, q.shape, 1)
    zero = jnp.zeros_like(q)
    return lane, (jnp.where(lane < HEAD_DIM, q, zero), jnp.where(lane >= HEAD_DIM, q, zero))


def _strict_upper(n):
    j = lax.broadcasted_iota(jnp.int32, (n, n), 0)
    s = lax.broadcasted_iota(jnp.int32, (n, n), 1)
    return (j > s).astype(BF16)


def _sb_block(qh, k, v, total, acc_ref, h, u, vis):
    sub = u.shape[0]
    z = _dot_nt(qh, k) * LOG2_E
    ls = jnp.minimum(z, 0.0) - jnp.log2(1.0 + jnp.exp2(-jnp.abs(z)))
    lk = ls - z
    if vis is not None:
        lk = jnp.where(vis, lk, 0.0)
    lk = lk.astype(BF16)
    for s in reversed(range(k.shape[0] // sub)):
        cols = slice(s * sub, (s + 1) * sub)
        after = _dot(lk[:, cols], u)
        w = jnp.exp2(ls[:, cols] + after + total)
        if vis is not None:
            w = jnp.where(vis[:, cols], w, 0.0)
        acc_ref[h] += _dot(w.astype(BF16), v[cols, :])
        total = total + after[:, 0:1] + lk[:, s * sub:s * sub + 1].astype(F32)
    return total


def _sb_kernel(q_ref, kd_ref, vd_ref, kp_ref, vp_ref, o_ref, acc_ref, *, tk, base, per_q):
    tq, td = q_ref.shape[1], kd_ref.shape[1]
    lane, qh = _head_halves(q_ref[0])
    first_row = (pl.program_id(2) % (td // tq)) * tq
    row = first_row + lax.broadcasted_iota(jnp.int32, (tq, td), 0)
    col = lax.broadcasted_iota(jnp.int32, (tq, td), 1)
    vis = col < row
    u_diag = _strict_upper(min(CUMSUM_SUB, td))
    u_past = _strict_upper(CUMSUM_SUB)
    acc_ref[...] = jnp.zeros_like(acc_ref)
    zero = jnp.zeros((tq, 1), F32)
    totals = tuple(_sb_block(qh[h], kd_ref[0], vd_ref[0], zero, acc_ref, h, u_diag, vis) for h in range(2))

    n_past = base + pl.program_id(2) // (td // tq) * per_q

    def body(jj, totals):
        start = pl.multiple_of((n_past - 1 - jj) * tk, tk)
        k = kp_ref[0, pl.ds(start, tk), :].astype(BF16)
        v = vp_ref[0, pl.ds(start, tk), :].astype(BF16)
        return tuple(_sb_block(qh[h], k, v, totals[h], acc_ref, h, u_past, None) for h in range(2))

    lax.fori_loop(0, n_past, body, totals)
    o_ref[0] = jnp.where(lane < HEAD_DIM, acc_ref[0], acc_ref[1]).astype(o_ref.dtype)


def _fox_block(qh, k, v, ck, m, l, acc_ref, h, vis):
    s = _dot_nt(qh, k) - ck
    if vis is not None:
        s = jnp.where(vis, s, NEG_INF)
    m_new = jnp.maximum(m, jnp.max(s, axis=1, keepdims=True))
    alpha = jnp.exp(m - m_new)
    p = jnp.exp(s - m_new)
    acc_ref[h] = alpha * acc_ref[h] + _dot(p.astype(BF16), v)
    return m_new, alpha * l + jnp.sum(p, axis=1, keepdims=True)


def _fox_kernel(q_ref, kd_ref, vd_ref, kp_ref, vp_ref, ckd_ref, ckp_ref, o_ref, acc_ref, *, tk, base, per_q):
    tq, td = q_ref.shape[1], kd_ref.shape[1]
    lane, qh = _head_halves(q_ref[0])
    first_row = (pl.program_id(2) % (td // tq)) * tq
    row = first_row + lax.broadcasted_iota(jnp.int32, (tq, td), 0)
    col = lax.broadcasted_iota(jnp.int32, (tq, td), 1)
    vis = col <= row
    acc_ref[...] = jnp.zeros_like(acc_ref)
    m0 = jnp.full((tq, 1), NEG_INF, F32)
    l0 = jnp.zeros((tq, 1), F32)
    ckd = ckd_ref[0, 0]
    stats = tuple(_fox_block(qh[h], kd_ref[0], vd_ref[0], ckd[h:h + 1, :], m0, l0, acc_ref, h, vis)
                  for h in range(2))

    n_past = base + pl.program_id(2) // (td // tq) * per_q

    def body(j, stats):
        start = pl.multiple_of(j * tk, tk)
        k = kp_ref[0, pl.ds(start, tk), :].astype(BF16)
        v = vp_ref[0, pl.ds(start, tk), :].astype(BF16)
        ck = ckp_ref[0, 0, j]
        return tuple(_fox_block(qh[h], k, v, ck[h:h + 1, :], stats[h][0], stats[h][1], acc_ref, h, None)
                     for h in range(2))

    stats = lax.fori_loop(0, n_past, body, stats)
    out = jnp.where(lane < HEAD_DIM, acc_ref[0] / stats[0][1], acc_ref[1] / stats[1][1])
    o_ref[0] = out.astype(o_ref.dtype)


def _attn_ab(kernel_fn, q, kd, vd, kp, vp, extra, extra_specs, *, tq, td, tk, base, per_q, name):
    b, t, w = q.shape
    tp = kp.shape[1]
    grid = (b, w // LANES, t // tq)
    qspec = pl.BlockSpec((1, tq, LANES), lambda bi, hp, i: (bi, i, hp))
    dspec = pl.BlockSpec((1, td, LANES), lambda bi, hp, i: (bi, i // (td // tq), hp))
    pspec = pl.BlockSpec((1, tp, LANES), lambda bi, hp, i: (bi, 0, hp))
    return pl.pallas_call(
        functools.partial(kernel_fn, tk=tk, base=base, per_q=per_q),
        grid=grid,
        in_specs=[qspec, dspec, dspec, pspec, pspec] + extra_specs,
        out_specs=qspec,
        out_shape=jax.ShapeDtypeStruct(q.shape, BF16),
        scratch_shapes=[pltpu.VMEM((2, tq, LANES), F32)],
        compiler_params=_params(("parallel", "parallel", "arbitrary")),
        name=name,
    )(q, kd, vd, kp, vp, *extra)


def _fox_bias_operands(c_q, c_p, tq, td, tk):
    b, t, h = c_q.shape
    tp = c_p.shape[1]
    hp = h // 2
    ckd = c_q.reshape(b, t, hp, 2).transpose(0, 2, 3, 1)
    ckp = c_p.reshape(b, tp // tk, tk, hp, 2).transpose(0, 3, 1, 4, 2)
    specs = [pl.BlockSpec((1, 1, 2, td), lambda bi, p, i: (bi, p, 0, i // (td // tq))),
             pl.BlockSpec((1, 1, tp // tk, 2, tk), lambda bi, p, i: (bi, p, 0, 0, 0))]
    return [ckd, ckp], specs


def _ffn_dense_kernel(x_ref, osb_ref, ofx_ref, wo_ref, g_ref, wg_ref, wu_ref, wd_ref, o_ref, *, tf):
    o_ref[...] = x_ref[...] + _dot(osb_ref[...], wo_ref[:SB_W, :]) + _dot(ofx_ref[...], wo_ref[SB_W:, :])
    hn = _rms(o_ref[...], g_ref[...]).astype(BF16)
    for c in range(wg_ref.shape[1] // tf):
        a = _dot(hn, wg_ref[:, c * tf:(c + 1) * tf])
        u = _dot(hn, wu_ref[:, c * tf:(c + 1) * tf])
        o_ref[...] += _dot((_silu(a) * u).astype(BF16), wd_ref[c * tf:(c + 1) * tf, :])


def _ffn_dense(x, osb, ofx, wo, g, wg, wu, wd):
    n, d = x.shape
    tm = min(ROW_TILE, n)
    row = lambda c: pl.BlockSpec((tm, c), lambda i: (i, 0))
    return pl.pallas_call(
        functools.partial(_ffn_dense_kernel, tf=256),
        grid=(n // tm,),
        in_specs=[row(d), row(SB_W), row(FOX_W), _resident(wo.shape), _resident((1, d)),
                  _resident(wg.shape), _resident(wu.shape), _resident(wd.shape)],
        out_specs=row(d),
        out_shape=jax.ShapeDtypeStruct((n, d), F32),
        compiler_params=_params(("parallel",)),
        name="ffn_dense",
    )(x, osb, ofx, wo, g, wg, wu, wd)


def _proj_c_kernel(x_ref, g_ref, w_ref, q_ref, k_ref, v_ref, k16_ref, v16_ref):
    hn = _rms(x_ref[...], g_ref[...]).astype(BF16)
    q_ref[...] = (_dot(hn, w_ref[:, :SWA_Q_W]) * ATTN_SCALE).astype(BF16)
    k = _dot(hn, w_ref[:, SWA_Q_W:SWA_Q_W + SWA_KV_W])
    v = _dot(hn, w_ref[:, SWA_Q_W + SWA_KV_W:])
    k_ref[...] = k
    v_ref[...] = v
    k16_ref[...] = k.astype(BF16)
    v16_ref[...] = v.astype(BF16)


def _proj_c(x, g, w):
    n, d = x.shape
    tm = min(ROW_TILE, n)
    row = lambda c: pl.BlockSpec((tm, c), lambda i: (i, 0))
    kv32 = jax.ShapeDtypeStruct((n, SWA_KV_W), F32)
    kv16 = jax.ShapeDtypeStruct((n, SWA_KV_W), BF16)
    return pl.pallas_call(
        _proj_c_kernel,
        grid=(n // tm,),
        in_specs=[row(d), _resident((1, d)), _resident(w.shape)],
        out_specs=[row(SWA_Q_W)] + [row(SWA_KV_W)] * 4,
        out_shape=[jax.ShapeDtypeStruct((n, SWA_Q_W), BF16), kv32, kv32, kv16, kv16],
        compiler_params=_params(("parallel",)),
        name="proj_c",
    )(x, g, w)


def _swap_halves(a):
    return jnp.concatenate([a[:, HEAD_DIM:], a[:, :HEAD_DIM]], axis=1)


def _swa_kernel(sink_ref, q_ref, kprev_ref, vprev_ref, kcur_ref, vcur_ref, o_ref, *, pos0):
    tq = q_ref.shape[1]
    tw = SWA_CACHE + tq
    first = pos0 + pl.program_id(1) * tq
    kw = jnp.concatenate([kprev_ref[0].astype(BF16), kcur_ref[0].astype(BF16)], axis=0)
    vw = jnp.concatenate([vprev_ref[0].astype(BF16), vcur_ref[0].astype(BF16)], axis=0)
    k_by_half = (kw, _swap_halves(kw))
    v_by_half = (vw, _swap_halves(vw))
    qpos = first + lax.broadcasted_iota(jnp.int32, (tq, tw), 0)
    kpos = first - SWA_CACHE + lax.broadcasted_iota(jnp.int32, (tq, tw), 1)
    qc = jnp.right_shift(qpos, 6)
    kc = jnp.right_shift(kpos, 6)
    vis = (kpos >= 0) & (kc <= qc) & (kc >= qc - WINDOW_CHUNKS)
    dist = jnp.abs(qpos - kpos).astype(F32)
    lane = lax.broadcasted_iota(jnp.int32, (tq, LANES), 1)
    outs = []
    for h in range(N_SWA_HEADS):
        g, half = h // SWA_GROUP, h % 2
        q128 = q_ref[0, :, (h // 2) * LANES:(h // 2 + 1) * LANES]
        keep = (lane < HEAD_DIM) if half == 0 else (lane >= HEAD_DIM)
        qm = jnp.where(keep, q128, jnp.zeros_like(q128))
        slope = 2.0 ** (-8.0 * (h + 1) / N_SWA_HEADS)
        s = _dot_nt(qm, k_by_half[g ^ half]) - slope * dist
        s = jnp.where(vis, s, NEG_INF)
        sink = sink_ref[h]
        m = jnp.maximum(jnp.max(s, axis=1, keepdims=True), sink)
        p = jnp.exp(s - m)
        l = jnp.sum(p, axis=1, keepdims=True) + jnp.exp(sink - m)
        outs.append(_dot(p.astype(BF16), v_by_half[g ^ half]) / l)
        if half == 1:
            o_ref[0, :, (h // 2) * LANES:(h // 2 + 1) * LANES] = jnp.where(
                lane < HEAD_DIM, outs[-2], outs[-1]).astype(o_ref.dtype)


def _swa(sinks, q, kprev, vprev, kcur, vcur, *, tq, pos0, prev_is_cache):
    b, t, _ = q.shape
    per = tq // SWA_CACHE
    prev_map = ((lambda bi, i, s: (bi, 0, 0)) if prev_is_cache
                else (lambda bi, i, s: (bi, jnp.maximum(i * per - 1, 0), 0)))
    cur = lambda w: pl.BlockSpec((1, tq, w), lambda bi, i, s: (bi, i, 0))
    prev = pl.BlockSpec((1, SWA_CACHE, SWA_KV_W), prev_map)
    return pl.pallas_call(
        functools.partial(_swa_kernel, pos0=pos0),
        grid_spec=pltpu.PrefetchScalarGridSpec(
            num_scalar_prefetch=1, grid=(b, t // tq),
            in_specs=[cur(SWA_Q_W), prev, prev, cur(SWA_KV_W), cur(SWA_KV_W)],
            out_specs=cur(SWA_Q_W)),
        out_shape=jax.ShapeDtypeStruct(q.shape, BF16),
        compiler_params=_params(("parallel", "arbitrary")),
        name="swa",
    )(sinks, q, kprev, vprev, kcur, vcur)


def _route_kernel(x_ref, o_ref, wo_ref, g_ref, wr_ref, x2_ref, xn_ref, route_ref):
    x2 = x_ref[...] + _dot(o_ref[...], wo_ref[...])
    xn = _rms(x2, g_ref[...])
    x2_ref[...] = x2
    xn_ref[...] = xn
    hi = xn.astype(BF16)
    mid = (xn - hi.astype(F32)).astype(BF16)
    d = xn.shape[1]
    logits = _dot(hi, wr_ref[:d, :]) + _dot(hi, wr_ref[d:, :]) + _dot(mid, wr_ref[:d, :])
    lane = lax.broadcasted_iota(jnp.int32, logits.shape, 1).astype(F32)
    lg = jnp.where(lane < N_EXPERTS, logits, -jnp.inf)
    m1 = jnp.max(lg, axis=1, keepdims=True)
    i1 = jnp.min(jnp.where(lg == m1, lane, float(LANES)), axis=1, keepdims=True)
    lg2 = jnp.where(lane == i1, -jnp.inf, lg)
    m2 = jnp.max(lg2, axis=1, keepdims=True)
    i2 = jnp.min(jnp.where(lg2 == m2, lane, float(LANES)), axis=1, keepdims=True)
    e = jnp.exp(m2 - m1)
    g1 = 1.0 / (1.0 + e)
    g2 = e / (1.0 + e)
    route = jnp.where(lane == 0, i1, jnp.where(lane == 1, i2, jnp.where(lane == 2, g1, jnp.where(lane == 3, g2, 0.0))))
    route_ref[...] = route


def _route(x, o, wo, g, wr):
    n, d = x.shape
    tm = min(ROW_TILE, n)
    row = lambda c: pl.BlockSpec((tm, c), lambda i: (i, 0))
    xs = jax.ShapeDtypeStruct((n, d), F32)
    return pl.pallas_call(
        _route_kernel,
        grid=(n // tm,),
        in_specs=[row(d), row(SWA_Q_W), _resident(wo.shape), _resident((1, d)), _resident(wr.shape)],
        out_specs=[row(d), row(d), row(LANES)],
        out_shape=[xs, xs, jax.ShapeDtypeStruct((n, LANES), F32)],
        compiler_params=_params(("parallel",)),
        name="route",
    )(x, o, wo, g, wr)


def _row_copy(src_hbm, dst_ref, src_row, dst_row, sem):
    return pltpu.make_async_copy(src_hbm.at[pl.ds(src_row, 1), :], dst_ref.at[pl.ds(dst_row, 1), :], sem)


def _gather_kernel(idx_hbm, x_hbm, o_ref, idx_smem, idx_sem, row_sem):
    rows = o_ref.shape[0]
    stage = pltpu.make_async_copy(idx_hbm.at[pl.ds(pl.program_id(0) * rows, rows)], idx_smem, idx_sem)
    stage.start()
    stage.wait()

    @pl.loop(0, rows)
    def _(r):
        _row_copy(x_hbm, o_ref, idx_smem[r], r, row_sem).start()

    @pl.loop(0, rows)
    def _(r):
        _row_copy(x_hbm, o_ref, 0, r, row_sem).wait()


def _gather_rows(idx, x):
    n = idx.shape[0]
    d = x.shape[1]
    return pl.pallas_call(
        _gather_kernel,
        grid=(n // GATHER_ROWS,),
        in_specs=[pl.BlockSpec(memory_space=pl.ANY), pl.BlockSpec(memory_space=pl.ANY)],
        out_specs=pl.BlockSpec((GATHER_ROWS, d), lambda i: (i, 0)),
        out_shape=jax.ShapeDtypeStruct((n, d), x.dtype),
        scratch_shapes=[pltpu.SMEM((GATHER_ROWS,), jnp.int32), pltpu.SemaphoreType.DMA, pltpu.SemaphoreType.DMA],
        compiler_params=_params(("arbitrary",)),
        name="moe_gather",
    )(idx, x)


def _expert_kernel(be_ref, nv_ref, x_ref, wg_ref, wu_ref, wd_ref, o_ref, xb_ref, acc_ref):
    blk, f = pl.program_id(0), pl.program_id(1)
    last = pl.num_programs(1) - 1
    valid = blk < nv_ref[0]

    @pl.when(valid & (f == 0))
    def _():
        xb_ref[...] = x_ref[...].astype(BF16)
        acc_ref[...] = jnp.zeros_like(acc_ref)

    @pl.when(valid)
    def _():
        xb = xb_ref[...]
        h = (_silu(_dot(xb, wg_ref[0])) * _dot(xb, wu_ref[0])).astype(BF16)
        acc_ref[...] += _dot(h, wd_ref[0])

    @pl.when(valid & (f == last))
    def _():
        o_ref[...] = acc_ref[...]

    @pl.when(jnp.logical_not(valid) & (f == last))
    def _():
        o_ref[...] = jnp.zeros_like(o_ref)


def _experts(block_expert, n_valid, x_slots, wg, wu, wd, *, tm, tf):
    cap, d = x_slots.shape
    nf = wg.shape[2] // tf
    fsel = lambda b, f, nv: jnp.where(b < nv[0], f, nf - 1)
    return pl.pallas_call(
        _expert_kernel,
        grid_spec=pltpu.PrefetchScalarGridSpec(
            num_scalar_prefetch=2, grid=(cap // tm, nf),
            in_specs=[pl.BlockSpec((tm, d), lambda b, f, be, nv: (b, 0)),
                      pl.BlockSpec((1, d, tf), lambda b, f, be, nv: (be[b], 0, fsel(b, f, nv))),
                      pl.BlockSpec((1, d, tf), lambda b, f, be, nv: (be[b], 0, fsel(b, f, nv))),
                      pl.BlockSpec((1, tf, d), lambda b, f, be, nv: (be[b], fsel(b, f, nv), 0))],
            out_specs=pl.BlockSpec((tm, d), lambda b, f, be, nv: (b, 0)),
            scratch_shapes=[pltpu.VMEM((tm, d), BF16), pltpu.VMEM((tm, d), F32)]),
        out_shape=jax.ShapeDtypeStruct((cap, d), F32),
        compiler_params=_params(("arbitrary", "arbitrary")),
        name="moe_experts",
    )(block_expert, n_valid, x_slots, wg, wu, wd)


def _combine_kernel(dest_hbm, y_hbm, x_ref, route_ref, g_ref, o_ref, idx_smem, y1_ref, y2_ref, idx_sem, row_sem):
    tm = x_ref.shape[0]
    stage = pltpu.make_async_copy(dest_hbm.at[pl.ds(pl.program_id(0) * 2 * tm, 2 * tm)], idx_smem, idx_sem)
    stage.start()
    stage.wait()

    @pl.loop(0, tm)
    def _(r):
        _row_copy(y_hbm, y1_ref, idx_smem[2 * r], r, row_sem).start()
        _row_copy(y_hbm, y2_ref, idx_smem[2 * r + 1], r, row_sem).start()

    @pl.loop(0, tm)
    def _(r):
        _row_copy(y_hbm, y1_ref, 0, r, row_sem).wait()
        _row_copy(y_hbm, y2_ref, 0, r, row_sem).wait()

    route = route_ref[...]
    y = x_ref[...] + (y1_ref[...] * route[:, 2:3] + y2_ref[...] * route[:, 3:4])
    o_ref[...] = _rms(y, g_ref[...])


def _combine(dest, y_slots, x2, route, g):
    n, d = x2.shape
    tm = min(GATHER_ROWS // 2, n)
    row = lambda c: pl.BlockSpec((tm, c), lambda i: (i, 0))
    return pl.pallas_call(
        _combine_kernel,
        grid=(n // tm,),
        in_specs=[pl.BlockSpec(memory_space=pl.ANY), pl.BlockSpec(memory_space=pl.ANY),
                  row(d), row(LANES), _resident((1, d))],
        out_specs=row(d),
        out_shape=jax.ShapeDtypeStruct((n, d), F32),
        scratch_shapes=[pltpu.SMEM((2 * tm,), jnp.int32), pltpu.VMEM((tm, d), F32), pltpu.VMEM((tm, d), F32),
                        pltpu.SemaphoreType.DMA, pltpu.SemaphoreType.DMA],
        compiler_params=_params(("arbitrary",)),
        name="moe_combine",
    )(dest, y_slots, x2, route, g)


def _dispatch(route, tm):
    n = route.shape[0]
    expert = route[:, :2].astype(jnp.int32).reshape(-1)
    onehot = (expert[:, None] == jnp.arange(N_EXPERTS, dtype=jnp.int32)[None, :]).astype(jnp.int32)
    csum = jnp.cumsum(onehot, axis=0)
    rank = jnp.sum(onehot * csum, axis=1) - 1
    counts = csum[-1]
    padded = (counts + tm - 1) // tm * tm
    pend = jnp.cumsum(padded)
    pstart = pend - padded
    dest = (jnp.sum(onehot * pstart[None, :], axis=1) + rank).astype(jnp.int32)
    step = max(tm, GATHER_ROWS)
    cap = -(-(2 * n + N_EXPERTS * (tm - 1)) // step) * step
    n_blocks = cap // tm
    token = jnp.arange(2 * n, dtype=jnp.int32) // 2
    slot_token = jnp.zeros((n_blocks * tm,), jnp.int32).at[dest].set(token, unique_indices=True)
    block_expert = jnp.minimum(
        jnp.searchsorted(pend, jnp.arange(n_blocks, dtype=jnp.int32) * tm, side="right"), N_EXPERTS - 1
    ).astype(jnp.int32)
    n_valid = (pend[-1:] // tm).astype(jnp.int32)
    return slot_token, dest, block_expert, n_valid


def _run(x, caches, wts, *, tq_ab, tq_c, expert_rows):
    b, t, d = x.shape
    n = b * t
    xf = x.reshape(n, d)
    (qsb, ksb, vsb, qfx, kfx, vfx, ksb16, vsb16, kfx16, vfx16, logf) = _proj_ab(
        xf, wts["g_ab"], wts["w_in_ab"], wts["w_f"], wts["b_f"])
    r3 = lambda a: a.reshape(b, t, -1)
    logf3 = r3(logf)
    if caches is None:
        c = _cumsum_time(logf3.transpose(0, 2, 1)).transpose(0, 2, 1)
        c_q, c_p = c, c
        kp_sb, vp_sb, kp_fx, vp_fx = r3(ksb16), r3(vsb16), r3(kfx16), r3(vfx16)
        td = tk = min(ATTN_TILE, t)
        base, per_q = 0, 1
    else:
        p_sb_k, p_sb_v, p_fx_k, p_fx_v, p_logf = caches[:5]
        past = p_logf.shape[1]
        pad = (-(past + t)) % LANES
        allf = jnp.concatenate([p_logf.astype(F32), logf3, jnp.zeros((b, pad, N_FOX_HEADS), F32)], axis=1)
        c = _cumsum_time(allf.transpose(0, 2, 1)).transpose(0, 2, 1)
        c_q, c_p = c[:, past:past + t], c[:, :past]
        kp_sb, vp_sb, kp_fx, vp_fx = (a.reshape(b, past, -1) for a in (p_sb_k, p_sb_v, p_fx_k, p_fx_v))
        td, tk = t, min(ATTN_TILE, past)
        base, per_q = past // tk, 0
    kw = dict(tq=tq_ab, td=td, tk=tk, base=base, per_q=per_q)
    o_sb = _attn_ab(_sb_kernel, r3(qsb), r3(ksb16), r3(vsb16), kp_sb, vp_sb, [], [], name="attn_sb", **kw)
    extra, extra_specs = _fox_bias_operands(c_q, c_p, tq_ab, td, tk)
    o_fx = _attn_ab(_fox_kernel, r3(qfx), r3(kfx16), r3(vfx16), kp_fx, vp_fx, extra, extra_specs,
                    name="attn_fox", **kw)
    x1 = _ffn_dense(xf, o_sb.reshape(n, SB_W), o_fx.reshape(n, FOX_W), wts["w_out_ab"], wts["g_ffn"],
                    wts["w_gate"], wts["w_up"], wts["w_down"])

    qc, kc, vc, kc16, vc16 = _proj_c(x1, wts["g_c"], wts["w_in_c"])
    if caches is None:
        kprev, vprev, pos0 = r3(kc16), r3(vc16), 0
        new_k, new_v = r3(kc)[:, t - SWA_CACHE:], r3(vc)[:, t - SWA_CACHE:]
    else:
        p_k, p_v = (a.reshape(b, SWA_CACHE, SWA_KV_W) for a in caches[5:])
        kprev, vprev, pos0 = p_k, p_v, caches[0].shape[1]
        new_k = jnp.concatenate([p_k, r3(kc)], axis=1)[:, t:]
        new_v = jnp.concatenate([p_v, r3(vc)], axis=1)[:, t:]
    o_c = _swa(wts["sinks"], r3(qc), kprev, vprev, r3(kc16), r3(vc16), tq=tq_c, pos0=pos0,
               prev_is_cache=caches is not None)

    x2, xn, route = _route(x1, o_c.reshape(n, SWA_Q_W), wts["w_out_c"], wts["g_moe"], wts["w_router"])
    slot_token, dest, block_expert, n_valid = _dispatch(route, expert_rows)
    x_slots = _gather_rows(slot_token, xn)
    y_slots = _experts(block_expert, n_valid, x_slots, wts["w_gate_moe"], wts["w_up_moe"], wts["w_down_moe"],
                       tm=expert_rows, tf=512)
    y = _combine(dest, y_slots, x2, route, wts["g_final"])

    heads = lambda a, h: a.reshape(1, b, -1, h, HEAD_DIM)
    states = (heads(ksb, N_SB_HEADS), heads(vsb, N_SB_HEADS), heads(kfx, N_FOX_HEADS), heads(vfx, N_FOX_HEADS),
              logf3[None], heads(new_k, N_SWA_KV_HEADS), heads(new_v, N_SWA_KV_HEADS))
    return y.reshape(b, t, d), states


def kernel(x_prompt, x_sample, cache_sb_k, cache_sb_v, cache_fox_k, cache_fox_v, cache_fox_logf, cache_swa_k,
           cache_swa_v, norm_mix_ab, w_in_ab, b_forget, w_out_ab, norm_ffn_dense, w_gate_dense, w_up_dense,
           w_down_dense, norm_mix_c, w_in_c, sinks, w_out_c, norm_ffn_moe, w_router, w_gate_moe, w_up_moe,
           w_down_moe, norm_final):
    d = x_prompt.shape[-1]
    main_w = 3 * SB_W + 3 * FOX_W
    w_f = jnp.zeros((d, LANES), F32).at[:, :N_FOX_HEADS].set(w_in_ab[0][:, main_w:])
    wr = jnp.zeros((d, LANES), F32).at[:, :N_EXPERTS].set(w_router[0])
    wr_hi = wr.astype(BF16)
    wr_mid = (wr - wr_hi.astype(F32)).astype(BF16)
    wts = dict(
        g_ab=norm_mix_ab[0][None], w_in_ab=w_in_ab[0][:, :main_w].astype(BF16), w_f=w_f.astype(BF16),
        b_f=b_forget[0][None], w_out_ab=w_out_ab[0].astype(BF16), g_ffn=norm_ffn_dense[0][None],
        w_gate=w_gate_dense[0].astype(BF16), w_up=w_up_dense[0].astype(BF16), w_down=w_down_dense[0].astype(BF16),
        g_c=norm_mix_c[0][None], w_in_c=w_in_c[0].astype(BF16), sinks=sinks[0], w_out_c=w_out_c[0].astype(BF16),
        g_moe=norm_ffn_moe[0][None], w_router=jnp.concatenate([wr_hi, wr_mid], axis=0),
        w_gate_moe=w_gate_moe[0].astype(BF16), w_up_moe=w_up_moe[0].astype(BF16),
        w_down_moe=w_down_moe[0].astype(BF16), g_final=norm_final[None])
    t_p, t_s = x_prompt.shape[1], x_sample.shape[1]
    y_p, st_p = _run(x_prompt, None, wts, tq_ab=min(ATTN_TILE, t_p), tq_c=min(SWA_TILE, t_p),
                     expert_rows=EXPERT_ROWS)
    caches = (cache_sb_k[0], cache_sb_v[0], cache_fox_k[0], cache_fox_v[0], cache_fox_logf[0],
              cache_swa_k[0], cache_swa_v[0])
    y_s, st_s = _run(x_sample, caches, wts, tq_ab=t_s, tq_c=t_s, expert_rows=SAMPLE_EXPERT_ROWS)
    return (y_p, y_s) + st_p + st_s
```

```python
import functools

import jax
import jax.numpy as jnp
from jax import lax
from jax.experimental import pallas as pl
from jax.experimental.pallas import tpu as pltpu
from jax.experimental.pallas import tpu_sc as plsc

F32 = jnp.float32
BF16 = jnp.bfloat16

HEAD_DIM = 64
LANES = 128
N_SB_HEADS = 8
N_FOX_HEADS = 8
N_SWA_HEADS = 16
N_SWA_KV_HEADS = 2
SWA_GROUP = N_SWA_HEADS // N_SWA_KV_HEADS
CHUNK = 64
WINDOW_CHUNKS = 2
SWA_CACHE = 128
N_EXPERTS = 8
RMS_EPS = 1e-6
NEG_INF = -1e30
ATTN_SCALE = HEAD_DIM ** -0.5
SB_W = N_SB_HEADS * HEAD_DIM
FOX_W = N_FOX_HEADS * HEAD_DIM
SWA_Q_W = N_SWA_HEADS * HEAD_DIM
SWA_KV_W = N_SWA_KV_HEADS * HEAD_DIM

VMEM_LIMIT_BYTES = 56 * 1024 * 1024
ROW_TILE = 512
ATTN_TILE = 512
CUMSUM_SUB = 256
SWA_TILE = 256
SAMPLE_EXPERT_ROWS = 256
LOG2_E = 1.4426950408889634
EXPERT_ROWS = 1024
SUBLANES = 8
GATHER_CHUNK = 256
SC_WINDOW = 128
SLOT_ALIGN = 1024
NT_DIMS = (((1,), (1,)), ((), ()))


def _params(sem):
    return pltpu.CompilerParams(dimension_semantics=sem, vmem_limit_bytes=VMEM_LIMIT_BYTES)


def _resident(shape):
    n = len(shape)
    return pl.BlockSpec(shape, lambda *_: (0,) * n, pipeline_mode=pl.Buffered(1))


def _rms(x, g):
    return x * lax.rsqrt(jnp.mean(x * x, axis=-1, keepdims=True) + RMS_EPS) * g


def _log_sigmoid(z):
    return jnp.minimum(z, 0.0) - jnp.log(1.0 + jnp.exp(-jnp.abs(z)))


def _silu(a):
    return a * (1.0 / (1.0 + jnp.exp(-a)))


def _dot(a, b):
    return jnp.dot(a, b, preferred_element_type=F32)


def _dot_nt(a, b):
    return lax.dot_general(a, b, NT_DIMS, preferred_element_type=F32)


def _split3(x):
    hi = x.astype(BF16)
    r = x - hi.astype(F32)
    mid = r.astype(BF16)
    lo = (r - mid.astype(F32)).astype(BF16)
    return hi, mid, lo


def _proj_ab_kernel(x_ref, g_ref, w_ref, wf_ref, bf_ref,
                    qsb_ref, ksb_ref, vsb_ref, qfx_ref, kfx_ref, vfx_ref,
                    ksb16_ref, vsb16_ref, kfx16_ref, vfx16_ref, logf_ref):
    hn = _rms(x_ref[...], g_ref[...]).astype(BF16)
    f32_outs = (None, ksb_ref, vsb_ref, None, kfx_ref, vfx_ref)
    b16_outs = (qsb_ref, ksb16_ref, vsb16_ref, qfx_ref, kfx16_ref, vfx16_ref)
    for c in range(6):
        p = _dot(hn, w_ref[:, c * SB_W:(c + 1) * SB_W])
        if f32_outs[c] is None:
            b16_outs[c][...] = (p * ATTN_SCALE).astype(BF16)
        else:
            f32_outs[c][...] = p
            b16_outs[c][...] = p.astype(BF16)
    f = _dot(hn, wf_ref[...])[:, :N_FOX_HEADS] + bf_ref[...]
    logf_ref[...] = _log_sigmoid(f)


def _proj_ab(x, g, w, wf, bf):
    n, d = x.shape
    tm = min(ROW_TILE, n)
    row = lambda c: pl.BlockSpec((tm, c), lambda i: (i, 0))
    f32o = jax.ShapeDtypeStruct((n, SB_W), F32)
    b16o = jax.ShapeDtypeStruct((n, SB_W), BF16)
    return pl.pallas_call(
        _proj_ab_kernel,
        grid=(n // tm,),
        in_specs=[row(d), _resident((1, d)), _resident(w.shape), _resident(wf.shape), _resident((1, N_FOX_HEADS))],
        out_specs=[row(SB_W)] * 10 + [row(N_FOX_HEADS)],
        out_shape=[b16o, f32o, f32o, b16o, f32o, f32o, b16o, b16o, b16o, b16o,
                   jax.ShapeDtypeStruct((n, N_FOX_HEADS), F32)],
        compiler_params=_params(("parallel",)),
        name="proj_ab",
    )(x, g, w, wf, bf)


def _cumsum_kernel(x_ref, o_ref):
    t = x_ref.shape[-1]
    r = lax.broadcasted_iota(jnp.int32, (LANES, LANES), 0)
    c = lax.broadcasted_iota(jnp.int32, (LANES, LANES), 1)
    tri = (r <= c).astype(BF16)
    carry = jnp.zeros((x_ref.shape[1], 1), F32)
    for i in range(t // LANES):
        hi, mid, lo = _split3(x_ref[0, :, i * LANES:(i + 1) * LANES])
        cs = _dot(hi, tri) + _dot(mid, tri) + _dot(lo, tri) + carry
        o_ref[0, :, i * LANES:(i + 1) * LANES] = cs
        carry = cs[:, LANES - 1:LANES]


def _cumsum_time(x):
    b, h, t = x.shape
    spec = pl.BlockSpec((1, h, t), lambda i: (i, 0, 0))
    return pl.pallas_call(
        _cumsum_kernel, grid=(b,), in_specs=[spec], out_specs=spec,
        out_shape=jax.ShapeDtypeStruct(x.shape, F32),
        compiler_params=_params(("parallel",)), name="cumsum_time",
    )(x)


def _head_halves(q):
    lane = lax.broadcasted_iota(jnp.int32, q.shape, 1)
    zero = jnp.zeros_like(q)
    return lane, (jnp.where(lane < HEAD_DIM, q, zero), jnp.where(lane >= HEAD_DIM, q, zero))


def _strict_upper(n):
    j = lax.broadcasted_iota(jnp.int32, (n, n), 0)
    s = lax.broadcasted_iota(jnp.int32, (n, n), 1)
    return (j > s).astype(BF16)


def _sb_block(qh, k, v, total, acc_ref, h, u, vis):
    sub = u.shape[0]
    z = _dot_nt(qh, k) * LOG2_E
    ls = jnp.minimum(z, 0.0) - jnp.log2(1.0 + jnp.exp2(-jnp.abs(z)))
    lk = ls - z
    if vis is not None:
        lk = jnp.where(vis, lk, 0.0)
    lk = lk.astype(BF16)
    for s in reversed(range(k.shape[0] // sub)):
        cols = slice(s * sub, (s + 1) * sub)
        after = _dot(lk[:, cols], u)
        w = jnp.exp2(ls[:, cols] + after + total)
        if vis is not None:
            w = jnp.where(vis[:, cols], w, 0.0)
        acc_ref[h] += _dot(w.astype(BF16), v[cols, :])
        total = total + after[:, 0:1] + lk[:, s * sub:s * sub + 1].astype(F32)
    return total


def _sb_kernel(q_ref, kd_ref, vd_ref, kp_ref, vp_ref, o_ref, acc_ref, *, tk, base, per_q):
    tq, td = q_ref.shape[1], kd_ref.shape[1]
    lane, qh = _head_halves(q_ref[0])
    first_row = (pl.program_id(2) % (td // tq)) * tq
    row = first_row + lax.broadcasted_iota(jnp.int32, (tq, td), 0)
    col = lax.broadcasted_iota(jnp.int32, (tq, td), 1)
    vis = col < row
    u_diag = _strict_upper(min(CUMSUM_SUB, td))
    u_past = _strict_upper(CUMSUM_SUB)
    acc_ref[...] = jnp.zeros_like(acc_ref)
    zero = jnp.zeros((tq, 1), F32)
    totals = tuple(_sb_block(qh[h], kd_ref[0], vd_ref[0], zero, acc_ref, h, u_diag, vis) for h in range(2))

    n_past = base + pl.program_id(2) // (td // tq) * per_q

    def body(jj, totals):
        start = pl.multiple_of((n_past - 1 - jj) * tk, tk)
        k = kp_ref[0, pl.ds(start, tk), :].astype(BF16)
        v = vp_ref[0, pl.ds(start, tk), :].astype(BF16)
        return tuple(_sb_block(qh[h], k, v, totals[h], acc_ref, h, u_past, None) for h in range(2))

    lax.fori_loop(0, n_past, body, totals)
    o_ref[0] = jnp.where(lane < HEAD_DIM, acc_ref[0], acc_ref[1]).astype(o_ref.dtype)


def _fox_block(qh, k, v, ck, m, l, acc_ref, h, vis):
    s = _dot_nt(qh, k) - ck
    if vis is not None:
        s = jnp.where(vis, s, NEG_INF)
    m_new = jnp.maximum(m, jnp.max(s, axis=1, keepdims=True))
    alpha = jnp.exp(m - m_new)
    p = jnp.exp(s - m_new)
    acc_ref[h] = alpha * acc_ref[h] + _dot(p.astype(BF16), v)
    return m_new, alpha * l + jnp.sum(p, axis=1, keepdims=True)


def _fox_kernel(q_ref, kd_ref, vd_ref, kp_ref, vp_ref, ckd_ref, ckp_ref, o_ref, acc_ref, *, tk, base, per_q):
    tq, td = q_ref.shape[1], kd_ref.shape[1]
    lane, qh = _head_halves(q_ref[0])
    first_row = (pl.program_id(2) % (td // tq)) * tq
    row = first_row + lax.broadcasted_iota(jnp.int32, (tq, td), 0)
    col = lax.broadcasted_iota(jnp.int32, (tq, td), 1)
    vis = col <= row
    acc_ref[...] = jnp.zeros_like(acc_ref)
    m0 = jnp.full((tq, 1), NEG_INF, F32)
    l0 = jnp.zeros((tq, 1), F32)
    ckd = ckd_ref[0, 0]
    stats = tuple(_fox_block(qh[h], kd_ref[0], vd_ref[0], ckd[h:h + 1, :], m0, l0, acc_ref, h, vis)
                  for h in range(2))

    n_past = base + pl.program_id(2) // (td // tq) * per_q

    def body(j, stats):
        start = pl.multiple_of(j * tk, tk)
        k = kp_ref[0, pl.ds(start, tk), :].astype(BF16)
        v = vp_ref[0, pl.ds(start, tk), :].astype(BF16)
        ck = ckp_ref[0, 0, j]
        return tuple(_fox_block(qh[h], k, v, ck[h:h + 1, :], stats[h][0], stats[h][1], acc_ref, h, None)
                     for h in range(2))

    stats = lax.fori_loop(0, n_past, body, stats)
    out = jnp.where(lane < HEAD_DIM, acc_ref[0] / stats[0][1], acc_ref[1] / stats[1][1])
    o_ref[0] = out.astype(o_ref.dtype)


def _attn_ab(kernel_fn, q, kd, vd, kp, vp, extra, extra_specs, *, tq, td, tk, base, per_q, name):
    b, t, w = q.shape
    tp = kp.shape[1]
    grid = (b, w // LANES, t // tq)
    qspec = pl.BlockSpec((1, tq, LANES), lambda bi, hp, i: (bi, i, hp))
    dspec = pl.BlockSpec((1, td, LANES), lambda bi, hp, i: (bi, i // (td // tq), hp))
    pspec = pl.BlockSpec((1, tp, LANES), lambda bi, hp, i: (bi, 0, hp))
    return pl.pallas_call(
        functools.partial(kernel_fn, tk=tk, base=base, per_q=per_q),
        grid=grid,
        in_specs=[qspec, dspec, dspec, pspec, pspec] + extra_specs,
        out_specs=qspec,
        out_shape=jax.ShapeDtypeStruct(q.shape, BF16),
        scratch_shapes=[pltpu.VMEM((2, tq, LANES), F32)],
        compiler_params=_params(("parallel", "parallel", "arbitrary")),
        name=name,
    )(q, kd, vd, kp, vp, *extra)


def _fox_bias_operands(c_q, c_p, tq, td, tk):
    b, t, h = c_q.shape
    tp = c_p.shape[1]
    hp = h // 2
    ckd = c_q.reshape(b, t, hp, 2).transpose(0, 2, 3, 1)
    ckp = c_p.reshape(b, tp // tk, tk, hp, 2).transpose(0, 3, 1, 4, 2)
    specs = [pl.BlockSpec((1, 1, 2, td), lambda bi, p, i: (bi, p, 0, i // (td // tq))),
             pl.BlockSpec((1, 1, tp // tk, 2, tk), lambda bi, p, i: (bi, p, 0, 0, 0))]
    return [ckd, ckp], specs


def _ffn_dense_kernel(x_ref, osb_ref, ofx_ref, wo_ref, g_ref, wg_ref, wu_ref, wd_ref, o_ref, *, tf):
    o_ref[...] = x_ref[...] + _dot(osb_ref[...], wo_ref[:SB_W, :]) + _dot(ofx_ref[...], wo_ref[SB_W:, :])
    hn = _rms(o_ref[...], g_ref[...]).astype(BF16)
    for c in range(wg_ref.shape[1] // tf):
        a = _dot(hn, wg_ref[:, c * tf:(c + 1) * tf])
        u = _dot(hn, wu_ref[:, c * tf:(c + 1) * tf])
        o_ref[...] += _dot((_silu(a) * u).astype(BF16), wd_ref[c * tf:(c + 1) * tf, :])


def _ffn_dense(x, osb, ofx, wo, g, wg, wu, wd):
    n, d = x.shape
    tm = min(ROW_TILE, n)
    row = lambda c: pl.BlockSpec((tm, c), lambda i: (i, 0))
    return pl.pallas_call(
        functools.partial(_ffn_dense_kernel, tf=256),
        grid=(n // tm,),
        in_specs=[row(d), row(SB_W), row(FOX_W), _resident(wo.shape), _resident((1, d)),
                  _resident(wg.shape), _resident(wu.shape), _resident(wd.shape)],
        out_specs=row(d),
        out_shape=jax.ShapeDtypeStruct((n, d), F32),
        compiler_params=_params(("parallel",)),
        name="ffn_dense",
    )(x, osb, ofx, wo, g, wg, wu, wd)


def _proj_c_kernel(x_ref, g_ref, w_ref, q_ref, k_ref, v_ref, k16_ref, v16_ref):
    hn = _rms(x_ref[...], g_ref[...]).astype(BF16)
    q_ref[...] = (_dot(hn, w_ref[:, :SWA_Q_W]) * ATTN_SCALE).astype(BF16)
    k = _dot(hn, w_ref[:, SWA_Q_W:SWA_Q_W + SWA_KV_W])
    v = _dot(hn, w_ref[:, SWA_Q_W + SWA_KV_W:])
    k_ref[...] = k
    v_ref[...] = v
    k16_ref[...] = k.astype(BF16)
    v16_ref[...] = v.astype(BF16)


def _proj_c(x, g, w):
    n, d = x.shape
    tm = min(ROW_TILE, n)
    row = lambda c: pl.BlockSpec((tm, c), lambda i: (i, 0))
    kv32 = jax.ShapeDtypeStruct((n, SWA_KV_W), F32)
    kv16 = jax.ShapeDtypeStruct((n, SWA_KV_W), BF16)
    return pl.pallas_call(
        _proj_c_kernel,
        grid=(n // tm,),
        in_specs=[row(d), _resident((1, d)), _resident(w.shape)],
        out_specs=[row(SWA_Q_W)] + [row(SWA_KV_W)] * 4,
        out_shape=[jax.ShapeDtypeStruct((n, SWA_Q_W), BF16), kv32, kv32, kv16, kv16],
        compiler_params=_params(("parallel",)),
        name="proj_c",
    )(x, g, w)


def _swap_halves(a):
    return jnp.concatenate([a[:, HEAD_DIM:], a[:, :HEAD_DIM]], axis=1)


def _swa_kernel(sink_ref, q_ref, kprev_ref, vprev_ref, kcur_ref, vcur_ref, o_ref, *, pos0):
    tq = q_ref.shape[1]
    tw = SWA_CACHE + tq
    first = pos0 + pl.program_id(1) * tq
    kw = jnp.concatenate([kprev_ref[0].astype(BF16), kcur_ref[0].astype(BF16)], axis=0)
    vw = jnp.concatenate([vprev_ref[0].astype(BF16), vcur_ref[0].astype(BF16)], axis=0)
    k_by_half = (kw, _swap_halves(kw))
    v_by_half = (vw, _swap_halves(vw))
    qpos = first + lax.broadcasted_iota(jnp.int32, (tq, tw), 0)
    kpos = first - SWA_CACHE + lax.broadcasted_iota(jnp.int32, (tq, tw), 1)
    qc = jnp.right_shift(qpos, 6)
    kc = jnp.right_shift(kpos, 6)
    vis = (kpos >= 0) & (kc <= qc) & (kc >= qc - WINDOW_CHUNKS)
    dist = jnp.abs(qpos - kpos).astype(F32)
    lane = lax.broadcasted_iota(jnp.int32, (tq, LANES), 1)
    outs = []
    for h in range(N_SWA_HEADS):
        g, half = h // SWA_GROUP, h % 2
        q128 = q_ref[0, :, (h // 2) * LANES:(h // 2 + 1) * LANES]
        keep = (lane < HEAD_DIM) if half == 0 else (lane >= HEAD_DIM)
        qm = jnp.where(keep, q128, jnp.zeros_like(q128))
        slope = 2.0 ** (-8.0 * (h + 1) / N_SWA_HEADS)
        s = _dot_nt(qm, k_by_half[g ^ half]) - slope * dist
        s = jnp.where(vis, s, NEG_INF)
        sink = sink_ref[h]
        m = jnp.maximum(jnp.max(s, axis=1, keepdims=True), sink)
        p = jnp.exp(s - m)
        l = jnp.sum(p, axis=1, keepdims=True) + jnp.exp(sink - m)
        outs.append(_dot(p.astype(BF16), v_by_half[g ^ half]) / l)
        if half == 1:
            o_ref[0, :, (h // 2) * LANES:(h // 2 + 1) * LANES] = jnp.where(
                lane < HEAD_DIM, outs[-2], outs[-1]).astype(o_ref.dtype)


def _swa(sinks, q, kprev, vprev, kcur, vcur, *, tq, pos0, prev_is_cache):
    b, t, _ = q.shape
    per = tq // SWA_CACHE
    prev_map = ((lambda bi, i, s: (bi, 0, 0)) if prev_is_cache
                else (lambda bi, i, s: (bi, jnp.maximum(i * per - 1, 0), 0)))
    cur = lambda w: pl.BlockSpec((1, tq, w), lambda bi, i, s: (bi, i, 0))
    prev = pl.BlockSpec((1, SWA_CACHE, SWA_KV_W), prev_map)
    return pl.pallas_call(
        functools.partial(_swa_kernel, pos0=pos0),
        grid_spec=pltpu.PrefetchScalarGridSpec(
            num_scalar_prefetch=1, grid=(b, t // tq),
            in_specs=[cur(SWA_Q_W), prev, prev, cur(SWA_KV_W), cur(SWA_KV_W)],
            out_specs=cur(SWA_Q_W)),
        out_shape=jax.ShapeDtypeStruct(q.shape, BF16),
        compiler_params=_params(("parallel", "arbitrary")),
        name="swa",
    )(sinks, q, kprev, vprev, kcur, vcur)


def _route_kernel(x_ref, o_ref, wo_ref, g_ref, wr_ref, x2_ref, xn_ref, route_ref):
    x2 = x_ref[...] + _dot(o_ref[...], wo_ref[...])
    xn = _rms(x2, g_ref[...])
    x2_ref[...] = x2
    _store_chunked(xn_ref, xn)
    hi = xn.astype(BF16)
    mid = (xn - hi.astype(F32)).astype(BF16)
    d = xn.shape[1]
    logits = _dot(hi, wr_ref[:d, :]) + _dot(hi, wr_ref[d:, :]) + _dot(mid, wr_ref[:d, :])
    lane = lax.broadcasted_iota(jnp.int32, logits.shape, 1).astype(F32)
    lg = jnp.where(lane < N_EXPERTS, logits, -jnp.inf)
    m1 = jnp.max(lg, axis=1, keepdims=True)
    i1 = jnp.min(jnp.where(lg == m1, lane, float(LANES)), axis=1, keepdims=True)
    lg2 = jnp.where(lane == i1, -jnp.inf, lg)
    m2 = jnp.max(lg2, axis=1, keepdims=True)
    i2 = jnp.min(jnp.where(lg2 == m2, lane, float(LANES)), axis=1, keepdims=True)
    e = jnp.exp(m2 - m1)
    g1 = 1.0 / (1.0 + e)
    g2 = e / (1.0 + e)
    route = jnp.where(lane == 0, i1, jnp.where(lane == 1, i2, jnp.where(lane == 2, g1, jnp.where(lane == 3, g2, 0.0))))
    route_ref[...] = route


def _route(x, o, wo, g, wr):
    n, d = x.shape
    tm = min(ROW_TILE, n)
    row = lambda c: pl.BlockSpec((tm, c), lambda i: (i, 0))
    xs = jax.ShapeDtypeStruct((n, d), F32)
    return pl.pallas_call(
        _route_kernel,
        grid=(n // tm,),
        in_specs=[row(d), row(SWA_Q_W), _resident(wo.shape), _resident((1, d)), _resident(wr.shape)],
        out_specs=[row(d), _chunked_spec(tm, d), row(LANES)],
        out_shape=[xs, _chunked_shape(n, d), jax.ShapeDtypeStruct((n, LANES), F32)],
        compiler_params=_params(("parallel",)),
        name="route",
    )(x, o, wo, g, wr)


def _chunked_shape(rows, d):
    return jax.ShapeDtypeStruct((rows // SUBLANES, d // GATHER_CHUNK, SUBLANES, GATHER_CHUNK), F32)


def _chunked_spec(tm, d, index_map=lambda i: (i, 0, 0, 0)):
    return pl.BlockSpec((tm // SUBLANES, d // GATHER_CHUNK, SUBLANES, GATHER_CHUNK), index_map)


def _store_chunked(ref, x):
    tm = x.shape[0]
    for j in range(x.shape[1] // GATHER_CHUNK):
        ref[:, j] = x[:, j * GATHER_CHUNK:(j + 1) * GATHER_CHUNK].reshape(tm // SUBLANES, SUBLANES, GATHER_CHUNK)


def _load_chunk(ref, j):
    g = ref.shape[0]
    return ref[:, j].reshape(g * SUBLANES, GATHER_CHUNK)


def _load_chunked(ref):
    return jnp.concatenate([_load_chunk(ref, j) for j in range(ref.shape[1])], axis=1)


def _sc_gather_flat(flat, idx):
    m, w = idx.shape[1], flat.shape[1]
    mesh = plsc.VectorSubcoreMesh(core_axis_name="core", subcore_axis_name="subcore")

    @pl.kernel(out_type=jax.ShapeDtypeStruct((m, w), flat.dtype), mesh=mesh)
    def gather(x_hbm, i_hbm, o_hbm):
        def body(i_vmem, o_vmem):
            pltpu.sync_copy(x_hbm.at[i_vmem.at[0]], o_vmem)

        pltpu.emit_pipeline(
            body,
            grid=(m // SC_WINDOW,),
            in_specs=[pl.BlockSpec((1, SC_WINDOW), index_map=lambda i: (0, i))],
            out_specs=[pl.BlockSpec((SC_WINDOW, w), index_map=lambda i: (i, 0))],
            core_axis_name=("core", "subcore"),
            dimension_semantics=(pltpu.PARALLEL,),
        )(i_hbm, o_hbm)

    return gather(flat, idx)


def _sc_gather(src, rows):
    _, n_chunks, _, w = src.shape
    m = rows.shape[0]
    p = jnp.arange(m * n_chunks, dtype=jnp.int32)
    out_row = SUBLANES * (p // (SUBLANES * n_chunks)) + p % SUBLANES
    chunk = (p % (SUBLANES * n_chunks)) // SUBLANES
    src_row = rows[out_row]
    idx = (src_row // SUBLANES) * (SUBLANES * n_chunks) + src_row % SUBLANES + SUBLANES * chunk
    out = _sc_gather_flat(src.reshape(-1, w), idx.reshape(1, -1))
    return out.reshape(m // SUBLANES, n_chunks, SUBLANES, w)


def _expert_kernel(be_ref, nv_ref, x_ref, wg_ref, wu_ref, wd_ref, o_ref, xb_ref, acc_ref):
    blk, f = pl.program_id(0), pl.program_id(1)
    last = pl.num_programs(1) - 1
    valid = blk < nv_ref[0]
    n_chunks = x_ref.shape[1]

    @pl.when(valid & (f == 0))
    def _():
        for j in range(n_chunks):
            xb_ref[j] = _load_chunk(x_ref, j).astype(BF16)
        acc_ref[...] = jnp.zeros_like(acc_ref)

    @pl.when(valid)
    def _():
        rows = lambda j: slice(j * GATHER_CHUNK, (j + 1) * GATHER_CHUNK)
        a = sum(_dot(xb_ref[j], wg_ref[0, rows(j), :]) for j in range(n_chunks))
        u = sum(_dot(xb_ref[j], wu_ref[0, rows(j), :]) for j in range(n_chunks))
        acc_ref[...] += _dot((_silu(a) * u).astype(BF16), wd_ref[0])

    @pl.when(valid & (f == last))
    def _():
        _store_chunked(o_ref, acc_ref[...])

    @pl.when(jnp.logical_not(valid) & (f == last))
    def _():
        o_ref[...] = jnp.zeros_like(o_ref)


def _experts(block_expert, n_valid, x_slots, wg, wu, wd, *, tm, tf):
    d = wg.shape[1]
    cap = x_slots.shape[0] * SUBLANES
    nf = wg.shape[2] // tf
    fsel = lambda b, f, nv: jnp.where(b < nv[0], f, nf - 1)
    slots = _chunked_spec(tm, d, lambda b, f, be, nv: (b, 0, 0, 0))
    return pl.pallas_call(
        _expert_kernel,
        grid_spec=pltpu.PrefetchScalarGridSpec(
            num_scalar_prefetch=2, grid=(cap // tm, nf),
            in_specs=[slots,
                      pl.BlockSpec((1, d, tf), lambda b, f, be, nv: (be[b], 0, fsel(b, f, nv))),
                      pl.BlockSpec((1, d, tf), lambda b, f, be, nv: (be[b], 0, fsel(b, f, nv))),
                      pl.BlockSpec((1, tf, d), lambda b, f, be, nv: (be[b], fsel(b, f, nv), 0))],
            out_specs=slots,
            scratch_shapes=[pltpu.VMEM((d // GATHER_CHUNK, tm, GATHER_CHUNK), BF16), pltpu.VMEM((tm, d), F32)]),
        out_shape=_chunked_shape(cap, d),
        compiler_params=_params(("arbitrary", "arbitrary")),
        name="moe_experts",
    )(block_expert, n_valid, x_slots, wg, wu, wd)


def _combine_kernel(y1_ref, y2_ref, x_ref, route_ref, g_ref, o_ref):
    route = route_ref[...]
    y = x_ref[...] + (_load_chunked(y1_ref) * route[:, 2:3] + _load_chunked(y2_ref) * route[:, 3:4])
    o_ref[...] = _rms(y, g_ref[...])


def _combine(y_pairs, x2, route, g):
    n, d = x2.shape
    tm = min(ROW_TILE, n)
    row = lambda c: pl.BlockSpec((tm, c), lambda i: (i, 0))
    second = n // tm
    return pl.pallas_call(
        _combine_kernel,
        grid=(n // tm,),
        in_specs=[_chunked_spec(tm, d), _chunked_spec(tm, d, lambda i: (i + second, 0, 0, 0)),
                  row(d), row(LANES), _resident((1, d))],
        out_specs=row(d),
        out_shape=jax.ShapeDtypeStruct((n, d), F32),
        compiler_params=_params(("parallel",)),
        name="moe_combine",
    )(y_pairs, y_pairs, x2, route, g)


def _dispatch(route, tm):
    n = route.shape[0]
    expert = route[:, :2].astype(jnp.int32).reshape(-1)
    onehot = (expert[:, None] == jnp.arange(N_EXPERTS, dtype=jnp.int32)[None, :]).astype(jnp.int32)
    csum = jnp.cumsum(onehot, axis=0)
    rank = jnp.sum(onehot * csum, axis=1) - 1
    counts = csum[-1]
    padded = (counts + tm - 1) // tm * tm
    pend = jnp.cumsum(padded)
    pstart = pend - padded
    dest = (jnp.sum(onehot * pstart[None, :], axis=1) + rank).astype(jnp.int32)
    step = max(tm, SLOT_ALIGN)
    cap = -(-(2 * n + N_EXPERTS * (tm - 1)) // step) * step
    n_blocks = cap // tm
    token = jnp.arange(2 * n, dtype=jnp.int32) // 2
    slot_token = jnp.zeros((n_blocks * tm,), jnp.int32).at[dest].set(token, unique_indices=True)
    block_expert = jnp.minimum(
        jnp.searchsorted(pend, jnp.arange(n_blocks, dtype=jnp.int32) * tm, side="right"), N_EXPERTS - 1
    ).astype(jnp.int32)
    n_valid = (pend[-1:] // tm).astype(jnp.int32)
    return slot_token, dest, block_expert, n_valid


def _run(x, caches, wts, *, tq_ab, tq_c, expert_rows):
    b, t, d = x.shape
    n = b * t
    xf = x.reshape(n, d)
    (qsb, ksb, vsb, qfx, kfx, vfx, ksb16, vsb16, kfx16, vfx16, logf) = _proj_ab(
        xf, wts["g_ab"], wts["w_in_ab"], wts["w_f"], wts["b_f"])
    r3 = lambda a: a.reshape(b, t, -1)
    logf3 = r3(logf)
    if caches is None:
        c = _cumsum_time(logf3.transpose(0, 2, 1)).transpose(0, 2, 1)
        c_q, c_p = c, c
        kp_sb, vp_sb, kp_fx, vp_fx = r3(ksb16), r3(vsb16), r3(kfx16), r3(vfx16)
        td = tk = min(ATTN_TILE, t)
        base, per_q = 0, 1
    else:
        p_sb_k, p_sb_v, p_fx_k, p_fx_v, p_logf = caches[:5]
        past = p_logf.shape[1]
        pad = (-(past + t)) % LANES
        allf = jnp.concatenate([p_logf.astype(F32), logf3, jnp.zeros((b, pad, N_FOX_HEADS), F32)], axis=1)
        c = _cumsum_time(allf.transpose(0, 2, 1)).transpose(0, 2, 1)
        c_q, c_p = c[:, past:past + t], c[:, :past]
        kp_sb, vp_sb, kp_fx, vp_fx = (a.reshape(b, past, -1) for a in (p_sb_k, p_sb_v, p_fx_k, p_fx_v))
        td, tk = t, min(ATTN_TILE, past)
        base, per_q = past // tk, 0
    kw = dict(tq=tq_ab, td=td, tk=tk, base=base, per_q=per_q)
    o_sb = _attn_ab(_sb_kernel, r3(qsb), r3(ksb16), r3(vsb16), kp_sb, vp_sb, [], [], name="attn_sb", **kw)
    extra, extra_specs = _fox_bias_operands(c_q, c_p, tq_ab, td, tk)
    o_fx = _attn_ab(_fox_kernel, r3(qfx), r3(kfx16), r3(vfx16), kp_fx, vp_fx, extra, extra_specs,
                    name="attn_fox", **kw)
    x1 = _ffn_dense(xf, o_sb.reshape(n, SB_W), o_fx.reshape(n, FOX_W), wts["w_out_ab"], wts["g_ffn"],
                    wts["w_gate"], wts["w_up"], wts["w_down"])

    qc, kc, vc, kc16, vc16 = _proj_c(x1, wts["g_c"], wts["w_in_c"])
    if caches is None:
        kprev, vprev, pos0 = r3(kc16), r3(vc16), 0
        new_k, new_v = r3(kc)[:, t - SWA_CACHE:], r3(vc)[:, t - SWA_CACHE:]
    else:
        p_k, p_v = (a.reshape(b, SWA_CACHE, SWA_KV_W) for a in caches[5:])
        kprev, vprev, pos0 = p_k, p_v, caches[0].shape[1]
        new_k = jnp.concatenate([p_k, r3(kc)], axis=1)[:, t:]
        new_v = jnp.concatenate([p_v, r3(vc)], axis=1)[:, t:]
    o_c = _swa(wts["sinks"], r3(qc), kprev, vprev, r3(kc16), r3(vc16), tq=tq_c, pos0=pos0,
               prev_is_cache=caches is not None)

    x2, xn, route = _route(x1, o_c.reshape(n, SWA_Q_W), wts["w_out_c"], wts["g_moe"], wts["w_router"])
    slot_token, dest, block_expert, n_valid = _dispatch(route, expert_rows)
    x_slots = _sc_gather(xn, slot_token)
    y_slots = _experts(block_expert, n_valid, x_slots, wts["w_gate_moe"], wts["w_up_moe"], wts["w_down_moe"],
                       tm=expert_rows, tf=512)
    y_pairs = _sc_gather(y_slots, dest.reshape(n, 2).T.reshape(-1))
    y = _combine(y_pairs, x2, route, wts["g_final"])

    heads = lambda a, h: a.reshape(1, b, -1, h, HEAD_DIM)
    states = (heads(ksb, N_SB_HEADS), heads(vsb, N_SB_HEADS), heads(kfx, N_FOX_HEADS), heads(vfx, N_FOX_HEADS),
              logf3[None], heads(new_k, N_SWA_KV_HEADS), heads(new_v, N_SWA_KV_HEADS))
    return y.reshape(b, t, d), states


def kernel(x_prompt, x_sample, cache_sb_k, cache_sb_v, cache_fox_k, cache_fox_v, cache_fox_logf, cache_swa_k,
           cache_swa_v, norm_mix_ab, w_in_ab, b_forget, w_out_ab, norm_ffn_dense, w_gate_dense, w_up_dense,
           w_down_dense, norm_mix_c, w_in_c, sinks, w_out_c, norm_ffn_moe, w_router, w_gate_moe, w_up_moe,
           w_down_moe, norm_final):
    d = x_prompt.shape[-1]
    main_w = 3 * SB_W + 3 * FOX_W
    w_f = jnp.zeros((d, LANES), F32).at[:, :N_FOX_HEADS].set(w_in_ab[0][:, main_w:])
    wr = jnp.zeros((d, LANES), F32).at[:, :N_EXPERTS].set(w_router[0])
    wr_hi = wr.astype(BF16)
    wr_mid = (wr - wr_hi.astype(F32)).astype(BF16)
    wts = dict(
        g_ab=norm_mix_ab[0][None], w_in_ab=w_in_ab[0][:, :main_w].astype(BF16), w_f=w_f.astype(BF16),
        b_f=b_forget[0][None], w_out_ab=w_out_ab[0].astype(BF16), g_ffn=norm_ffn_dense[0][None],
        w_gate=w_gate_dense[0].astype(BF16), w_up=w_up_dense[0].astype(BF16), w_down=w_down_dense[0].astype(BF16),
        g_c=norm_mix_c[0][None], w_in_c=w_in_c[0].astype(BF16), sinks=sinks[0], w_out_c=w_out_c[0].astype(BF16),
        g_moe=norm_ffn_moe[0][None], w_router=jnp.concatenate([wr_hi, wr_mid], axis=0),
        w_gate_moe=w_gate_moe[0].astype(BF16), w_up_moe=w_up_moe[0].astype(BF16),
        w_down_moe=w_down_moe[0].astype(BF16), g_final=norm_final[None])
    t_p, t_s = x_prompt.shape[1], x_sample.shape[1]
    y_p, st_p = _run(x_prompt, None, wts, tq_ab=min(ATTN_TILE, t_p), tq_c=min(SWA_TILE, t_p),
                     expert_rows=EXPERT_ROWS)
    caches = (cache_sb_k[0], cache_sb_v[0], cache_fox_k[0], cache_fox_v[0], cache_fox_logf[0],
              cache_swa_k[0], cache_swa_v[0])
    y_s, st_s = _run(x_sample, caches, wts, tq_ab=t_s, tq_c=t_s, expert_rows=SAMPLE_EXPERT_ROWS)
    return (y_p, y_s) + st_p + st_s
```

```python
import functools

import jax
import jax.numpy as jnp
from jax import lax
from jax.experimental import pallas as pl
from jax.experimental.pallas import tpu as pltpu
from jax.experimental.pallas import tpu_sc as plsc

F32 = jnp.float32
BF16 = jnp.bfloat16

HEAD_DIM = 64
LANES = 128
N_SB_HEADS = 8
N_FOX_HEADS = 8
N_SWA_HEADS = 16
N_SWA_KV_HEADS = 2
SWA_GROUP = N_SWA_HEADS // N_SWA_KV_HEADS
CHUNK = 64
WINDOW_CHUNKS = 2
SWA_CACHE = 128
N_EXPERTS = 8
RMS_EPS = 1e-6
NEG_INF = -1e30
ATTN_SCALE = HEAD_DIM ** -0.5
SB_W = N_SB_HEADS * HEAD_DIM
FOX_W = N_FOX_HEADS * HEAD_DIM
SWA_Q_W = N_SWA_HEADS * HEAD_DIM
SWA_KV_W = N_SWA_KV_HEADS * HEAD_DIM

VMEM_LIMIT_BYTES = 56 * 1024 * 1024
ROW_TILE = 512
ATTN_TILE = 512
CUMSUM_SUB = 256
SWA_TILE = 256
SAMPLE_EXPERT_ROWS = 256
LOG2_E = 1.4426950408889634
EXPERT_ROWS = 1024
SUBLANES = 8
GATHER_CHUNK = 256
SC_WINDOW = 128
SLOT_ALIGN = 1024
NT_DIMS = (((1,), (1,)), ((), ()))


def _params(sem):
    return pltpu.CompilerParams(dimension_semantics=sem, vmem_limit_bytes=VMEM_LIMIT_BYTES)


def _resident(shape):
    n = len(shape)
    return pl.BlockSpec(shape, lambda *_: (0,) * n, pipeline_mode=pl.Buffered(1))


def _rms(x, g):
    return x * lax.rsqrt(jnp.mean(x * x, axis=-1, keepdims=True) + RMS_EPS) * g


def _log_sigmoid(z):
    return jnp.minimum(z, 0.0) - jnp.log(1.0 + jnp.exp(-jnp.abs(z)))


def _silu(a):
    return a * (1.0 / (1.0 + jnp.exp(-a)))


def _dot(a, b):
    return jnp.dot(a, b, preferred_element_type=F32)


def _dot_nt(a, b):
    return lax.dot_general(a, b, NT_DIMS, preferred_element_type=F32)


def _split3(x):
    hi = x.astype(BF16)
    r = x - hi.astype(F32)
    mid = r.astype(BF16)
    lo = (r - mid.astype(F32)).astype(BF16)
    return hi, mid, lo


def _proj_ab_kernel(x_ref, g_ref, w_ref, wf_ref, bf_ref,
                    qsb_ref, ksb_ref, vsb_ref, qfx_ref, kfx_ref, vfx_ref,
                    ksb16_ref, vsb16_ref, kfx16_ref, vfx16_ref, logf_ref):
    hn = _rms(x_ref[...], g_ref[...]).astype(BF16)
    f32_outs = (None, ksb_ref, vsb_ref, None, kfx_ref, vfx_ref)
    b16_outs = (qsb_ref, ksb16_ref, vsb16_ref, qfx_ref, kfx16_ref, vfx16_ref)
    for c in range(6):
        p = _dot(hn, w_ref[:, c * SB_W:(c + 1) * SB_W])
        if f32_outs[c] is None:
            b16_outs[c][...] = (p * ATTN_SCALE).astype(BF16)
        else:
            f32_outs[c][...] = p
            b16_outs[c][...] = p.astype(BF16)
    f = _dot(hn, wf_ref[...])[:, :N_FOX_HEADS] + bf_ref[...]
    logf_ref[...] = _log_sigmoid(f)


def _proj_ab(x, g, w, wf, bf):
    n, d = x.shape
    tm = min(ROW_TILE, n)
    row = lambda c: pl.BlockSpec((tm, c), lambda i: (i, 0))
    f32o = jax.ShapeDtypeStruct((n, SB_W), F32)
    b16o = jax.ShapeDtypeStruct((n, SB_W), BF16)
    return pl.pallas_call(
        _proj_ab_kernel,
        grid=(n // tm,),
        in_specs=[row(d), _resident((1, d)), _resident(w.shape), _resident(wf.shape), _resident((1, N_FOX_HEADS))],
        out_specs=[row(SB_W)] * 10 + [row(N_FOX_HEADS)],
        out_shape=[b16o, f32o, f32o, b16o, f32o, f32o, b16o, b16o, b16o, b16o,
                   jax.ShapeDtypeStruct((n, N_FOX_HEADS), F32)],
        compiler_params=_params(("parallel",)),
        name="proj_ab",
    )(x, g, w, wf, bf)


def _cumsum_kernel(x_ref, o_ref):
    t = x_ref.shape[-1]
    r = lax.broadcasted_iota(jnp.int32, (LANES, LANES), 0)
    c = lax.broadcasted_iota(jnp.int32, (LANES, LANES), 1)
    tri = (r <= c).astype(BF16)
    carry = jnp.zeros((x_ref.shape[1], 1), F32)
    for i in range(t // LANES):
        hi, mid, lo = _split3(x_ref[0, :, i * LANES:(i + 1) * LANES])
        cs = _dot(hi, tri) + _dot(mid, tri) + _dot(lo, tri) + carry
        o_ref[0, :, i * LANES:(i + 1) * LANES] = cs
        carry = cs[:, LANES - 1:LANES]


def _cumsum_time(x):
    b, h, t = x.shape
    spec = pl.BlockSpec((1, h, t), lambda i: (i, 0, 0))
    return pl.pallas_call(
        _cumsum_kernel, grid=(b,), in_specs=[spec], out_specs=spec,
        out_shape=jax.ShapeDtypeStruct(x.shape, F32),
        compiler_params=_params(("parallel",)), name="cumsum_time",
    )(x)


def _head_halves(q):
    lane = lax.broadcasted_iota(jnp.int32, q.shape, 1)
    zero = jnp.zeros_like(q)
    return lane, (jnp.where(lane < HEAD_DIM, q, zero), jnp.where(lane >= HEAD_DIM, q, zero))


def _strict_upper(n):
    j = lax.broadcasted_iota(jnp.int32, (n, n), 0)
    s = lax.broadcasted_iota(jnp.int32, (n, n), 1)
    return (j > s).astype(BF16)


def _sb_block(qh, k, v, total, acc_ref, h, u, vis):
    sub = u.shape[0]
    z = _dot_nt(qh, k) * LOG2_E
    ls = jnp.minimum(z, 0.0) - jnp.log2(1.0 + jnp.exp2(-jnp.abs(z)))
    lk = ls - z
    if vis is not None:
        lk = jnp.where(vis, lk, 0.0)
    lk = lk.astype(BF16)
    for s in reversed(range(k.shape[0] // sub)):
        cols = slice(s * sub, (s + 1) * sub)
        after = _dot(lk[:, cols], u)
        w = jnp.exp2(ls[:, cols] + after + total)
        if vis is not None:
            w = jnp.where(vis[:, cols], w, 0.0)
        acc_ref[h] += _dot(w.astype(BF16), v[cols, :])
        total = total + after[:, 0:1] + lk[:, s * sub:s * sub + 1].astype(F32)
    return total


def _sb_kernel(q_ref, kd_ref, vd_ref, kp_ref, vp_ref, o_ref, acc_ref, *, tk, base, per_q):
    tq, td = q_ref.shape[1], kd_ref.shape[1]
    lane, qh = _head_halves(q_ref[0])
    first_row = (pl.program_id(2) % (td // tq)) * tq
    row = first_row + lax.broadcasted_iota(jnp.int32, (tq, td), 0)
    col = lax.broadcasted_iota(jnp.int32, (tq, td), 1)
    vis = col < row
    u_diag = _strict_upper(min(CUMSUM_SUB, td))
    u_past = _strict_upper(CUMSUM_SUB)
    acc_ref[...] = jnp.zeros_like(acc_ref)
    zero = jnp.zeros((tq, 1), F32)
    totals = tuple(_sb_block(qh[h], kd_ref[0], vd_ref[0], zero, acc_ref, h, u_diag, vis) for h in range(2))

    n_past = base + pl.program_id(2) // (td // tq) * per_q

    def body(jj, totals):
        start = pl.multiple_of((n_past - 1 - jj) * tk, tk)
        k = kp_ref[0, pl.ds(start, tk), :].astype(BF16)
        v = vp_ref[0, pl.ds(start, tk), :].astype(BF16)
        return tuple(_sb_block(qh[h], k, v, totals[h], acc_ref, h, u_past, None) for h in range(2))

    lax.fori_loop(0, n_past, body, totals)
    o_ref[0] = jnp.where(lane < HEAD_DIM, acc_ref[0], acc_ref[1]).astype(o_ref.dtype)


def _fox_block(qh, k, v, ck, m, l, acc_ref, h, vis):
    s = _dot_nt(qh, k) - ck
    if vis is not None:
        s = jnp.where(vis, s, NEG_INF)
    m_new = jnp.maximum(m, jnp.max(s, axis=1, keepdims=True))
    alpha = jnp.exp(m - m_new)
    p = jnp.exp(s - m_new)
    acc_ref[h] = alpha * acc_ref[h] + _dot(p.astype(BF16), v)
    return m_new, alpha * l + jnp.sum(p, axis=1, keepdims=True)


def _fox_kernel(q_ref, kd_ref, vd_ref, kp_ref, vp_ref, ckd_ref, ckp_ref, o_ref, acc_ref, *, tk, base, per_q):
    tq, td = q_ref.shape[1], kd_ref.shape[1]
    lane, qh = _head_halves(q_ref[0])
    first_row = (pl.program_id(2) % (td // tq)) * tq
    row = first_row + lax.broadcasted_iota(jnp.int32, (tq, td), 0)
    col = lax.broadcasted_iota(jnp.int32, (tq, td), 1)
    vis = col <= row
    acc_ref[...] = jnp.zeros_like(acc_ref)
    m0 = jnp.full((tq, 1), NEG_INF, F32)
    l0 = jnp.zeros((tq, 1), F32)
    ckd = ckd_ref[0, 0]
    stats = tuple(_fox_block(qh[h], kd_ref[0], vd_ref[0], ckd[h:h + 1, :], m0, l0, acc_ref, h, vis)
                  for h in range(2))

    n_past = base + pl.program_id(2) // (td // tq) * per_q

    def body(j, stats):
        start = pl.multiple_of(j * tk, tk)
        k = kp_ref[0, pl.ds(start, tk), :].astype(BF16)
        v = vp_ref[0, pl.ds(start, tk), :].astype(BF16)
        ck = ckp_ref[0, 0, j]
        return tuple(_fox_block(qh[h], k, v, ck[h:h + 1, :], stats[h][0], stats[h][1], acc_ref, h, None)
                     for h in range(2))

    stats = lax.fori_loop(0, n_past, body, stats)
    out = jnp.where(lane < HEAD_DIM, acc_ref[0] / stats[0][1], acc_ref[1] / stats[1][1])
    o_ref[0] = out.astype(o_ref.dtype)


def _attn_ab(kernel_fn, q, kd, vd, kp, vp, extra, extra_specs, *, tq, td, tk, base, per_q, name):
    b, t, w = q.shape
    tp = kp.shape[1]
    grid = (b, w // LANES, t // tq)
    qspec = pl.BlockSpec((1, tq, LANES), lambda bi, hp, i: (bi, i, hp))
    dspec = pl.BlockSpec((1, td, LANES), lambda bi, hp, i: (bi, i // (td // tq), hp))
    pspec = pl.BlockSpec((1, tp, LANES), lambda bi, hp, i: (bi, 0, hp))
    return pl.pallas_call(
        functools.partial(kernel_fn, tk=tk, base=base, per_q=per_q),
        grid=grid,
        in_specs=[qspec, dspec, dspec, pspec, pspec] + extra_specs,
        out_specs=qspec,
        out_shape=jax.ShapeDtypeStruct(q.shape, BF16),
        scratch_shapes=[pltpu.VMEM((2, tq, LANES), F32)],
        compiler_params=_params(("parallel", "parallel", "arbitrary")),
        name=name,
    )(q, kd, vd, kp, vp, *extra)


def _fox_bias_operands(c_q, c_p, tq, td, tk):
    b, t, h = c_q.shape
    tp = c_p.shape[1]
    hp = h // 2
    ckd = c_q.reshape(b, t, hp, 2).transpose(0, 2, 3, 1)
    ckp = c_p.reshape(b, tp // tk, tk, hp, 2).transpose(0, 3, 1, 4, 2)
    specs = [pl.BlockSpec((1, 1, 2, td), lambda bi, p, i: (bi, p, 0, i // (td // tq))),
             pl.BlockSpec((1, 1, tp // tk, 2, tk), lambda bi, p, i: (bi, p, 0, 0, 0))]
    return [ckd, ckp], specs


def _ffn_dense_kernel(x_ref, osb_ref, ofx_ref, wo_ref, g_ref, wg_ref, wu_ref, wd_ref, o_ref, *, tf):
    o_ref[...] = x_ref[...] + _dot(osb_ref[...], wo_ref[:SB_W, :]) + _dot(ofx_ref[...], wo_ref[SB_W:, :])
    hn = _rms(o_ref[...], g_ref[...]).astype(BF16)
    for c in range(wg_ref.shape[1] // tf):
        a = _dot(hn, wg_ref[:, c * tf:(c + 1) * tf])
        u = _dot(hn, wu_ref[:, c * tf:(c + 1) * tf])
        o_ref[...] += _dot((_silu(a) * u).astype(BF16), wd_ref[c * tf:(c + 1) * tf, :])


def _ffn_dense(x, osb, ofx, wo, g, wg, wu, wd):
    n, d = x.shape
    tm = min(ROW_TILE, n)
    row = lambda c: pl.BlockSpec((tm, c), lambda i: (i, 0))
    return pl.pallas_call(
        functools.partial(_ffn_dense_kernel, tf=256),
        grid=(n // tm,),
        in_specs=[row(d), row(SB_W), row(FOX_W), _resident(wo.shape), _resident((1, d)),
                  _resident(wg.shape), _resident(wu.shape), _resident(wd.shape)],
        out_specs=row(d),
        out_shape=jax.ShapeDtypeStruct((n, d), F32),
        compiler_params=_params(("parallel",)),
        name="ffn_dense",
    )(x, osb, ofx, wo, g, wg, wu, wd)


def _proj_c_kernel(x_ref, g_ref, w_ref, q_ref, k_ref, v_ref, k16_ref, v16_ref):
    hn = _rms(x_ref[...], g_ref[...]).astype(BF16)
    q_ref[...] = (_dot(hn, w_ref[:, :SWA_Q_W]) * ATTN_SCALE).astype(BF16)
    k = _dot(hn, w_ref[:, SWA_Q_W:SWA_Q_W + SWA_KV_W])
    v = _dot(hn, w_ref[:, SWA_Q_W + SWA_KV_W:])
    k_ref[...] = k
    v_ref[...] = v
    k16_ref[...] = k.astype(BF16)
    v16_ref[...] = v.astype(BF16)


def _proj_c(x, g, w):
    n, d = x.shape
    tm = min(ROW_TILE, n)
    row = lambda c: pl.BlockSpec((tm, c), lambda i: (i, 0))
    kv32 = jax.ShapeDtypeStruct((n, SWA_KV_W), F32)
    kv16 = jax.ShapeDtypeStruct((n, SWA_KV_W), BF16)
    return pl.pallas_call(
        _proj_c_kernel,
        grid=(n // tm,),
        in_specs=[row(d), _resident((1, d)), _resident(w.shape)],
        out_specs=[row(SWA_Q_W)] + [row(SWA_KV_W)] * 4,
        out_shape=[jax.ShapeDtypeStruct((n, SWA_Q_W), BF16), kv32, kv32, kv16, kv16],
        compiler_params=_params(("parallel",)),
        name="proj_c",
    )(x, g, w)


def _swap_halves(a):
    return jnp.concatenate([a[:, HEAD_DIM:], a[:, :HEAD_DIM]], axis=1)


def _swa_kernel(sink_ref, q_ref, kprev_ref, vprev_ref, kcur_ref, vcur_ref, o_ref, *, pos0):
    tq = q_ref.shape[1]
    band = (WINDOW_CHUNKS + 1) * CHUNK
    stack = SWA_GROUP // 2
    rows = stack * CHUNK
    first = pos0 + pl.program_id(1) * tq
    kw = jnp.concatenate([kprev_ref[0].astype(BF16), kcur_ref[0].astype(BF16)], axis=0)
    vw = jnp.concatenate([vprev_ref[0].astype(BF16), vcur_ref[0].astype(BF16)], axis=0)
    k_by_half = (kw, _swap_halves(kw))
    v_by_half = (vw, _swap_halves(vw))
    q = q_ref[0]
    in_low = lax.broadcasted_iota(jnp.int32, q.shape, 1) % LANES < HEAD_DIM
    q_by_half = (jnp.where(in_low, q, jnp.zeros_like(q)), jnp.where(in_low, jnp.zeros_like(q), q))
    r_id = lax.broadcasted_iota(jnp.int32, (rows, band), 0)
    c_id = lax.broadcasted_iota(jnp.int32, (rows, band), 1)
    dist = jnp.abs(WINDOW_CHUNKS * CHUNK + r_id % CHUNK - c_id).astype(F32)
    head_in_stack = lax.broadcasted_iota(jnp.int32, (rows, 1), 0) // CHUNK
    col = lax.broadcasted_iota(jnp.int32, (1, band), 1)
    lane = lax.broadcasted_iota(jnp.int32, (CHUNK, LANES), 1)

    def per_head(values):
        out = jnp.full((rows, 1), values[-1], F32)
        for i in range(stack - 2, -1, -1):
            out = jnp.where(head_in_stack == i, values[i], out)
        return out

    heads = {(g, half): [g * SWA_GROUP + 2 * i + half for i in range(stack)]
             for g in range(N_SWA_KV_HEADS) for half in range(2)}
    bias = {key: per_head([2.0 ** (-8.0 * (h + 1) / N_SWA_HEADS) for h in hs]) * dist for key, hs in heads.items()}
    sink = {key: per_head([sink_ref[h] for h in hs]) for key, hs in heads.items()}

    for c in range(tq // CHUNK):
        in_range = first - WINDOW_CHUNKS * CHUNK + c * CHUNK + col >= 0
        win = slice(c * CHUNK, c * CHUNK + band)
        outs = {}
        for (g, half), hs in heads.items():
            qs = jnp.concatenate([q_by_half[half][c * CHUNK:(c + 1) * CHUNK, (h // 2) * LANES:(h // 2 + 1) * LANES]
                                  for h in hs], axis=0)
            s = _dot_nt(qs, k_by_half[g ^ half][win]) - bias[g, half]
            s = jnp.where(in_range, s, NEG_INF)
            m = jnp.maximum(jnp.max(s, axis=1, keepdims=True), sink[g, half])
            p = jnp.exp(s - m)
            l = jnp.sum(p, axis=1, keepdims=True) + jnp.exp(sink[g, half] - m)
            outs[g, half] = _dot(p.astype(BF16), v_by_half[g ^ half][win]) / l
        for g in range(N_SWA_KV_HEADS):
            for i in range(stack):
                pair = (g * SWA_GROUP + 2 * i) // 2
                o_ref[0, c * CHUNK:(c + 1) * CHUNK, pair * LANES:(pair + 1) * LANES] = jnp.where(
                    lane < HEAD_DIM, outs[g, 0][i * CHUNK:(i + 1) * CHUNK],
                    outs[g, 1][i * CHUNK:(i + 1) * CHUNK]).astype(o_ref.dtype)


def _swa(sinks, q, kprev, vprev, kcur, vcur, *, tq, pos0, prev_is_cache):
    b, t, _ = q.shape
    per = tq // SWA_CACHE
    prev_map = ((lambda bi, i, s: (bi, 0, 0)) if prev_is_cache
                else (lambda bi, i, s: (bi, jnp.maximum(i * per - 1, 0), 0)))
    cur = lambda w: pl.BlockSpec((1, tq, w), lambda bi, i, s: (bi, i, 0))
    prev = pl.BlockSpec((1, SWA_CACHE, SWA_KV_W), prev_map)
    return pl.pallas_call(
        functools.partial(_swa_kernel, pos0=pos0),
        grid_spec=pltpu.PrefetchScalarGridSpec(
            num_scalar_prefetch=1, grid=(b, t // tq),
            in_specs=[cur(SWA_Q_W), prev, prev, cur(SWA_KV_W), cur(SWA_KV_W)],
            out_specs=cur(SWA_Q_W)),
        out_shape=jax.ShapeDtypeStruct(q.shape, BF16),
        compiler_params=_params(("parallel", "arbitrary")),
        name="swa",
    )(sinks, q, kprev, vprev, kcur, vcur)


def _route_kernel(x_ref, o_ref, wo_ref, g_ref, wr_ref, x2_ref, xn_ref, route_ref):
    x2 = x_ref[...] + _dot(o_ref[...], wo_ref[...])
    xn = _rms(x2, g_ref[...])
    x2_ref[...] = x2
    _store_chunked(xn_ref, xn)
    hi = xn.astype(BF16)
    mid = (xn - hi.astype(F32)).astype(BF16)
    d = xn.shape[1]
    logits = _dot(hi, wr_ref[:d, :]) + _dot(hi, wr_ref[d:, :]) + _dot(mid, wr_ref[:d, :])
    lane = lax.broadcasted_iota(jnp.int32, logits.shape, 1).astype(F32)
    lg = jnp.where(lane < N_EXPERTS, logits, -jnp.inf)
    m1 = jnp.max(lg, axis=1, keepdims=True)
    i1 = jnp.min(jnp.where(lg == m1, lane, float(LANES)), axis=1, keepdims=True)
    lg2 = jnp.where(lane == i1, -jnp.inf, lg)
    m2 = jnp.max(lg2, axis=1, keepdims=True)
    i2 = jnp.min(jnp.where(lg2 == m2, lane, float(LANES)), axis=1, keepdims=True)
    e = jnp.exp(m2 - m1)
    g1 = 1.0 / (1.0 + e)
    g2 = e / (1.0 + e)
    route = jnp.where(lane == 0, i1, jnp.where(lane == 1, i2, jnp.where(lane == 2, g1, jnp.where(lane == 3, g2, 0.0))))
    route_ref[...] = route


def _route(x, o, wo, g, wr):
    n, d = x.shape
    tm = min(ROW_TILE, n)
    row = lambda c: pl.BlockSpec((tm, c), lambda i: (i, 0))
    xs = jax.ShapeDtypeStruct((n, d), F32)
    return pl.pallas_call(
        _route_kernel,
        grid=(n // tm,),
        in_specs=[row(d), row(SWA_Q_W), _resident(wo.shape), _resident((1, d)), _resident(wr.shape)],
        out_specs=[row(d), _chunked_spec(tm, d), row(LANES)],
        out_shape=[xs, _chunked_shape(n, d), jax.ShapeDtypeStruct((n, LANES), F32)],
        compiler_params=_params(("parallel",)),
        name="route",
    )(x, o, wo, g, wr)


def _chunked_shape(rows, d):
    return jax.ShapeDtypeStruct((rows // SUBLANES, d // GATHER_CHUNK, SUBLANES, GATHER_CHUNK), F32)


def _chunked_spec(tm, d, index_map=lambda i: (i, 0, 0, 0)):
    return pl.BlockSpec((tm // SUBLANES, d // GATHER_CHUNK, SUBLANES, GATHER_CHUNK), index_map)


def _store_chunked(ref, x):
    tm = x.shape[0]
    for j in range(x.shape[1] // GATHER_CHUNK):
        ref[:, j] = x[:, j * GATHER_CHUNK:(j + 1) * GATHER_CHUNK].reshape(tm // SUBLANES, SUBLANES, GATHER_CHUNK)


def _load_chunk(ref, j):
    g = ref.shape[0]
    return ref[:, j].reshape(g * SUBLANES, GATHER_CHUNK)


def _load_chunked(ref):
    return jnp.concatenate([_load_chunk(ref, j) for j in range(ref.shape[1])], axis=1)


def _sc_gather_flat(flat, idx):
    m, w = idx.shape[1], flat.shape[1]
    mesh = plsc.VectorSubcoreMesh(core_axis_name="core", subcore_axis_name="subcore")

    @pl.kernel(out_type=jax.ShapeDtypeStruct((m, w), flat.dtype), mesh=mesh)
    def gather(x_hbm, i_hbm, o_hbm):
        def body(i_vmem, o_vmem):
            pltpu.sync_copy(x_hbm.at[i_vmem.at[0]], o_vmem)

        pltpu.emit_pipeline(
            body,
            grid=(m // SC_WINDOW,),
            in_specs=[pl.BlockSpec((1, SC_WINDOW), index_map=lambda i: (0, i))],
            out_specs=[pl.BlockSpec((SC_WINDOW, w), index_map=lambda i: (i, 0))],
            core_axis_name=("core", "subcore"),
            dimension_semantics=(pltpu.PARALLEL,),
        )(i_hbm, o_hbm)

    return gather(flat, idx)


def _sc_gather(src, rows):
    _, n_chunks, _, w = src.shape
    m = rows.shape[0]
    src_row = rows.reshape(m // SUBLANES, 1, SUBLANES)
    chunk = jnp.arange(n_chunks, dtype=jnp.int32).reshape(1, n_chunks, 1)
    idx = (src_row // SUBLANES) * (SUBLANES * n_chunks) + src_row % SUBLANES + SUBLANES * chunk
    out = _sc_gather_flat(src.reshape(-1, w), idx.reshape(1, -1))
    return out.reshape(m // SUBLANES, n_chunks, SUBLANES, w)


def _expert_kernel(be_ref, nv_ref, x_ref, wg_ref, wu_ref, wd_ref, o_ref, xb_ref, acc_ref):
    blk, f = pl.program_id(0), pl.program_id(1)
    last = pl.num_programs(1) - 1
    valid = blk < nv_ref[0]
    n_chunks = x_ref.shape[1]

    @pl.when(valid & (f == 0))
    def _():
        for j in range(n_chunks):
            xb_ref[:, j * GATHER_CHUNK:(j + 1) * GATHER_CHUNK] = _load_chunk(x_ref, j).astype(BF16)
        acc_ref[...] = jnp.zeros_like(acc_ref)

    @pl.when(valid)
    def _():
        xb = xb_ref[...]
        h = (_silu(_dot(xb, wg_ref[0])) * _dot(xb, wu_ref[0])).astype(BF16)
        acc_ref[...] += _dot(h, wd_ref[0])

    @pl.when(valid & (f == last))
    def _():
        _store_chunked(o_ref, acc_ref[...])

    @pl.when(jnp.logical_not(valid) & (f == last))
    def _():
        o_ref[...] = jnp.zeros_like(o_ref)


def _experts(block_expert, n_valid, x_slots, wg, wu, wd, *, tm, tf):
    d = wg.shape[1]
    cap = x_slots.shape[0] * SUBLANES
    nf = wg.shape[2] // tf
    fsel = lambda b, f, nv: jnp.where(b < nv[0], f, nf - 1)
    slots = _chunked_spec(tm, d, lambda b, f, be, nv: (b, 0, 0, 0))
    return pl.pallas_call(
        _expert_kernel,
        grid_spec=pltpu.PrefetchScalarGridSpec(
            num_scalar_prefetch=2, grid=(cap // tm, nf),
            in_specs=[slots,
                      pl.BlockSpec((1, d, tf), lambda b, f, be, nv: (be[b], 0, fsel(b, f, nv))),
                      pl.BlockSpec((1, d, tf), lambda b, f, be, nv: (be[b], 0, fsel(b, f, nv))),
                      pl.BlockSpec((1, tf, d), lambda b, f, be, nv: (be[b], fsel(b, f, nv), 0))],
            out_specs=slots,
            scratch_shapes=[pltpu.VMEM((tm, d), BF16), pltpu.VMEM((tm, d), F32)]),
        out_shape=_chunked_shape(cap, d),
        compiler_params=_params(("arbitrary", "arbitrary")),
        name="moe_experts",
    )(block_expert, n_valid, x_slots, wg, wu, wd)


def _combine_kernel(y1_ref, y2_ref, x_ref, route_ref, g_ref, o_ref):
    route = route_ref[...]
    y = x_ref[...] + (_load_chunked(y1_ref) * route[:, 2:3] + _load_chunked(y2_ref) * route[:, 3:4])
    o_ref[...] = _rms(y, g_ref[...])


def _combine(y_pairs, x2, route, g):
    n, d = x2.shape
    tm = min(ROW_TILE, n)
    row = lambda c: pl.BlockSpec((tm, c), lambda i: (i, 0))
    second = n // tm
    return pl.pallas_call(
        _combine_kernel,
        grid=(n // tm,),
        in_specs=[_chunked_spec(tm, d), _chunked_spec(tm, d, lambda i: (i + second, 0, 0, 0)),
                  row(d), row(LANES), _resident((1, d))],
        out_specs=row(d),
        out_shape=jax.ShapeDtypeStruct((n, d), F32),
        compiler_params=_params(("parallel",)),
        name="moe_combine",
    )(y_pairs, y_pairs, x2, route, g)


def _dispatch(route, tm):
    n = route.shape[0]
    expert = route[:, :2].astype(jnp.int32).reshape(-1)
    onehot = (expert[:, None] == jnp.arange(N_EXPERTS, dtype=jnp.int32)[None, :]).astype(jnp.int32)
    csum = jnp.cumsum(onehot, axis=0)
    rank = jnp.sum(onehot * csum, axis=1) - 1
    counts = csum[-1]
    padded = (counts + tm - 1) // tm * tm
    pend = jnp.cumsum(padded)
    pstart = pend - padded
    dest = (jnp.sum(onehot * pstart[None, :], axis=1) + rank).astype(jnp.int32)
    step = max(tm, SLOT_ALIGN)
    cap = -(-(2 * n + N_EXPERTS * (tm - 1)) // step) * step
    n_blocks = cap // tm
    token = jnp.arange(2 * n, dtype=jnp.int32) // 2
    slot_token = jnp.zeros((n_blocks * tm,), jnp.int32).at[dest].set(token, unique_indices=True)
    block_expert = jnp.minimum(
        jnp.searchsorted(pend, jnp.arange(n_blocks, dtype=jnp.int32) * tm, side="right"), N_EXPERTS - 1
    ).astype(jnp.int32)
    n_valid = (pend[-1:] // tm).astype(jnp.int32)
    return slot_token, dest, block_expert, n_valid


def _run(x, caches, wts, *, tq_ab, tq_c, expert_rows):
    b, t, d = x.shape
    n = b * t
    xf = x.reshape(n, d)
    (qsb, ksb, vsb, qfx, kfx, vfx, ksb16, vsb16, kfx16, vfx16, logf) = _proj_ab(
        xf, wts["g_ab"], wts["w_in_ab"], wts["w_f"], wts["b_f"])
    r3 = lambda a: a.reshape(b, t, -1)
    logf3 = r3(logf)
    if caches is None:
        c = _cumsum_time(logf3.transpose(0, 2, 1)).transpose(0, 2, 1)
        c_q, c_p = c, c
        kp_sb, vp_sb, kp_fx, vp_fx = r3(ksb16), r3(vsb16), r3(kfx16), r3(vfx16)
        td = tk = min(ATTN_TILE, t)
        base, per_q = 0, 1
    else:
        p_sb_k, p_sb_v, p_fx_k, p_fx_v, p_logf = caches[:5]
        past = p_logf.shape[1]
        pad = (-(past + t)) % LANES
        allf = jnp.concatenate([p_logf.astype(F32), logf3, jnp.zeros((b, pad, N_FOX_HEADS), F32)], axis=1)
        c = _cumsum_time(allf.transpose(0, 2, 1)).transpose(0, 2, 1)
        c_q, c_p = c[:, past:past + t], c[:, :past]
        kp_sb, vp_sb, kp_fx, vp_fx = (a.reshape(b, past, -1) for a in (p_sb_k, p_sb_v, p_fx_k, p_fx_v))
        td, tk = t, min(ATTN_TILE, past)
        base, per_q = past // tk, 0
    kw = dict(tq=tq_ab, td=td, tk=tk, base=base, per_q=per_q)
    o_sb = _attn_ab(_sb_kernel, r3(qsb), r3(ksb16), r3(vsb16), kp_sb, vp_sb, [], [], name="attn_sb", **kw)
    extra, extra_specs = _fox_bias_operands(c_q, c_p, tq_ab, td, tk)
    o_fx = _attn_ab(_fox_kernel, r3(qfx), r3(kfx16), r3(vfx16), kp_fx, vp_fx, extra, extra_specs,
                    name="attn_fox", **kw)
    x1 = _ffn_dense(xf, o_sb.reshape(n, SB_W), o_fx.reshape(n, FOX_W), wts["w_out_ab"], wts["g_ffn"],
                    wts["w_gate"], wts["w_up"], wts["w_down"])

    qc, kc, vc, kc16, vc16 = _proj_c(x1, wts["g_c"], wts["w_in_c"])
    if caches is None:
        kprev, vprev, pos0 = r3(kc16), r3(vc16), 0
        new_k, new_v = r3(kc)[:, t - SWA_CACHE:], r3(vc)[:, t - SWA_CACHE:]
    else:
        p_k, p_v = (a.reshape(b, SWA_CACHE, SWA_KV_W) for a in caches[5:])
        kprev, vprev, pos0 = p_k, p_v, caches[0].shape[1]
        new_k = jnp.concatenate([p_k, r3(kc)], axis=1)[:, t:]
        new_v = jnp.concatenate([p_v, r3(vc)], axis=1)[:, t:]
    o_c = _swa(wts["sinks"], r3(qc), kprev, vprev, r3(kc16), r3(vc16), tq=tq_c, pos0=pos0,
               prev_is_cache=caches is not None)

    x2, xn, route = _route(x1, o_c.reshape(n, SWA_Q_W), wts["w_out_c"], wts["g_moe"], wts["w_router"])
    slot_token, dest, block_expert, n_valid = _dispatch(route, expert_rows)
    x_slots = _sc_gather(xn, slot_token)
    y_slots = _experts(block_expert, n_valid, x_slots, wts["w_gate_moe"], wts["w_up_moe"], wts["w_down_moe"],
                       tm=expert_rows, tf=512)
    y_pairs = _sc_gather(y_slots, dest.reshape(n, 2).T.reshape(-1))
    y = _combine(y_pairs, x2, route, wts["g_final"])

    heads = lambda a, h: a.reshape(1, b, -1, h, HEAD_DIM)
    states = (heads(ksb, N_SB_HEADS), heads(vsb, N_SB_HEADS), heads(kfx, N_FOX_HEADS), heads(vfx, N_FOX_HEADS),
              logf3[None], heads(new_k, N_SWA_KV_HEADS), heads(new_v, N_SWA_KV_HEADS))
    return y.reshape(b, t, d), states


def kernel(x_prompt, x_sample, cache_sb_k, cache_sb_v, cache_fox_k, cache_fox_v, cache_fox_logf, cache_swa_k,
           cache_swa_v, norm_mix_ab, w_in_ab, b_forget, w_out_ab, norm_ffn_dense, w_gate_dense, w_up_dense,
           w_down_dense, norm_mix_c, w_in_c, sinks, w_out_c, norm_ffn_moe, w_router, w_gate_moe, w_up_moe,
           w_down_moe, norm_final):
    d = x_prompt.shape[-1]
    main_w = 3 * SB_W + 3 * FOX_W
    w_f = jnp.zeros((d, LANES), F32).at[:, :N_FOX_HEADS].set(w_in_ab[0][:, main_w:])
    wr = jnp.zeros((d, LANES), F32).at[:, :N_EXPERTS].set(w_router[0])
    wr_hi = wr.astype(BF16)
    wr_mid = (wr - wr_hi.astype(F32)).astype(BF16)
    wts = dict(
        g_ab=norm_mix_ab[0][None], w_in_ab=w_in_ab[0][:, :main_w].astype(BF16), w_f=w_f.astype(BF16),
        b_f=b_forget[0][None], w_out_ab=w_out_ab[0].astype(BF16), g_ffn=norm_ffn_dense[0][None],
        w_gate=w_gate_dense[0].astype(BF16), w_up=w_up_dense[0].astype(BF16), w_down=w_down_dense[0].astype(BF16),
        g_c=norm_mix_c[0][None], w_in_c=w_in_c[0].astype(BF16), sinks=sinks[0], w_out_c=w_out_c[0].astype(BF16),
        g_moe=norm_ffn_moe[0][None], w_router=jnp.concatenate([wr_hi, wr_mid], axis=0),
        w_gate_moe=w_gate_moe[0].astype(BF16), w_up_moe=w_up_moe[0].astype(BF16),
        w_down_moe=w_down_moe[0].astype(BF16), g_final=norm_final[None])
    t_p, t_s = x_prompt.shape[1], x_sample.shape[1]
    y_p, st_p = _run(x_prompt, None, wts, tq_ab=min(ATTN_TILE, t_p), tq_c=min(SWA_TILE, t_p),
                     expert_rows=EXPERT_ROWS)
    caches = (cache_sb_k[0], cache_sb_v[0], cache_fox_k[0], cache_fox_v[0], cache_fox_logf[0],
              cache_swa_k[0], cache_swa_v[0])
    y_s, st_s = _run(x_sample, caches, wts, tq_ab=t_s, tq_c=t_s, expert_rows=SAMPLE_EXPERT_ROWS)
    return (y_p, y_s) + st_p + st_s
```

```python
import functools

import jax
import jax.numpy as jnp
from jax import lax
from jax.experimental import pallas as pl
from jax.experimental.pallas import tpu as pltpu
from jax.experimental.pallas import tpu_sc as plsc

F32 = jnp.float32
BF16 = jnp.bfloat16

HEAD_DIM = 64
LANES = 128
N_SB_HEADS = 8
N_FOX_HEADS = 8
N_SWA_HEADS = 16
N_SWA_KV_HEADS = 2
SWA_GROUP = N_SWA_HEADS // N_SWA_KV_HEADS
CHUNK = 64
WINDOW_CHUNKS = 2
SWA_CACHE = 128
N_EXPERTS = 8
RMS_EPS = 1e-6
NEG_INF = -1e30
ATTN_SCALE = HEAD_DIM ** -0.5
SB_W = N_SB_HEADS * HEAD_DIM
FOX_W = N_FOX_HEADS * HEAD_DIM
SWA_Q_W = N_SWA_HEADS * HEAD_DIM
SWA_KV_W = N_SWA_KV_HEADS * HEAD_DIM

VMEM_LIMIT_BYTES = 56 * 1024 * 1024
ROW_TILE = 512
ATTN_TILE = 512
ATTN_ROWS = 512
SAMPLE_ATTN_TILE = 2048
CUMSUM_SUB = 256
SWA_TILE = 256
SAMPLE_EXPERT_ROWS = 256
LOG2_E = 1.4426950408889634
EXPERT_ROWS = 1024
SUBLANES = 8
GATHER_CHUNK = 256
SC_WINDOW = 128
SLOT_ALIGN = 1024
NT_DIMS = (((1,), (1,)), ((), ()))


def _params(sem):
    return pltpu.CompilerParams(dimension_semantics=sem, vmem_limit_bytes=VMEM_LIMIT_BYTES)


def _resident(shape):
    n = len(shape)
    return pl.BlockSpec(shape, lambda *_: (0,) * n, pipeline_mode=pl.Buffered(1))


def _rms(x, g):
    return x * lax.rsqrt(jnp.mean(x * x, axis=-1, keepdims=True) + RMS_EPS) * g


def _log_sigmoid(z):
    return jnp.minimum(z, 0.0) - jnp.log(1.0 + jnp.exp(-jnp.abs(z)))


def _silu(a):
    return a * (1.0 / (1.0 + jnp.exp(-a)))


def _dot(a, b):
    return jnp.dot(a, b, preferred_element_type=F32)


def _dot_nt(a, b):
    return lax.dot_general(a, b, NT_DIMS, preferred_element_type=F32)


def _split3(x):
    hi = x.astype(BF16)
    r = x - hi.astype(F32)
    mid = r.astype(BF16)
    lo = (r - mid.astype(F32)).astype(BF16)
    return hi, mid, lo


def _proj_ab_kernel(x_ref, g_ref, w_ref, wf_ref, bf_ref,
                    qsb_ref, ksb_ref, vsb_ref, qfx_ref, kfx_ref, vfx_ref,
                    ksb16_ref, vsb16_ref, kfx16_ref, vfx16_ref, logf_ref):
    hn = _rms(x_ref[...], g_ref[...]).astype(BF16)
    f32_outs = (None, ksb_ref, vsb_ref, None, kfx_ref, vfx_ref)
    b16_outs = (qsb_ref, ksb16_ref, vsb16_ref, qfx_ref, kfx16_ref, vfx16_ref)
    for c in range(6):
        p = _dot(hn, w_ref[:, c * SB_W:(c + 1) * SB_W])
        if f32_outs[c] is None:
            b16_outs[c][...] = (p * ATTN_SCALE).astype(BF16)
        else:
            f32_outs[c][...] = p
            b16_outs[c][...] = p.astype(BF16)
    f = _dot(hn, wf_ref[...])[:, :N_FOX_HEADS] + bf_ref[...]
    logf_ref[...] = _log_sigmoid(f)


def _proj_ab(x, g, w, wf, bf):
    n, d = x.shape
    tm = min(ROW_TILE, n)
    row = lambda c: pl.BlockSpec((tm, c), lambda i: (i, 0))
    f32o = jax.ShapeDtypeStruct((n, SB_W), F32)
    b16o = jax.ShapeDtypeStruct((n, SB_W), BF16)
    return pl.pallas_call(
        _proj_ab_kernel,
        grid=(n // tm,),
        in_specs=[row(d), _resident((1, d)), _resident(w.shape), _resident(wf.shape), _resident((1, N_FOX_HEADS))],
        out_specs=[row(SB_W)] * 10 + [row(N_FOX_HEADS)],
        out_shape=[b16o, f32o, f32o, b16o, f32o, f32o, b16o, b16o, b16o, b16o,
                   jax.ShapeDtypeStruct((n, N_FOX_HEADS), F32)],
        compiler_params=_params(("parallel",)),
        name="proj_ab",
    )(x, g, w, wf, bf)


def _cumsum_kernel(x_ref, o_ref):
    t = x_ref.shape[-1]
    r = lax.broadcasted_iota(jnp.int32, (LANES, LANES), 0)
    c = lax.broadcasted_iota(jnp.int32, (LANES, LANES), 1)
    tri = (r <= c).astype(BF16)
    carry = jnp.zeros((x_ref.shape[1], 1), F32)
    for i in range(t // LANES):
        hi, mid, lo = _split3(x_ref[0, :, i * LANES:(i + 1) * LANES])
        cs = _dot(hi, tri) + _dot(mid, tri) + _dot(lo, tri) + carry
        o_ref[0, :, i * LANES:(i + 1) * LANES] = cs
        carry = cs[:, LANES - 1:LANES]


def _cumsum_time(x):
    b, h, t = x.shape
    spec = pl.BlockSpec((1, h, t), lambda i: (i, 0, 0))
    return pl.pallas_call(
        _cumsum_kernel, grid=(b,), in_specs=[spec], out_specs=spec,
        out_shape=jax.ShapeDtypeStruct(x.shape, F32),
        compiler_params=_params(("parallel",)), name="cumsum_time",
    )(x)


def _stacked_heads(q):
    lane = lax.broadcasted_iota(jnp.int32, q.shape, 1)
    zero = jnp.zeros_like(q)
    return jnp.concatenate([jnp.where(lane < HEAD_DIM, q, zero), jnp.where(lane >= HEAD_DIM, q, zero)], axis=0)


def _unstack_heads(acc):
    tq = acc.shape[0] // 2
    lane = lax.broadcasted_iota(jnp.int32, (tq, LANES), 1)
    return jnp.where(lane < HEAD_DIM, acc[:tq], acc[tq:])


def _causal_mask(tq, td, first_row, strict):
    row = first_row + lax.broadcasted_iota(jnp.int32, (2 * tq, td), 0) % tq
    col = lax.broadcasted_iota(jnp.int32, (2 * tq, td), 1)
    return col < row if strict else col <= row


def _strict_upper(n):
    j = lax.broadcasted_iota(jnp.int32, (n, n), 0)
    s = lax.broadcasted_iota(jnp.int32, (n, n), 1)
    return (j > s).astype(BF16)


def _sb_block(qk, v, total, acc_ref, u, vis):
    sub = u.shape[0]
    z = qk * LOG2_E
    neg_abs = lax.bitcast_convert_type(lax.bitcast_convert_type(z, jnp.uint32) | jnp.uint32(0x80000000), F32)
    ls = jnp.minimum(z, 0.0) - jnp.log2(1.0 + jnp.exp2(neg_abs))
    lk = ls - z
    if vis is not None:
        lk = jnp.where(vis, lk, 0.0)
    lk = lk.astype(BF16)
    for s in reversed(range(qk.shape[1] // sub)):
        cols = slice(s * sub, (s + 1) * sub)
        after = _dot(lk[:, cols], u)
        w = jnp.exp2(ls[:, cols] + after + total)
        if vis is not None:
            w = jnp.where(vis[:, cols], w, 0.0)
        acc_ref[...] += _dot(w.astype(BF16), v[cols, :])
        total = total + after[:, 0:1] + lk[:, s * sub:s * sub + 1].astype(F32)
    return total


def _sb_kernel(q_ref, kd_ref, vd_ref, kp_ref, vp_ref, o_ref, acc_ref, *, tk, base, per_q):
    tq, td = q_ref.shape[1], kd_ref.shape[1]
    qs = _stacked_heads(q_ref[0])
    first_row = (pl.program_id(2) % (td // tq)) * tq
    vis = _causal_mask(tq, td, first_row, strict=True)
    u_diag = _strict_upper(min(CUMSUM_SUB, td))
    u_past = _strict_upper(CUMSUM_SUB)
    acc_ref[...] = jnp.zeros_like(acc_ref)
    total = _sb_block(_dot_nt(qs, kd_ref[0]), vd_ref[0], jnp.zeros((2 * tq, 1), F32), acc_ref, u_diag, vis)

    n_past = base + pl.program_id(2) // (td // tq) * per_q

    def body(jj, total):
        start = pl.multiple_of((n_past - 1 - jj) * tk, tk)
        k = kp_ref[0, pl.ds(start, tk), :].astype(BF16)
        v = vp_ref[0, pl.ds(start, tk), :].astype(BF16)
        return _sb_block(_dot_nt(qs, k), v, total, acc_ref, u_past, None)

    lax.fori_loop(0, n_past, body, total)
    o_ref[0] = _unstack_heads(acc_ref[...]).astype(o_ref.dtype)


def _fox_block(qk, v, ck, m, l, acc_ref, vis):
    tq = qk.shape[0] // 2
    s = jnp.concatenate([qk[:tq] - ck[0:1, :], qk[tq:] - ck[1:2, :]], axis=0)
    if vis is not None:
        s = jnp.where(vis, s, NEG_INF)
    m_new = jnp.maximum(m, jnp.max(s, axis=1, keepdims=True))
    alpha = jnp.exp(m - m_new)
    p = jnp.exp(s - m_new)
    acc_ref[...] = alpha * acc_ref[...] + _dot(p.astype(BF16), v)
    return m_new, alpha * l + jnp.sum(p, axis=1, keepdims=True)


def _fox_kernel(q_ref, kd_ref, vd_ref, kp_ref, vp_ref, ckd_ref, ckp_ref, o_ref, acc_ref, *, tk, base, per_q):
    tq, td = q_ref.shape[1], kd_ref.shape[1]
    qs = _stacked_heads(q_ref[0])
    first_row = (pl.program_id(2) % (td // tq)) * tq
    vis = _causal_mask(tq, td, first_row, strict=False)
    acc_ref[...] = jnp.zeros_like(acc_ref)
    m0 = jnp.full((2 * tq, 1), NEG_INF, F32)
    l0 = jnp.zeros((2 * tq, 1), F32)
    stats = _fox_block(_dot_nt(qs, kd_ref[0]), vd_ref[0], ckd_ref[0, 0], m0, l0, acc_ref, vis)

    n_past = base + pl.program_id(2) // (td // tq) * per_q

    def body(j, stats):
        start = pl.multiple_of(j * tk, tk)
        k = kp_ref[0, pl.ds(start, tk), :].astype(BF16)
        v = vp_ref[0, pl.ds(start, tk), :].astype(BF16)
        return _fox_block(_dot_nt(qs, k), v, ckp_ref[0, 0, j], stats[0], stats[1], acc_ref, None)

    _, l = lax.fori_loop(0, n_past, body, stats)
    o_ref[0] = _unstack_heads(acc_ref[...] / l).astype(o_ref.dtype)


def _attn_ab(kernel_fn, q, kd, vd, kp, vp, extra, extra_specs, *, tq, td, tk, base, per_q, name):
    b, t, w = q.shape
    tp = kp.shape[1]
    grid = (b, w // LANES, t // tq)
    qspec = pl.BlockSpec((1, tq, LANES), lambda bi, hp, i: (bi, i, hp))
    dspec = pl.BlockSpec((1, td, LANES), lambda bi, hp, i: (bi, i // (td // tq), hp))
    pspec = pl.BlockSpec((1, tp, LANES), lambda bi, hp, i: (bi, 0, hp))
    return pl.pallas_call(
        functools.partial(kernel_fn, tk=tk, base=base, per_q=per_q),
        grid=grid,
        in_specs=[qspec, dspec, dspec, pspec, pspec] + extra_specs,
        out_specs=qspec,
        out_shape=jax.ShapeDtypeStruct(q.shape, BF16),
        scratch_shapes=[pltpu.VMEM((2 * tq, LANES), F32)],
        compiler_params=_params(("parallel", "parallel", "arbitrary")),
        name=name,
    )(q, kd, vd, kp, vp, *extra)


def _fox_bias_operands(c_q, c_p, tq, td, tk):
    b, t, h = c_q.shape
    tp = c_p.shape[1]
    hp = h // 2
    ckd = c_q.reshape(b, t, hp, 2).transpose(0, 2, 3, 1)
    ckp = c_p.reshape(b, tp // tk, tk, hp, 2).transpose(0, 3, 1, 4, 2)
    specs = [pl.BlockSpec((1, 1, 2, td), lambda bi, p, i: (bi, p, 0, i // (td // tq))),
             pl.BlockSpec((1, 1, tp // tk, 2, tk), lambda bi, p, i: (bi, p, 0, 0, 0))]
    return [ckd, ckp], specs


def _ffn_dense_kernel(x_ref, osb_ref, ofx_ref, wo_ref, g_ref, wg_ref, wu_ref, wd_ref, o_ref, *, tf):
    o_ref[...] = x_ref[...] + _dot(osb_ref[...], wo_ref[:SB_W, :]) + _dot(ofx_ref[...], wo_ref[SB_W:, :])
    hn = _rms(o_ref[...], g_ref[...]).astype(BF16)
    for c in range(wg_ref.shape[1] // tf):
        a = _dot(hn, wg_ref[:, c * tf:(c + 1) * tf])
        u = _dot(hn, wu_ref[:, c * tf:(c + 1) * tf])
        o_ref[...] += _dot((_silu(a) * u).astype(BF16), wd_ref[c * tf:(c + 1) * tf, :])


def _ffn_dense(x, osb, ofx, wo, g, wg, wu, wd):
    n, d = x.shape
    tm = min(ROW_TILE, n)
    row = lambda c: pl.BlockSpec((tm, c), lambda i: (i, 0))
    return pl.pallas_call(
        functools.partial(_ffn_dense_kernel, tf=256),
        grid=(n // tm,),
        in_specs=[row(d), row(SB_W), row(FOX_W), _resident(wo.shape), _resident((1, d)),
                  _resident(wg.shape), _resident(wu.shape), _resident(wd.shape)],
        out_specs=row(d),
        out_shape=jax.ShapeDtypeStruct((n, d), F32),
        compiler_params=_params(("parallel",)),
        name="ffn_dense",
    )(x, osb, ofx, wo, g, wg, wu, wd)


def _proj_c_kernel(x_ref, g_ref, w_ref, q_ref, k_ref, v_ref, k16_ref, v16_ref):
    hn = _rms(x_ref[...], g_ref[...]).astype(BF16)
    q_ref[...] = (_dot(hn, w_ref[:, :SWA_Q_W]) * ATTN_SCALE).astype(BF16)
    k = _dot(hn, w_ref[:, SWA_Q_W:SWA_Q_W + SWA_KV_W])
    v = _dot(hn, w_ref[:, SWA_Q_W + SWA_KV_W:])
    k_ref[...] = k
    v_ref[...] = v
    k16_ref[...] = k.astype(BF16)
    v16_ref[...] = v.astype(BF16)


def _proj_c(x, g, w):
    n, d = x.shape
    tm = min(ROW_TILE, n)
    row = lambda c: pl.BlockSpec((tm, c), lambda i: (i, 0))
    kv32 = jax.ShapeDtypeStruct((n, SWA_KV_W), F32)
    kv16 = jax.ShapeDtypeStruct((n, SWA_KV_W), BF16)
    return pl.pallas_call(
        _proj_c_kernel,
        grid=(n // tm,),
        in_specs=[row(d), _resident((1, d)), _resident(w.shape)],
        out_specs=[row(SWA_Q_W)] + [row(SWA_KV_W)] * 4,
        out_shape=[jax.ShapeDtypeStruct((n, SWA_Q_W), BF16), kv32, kv32, kv16, kv16],
        compiler_params=_params(("parallel",)),
        name="proj_c",
    )(x, g, w)


def _swap_halves(a):
    return jnp.concatenate([a[:, HEAD_DIM:], a[:, :HEAD_DIM]], axis=1)


def _swa_kernel(sink_ref, q_ref, kprev_ref, vprev_ref, kcur_ref, vcur_ref, o_ref, *, pos0):
    tq = q_ref.shape[1]
    band = (WINDOW_CHUNKS + 1) * CHUNK
    stack = SWA_GROUP // 2
    rows = stack * CHUNK
    first = pos0 + pl.program_id(1) * tq
    kw = jnp.concatenate([kprev_ref[0].astype(BF16), kcur_ref[0].astype(BF16)], axis=0)
    vw = jnp.concatenate([vprev_ref[0].astype(BF16), vcur_ref[0].astype(BF16)], axis=0)
    k_by_half = (kw, _swap_halves(kw))
    v_by_half = (vw, _swap_halves(vw))
    q = q_ref[0]
    in_low = lax.broadcasted_iota(jnp.int32, q.shape, 1) % LANES < HEAD_DIM
    q_by_half = (jnp.where(in_low, q, jnp.zeros_like(q)), jnp.where(in_low, jnp.zeros_like(q), q))
    r_id = lax.broadcasted_iota(jnp.int32, (rows, band), 0)
    c_id = lax.broadcasted_iota(jnp.int32, (rows, band), 1)
    dist = jnp.abs(WINDOW_CHUNKS * CHUNK + r_id % CHUNK - c_id).astype(F32)
    head_in_stack = lax.broadcasted_iota(jnp.int32, (rows, 1), 0) // CHUNK
    col = lax.broadcasted_iota(jnp.int32, (1, band), 1)
    lane = lax.broadcasted_iota(jnp.int32, (CHUNK, LANES), 1)

    def per_head(values):
        out = jnp.full((rows, 1), values[-1], F32)
        for i in range(stack - 2, -1, -1):
            out = jnp.where(head_in_stack == i, values[i], out)
        return out

    heads = {(g, half): [g * SWA_GROUP + 2 * i + half for i in range(stack)]
             for g in range(N_SWA_KV_HEADS) for half in range(2)}
    bias = {key: per_head([2.0 ** (-8.0 * (h + 1) / N_SWA_HEADS) for h in hs]) * dist for key, hs in heads.items()}
    sink = {key: per_head([sink_ref[h] for h in hs]) for key, hs in heads.items()}

    for c in range(tq // CHUNK):
        in_range = first - WINDOW_CHUNKS * CHUNK + c * CHUNK + col >= 0
        win = slice(c * CHUNK, c * CHUNK + band)
        outs = {}
        for (g, half), hs in heads.items():
            qs = jnp.concatenate([q_by_half[half][c * CHUNK:(c + 1) * CHUNK, (h // 2) * LANES:(h // 2 + 1) * LANES]
                                  for h in hs], axis=0)
            s = _dot_nt(qs, k_by_half[g ^ half][win]) - bias[g, half]
            s = jnp.where(in_range, s, NEG_INF)
            m = jnp.maximum(jnp.max(s, axis=1, keepdims=True), sink[g, half])
            p = jnp.exp(s - m)
            l = jnp.sum(p, axis=1, keepdims=True) + jnp.exp(sink[g, half] - m)
            outs[g, half] = _dot(p.astype(BF16), v_by_half[g ^ half][win]) / l
        for g in range(N_SWA_KV_HEADS):
            for i in range(stack):
                pair = (g * SWA_GROUP + 2 * i) // 2
                o_ref[0, c * CHUNK:(c + 1) * CHUNK, pair * LANES:(pair + 1) * LANES] = jnp.where(
                    lane < HEAD_DIM, outs[g, 0][i * CHUNK:(i + 1) * CHUNK],
                    outs[g, 1][i * CHUNK:(i + 1) * CHUNK]).astype(o_ref.dtype)


def _swa(sinks, q, kprev, vprev, kcur, vcur, *, tq, pos0, prev_is_cache):
    b, t, _ = q.shape
    per = tq // SWA_CACHE
    prev_map = ((lambda bi, i, s: (bi, 0, 0)) if prev_is_cache
                else (lambda bi, i, s: (bi, jnp.maximum(i * per - 1, 0), 0)))
    cur = lambda w: pl.BlockSpec((1, tq, w), lambda bi, i, s: (bi, i, 0))
    prev = pl.BlockSpec((1, SWA_CACHE, SWA_KV_W), prev_map)
    return pl.pallas_call(
        functools.partial(_swa_kernel, pos0=pos0),
        grid_spec=pltpu.PrefetchScalarGridSpec(
            num_scalar_prefetch=1, grid=(b, t // tq),
            in_specs=[cur(SWA_Q_W), prev, prev, cur(SWA_KV_W), cur(SWA_KV_W)],
            out_specs=cur(SWA_Q_W)),
        out_shape=jax.ShapeDtypeStruct(q.shape, BF16),
        compiler_params=_params(("parallel", "arbitrary")),
        name="swa",
    )(sinks, q, kprev, vprev, kcur, vcur)


def _route_kernel(x_ref, o_ref, wo_ref, g_ref, wr_ref, x2_ref, xn_ref, route_ref):
    x2 = x_ref[...] + _dot(o_ref[...], wo_ref[...])
    xn = _rms(x2, g_ref[...])
    x2_ref[...] = x2
    _store_chunked(xn_ref, xn)
    hi = xn.astype(BF16)
    mid = (xn - hi.astype(F32)).astype(BF16)
    d = xn.shape[1]
    logits = _dot(hi, wr_ref[:d, :]) + _dot(hi, wr_ref[d:, :]) + _dot(mid, wr_ref[:d, :])
    lane = lax.broadcasted_iota(jnp.int32, logits.shape, 1).astype(F32)
    lg = jnp.where(lane < N_EXPERTS, logits, -jnp.inf)
    m1 = jnp.max(lg, axis=1, keepdims=True)
    i1 = jnp.min(jnp.where(lg == m1, lane, float(LANES)), axis=1, keepdims=True)
    lg2 = jnp.where(lane == i1, -jnp.inf, lg)
    m2 = jnp.max(lg2, axis=1, keepdims=True)
    i2 = jnp.min(jnp.where(lg2 == m2, lane, float(LANES)), axis=1, keepdims=True)
    e = jnp.exp(m2 - m1)
    g1 = 1.0 / (1.0 + e)
    g2 = e / (1.0 + e)
    route = jnp.where(lane == 0, i1, jnp.where(lane == 1, i2, jnp.where(lane == 2, g1, jnp.where(lane == 3, g2, 0.0))))
    route_ref[...] = route


def _route(x, o, wo, g, wr):
    n, d = x.shape
    tm = min(ROW_TILE, n)
    row = lambda c: pl.BlockSpec((tm, c), lambda i: (i, 0))
    xs = jax.ShapeDtypeStruct((n, d), F32)
    return pl.pallas_call(
        _route_kernel,
        grid=(n // tm,),
        in_specs=[row(d), row(SWA_Q_W), _resident(wo.shape), _resident((1, d)), _resident(wr.shape)],
        out_specs=[row(d), _chunked_spec(tm, d), row(LANES)],
        out_shape=[xs, _chunked_shape(n, d), jax.ShapeDtypeStruct((n, LANES), F32)],
        compiler_params=_params(("parallel",)),
        name="route",
    )(x, o, wo, g, wr)


def _chunked_shape(rows, d):
    return jax.ShapeDtypeStruct((rows // SUBLANES, d // GATHER_CHUNK, SUBLANES, GATHER_CHUNK), F32)


def _chunked_spec(tm, d, index_map=lambda i: (i, 0, 0, 0)):
    return pl.BlockSpec((tm // SUBLANES, d // GATHER_CHUNK, SUBLANES, GATHER_CHUNK), index_map)


def _store_chunked(ref, x):
    tm = x.shape[0]
    for j in range(x.shape[1] // GATHER_CHUNK):
        ref[:, j] = x[:, j * GATHER_CHUNK:(j + 1) * GATHER_CHUNK].reshape(tm // SUBLANES, SUBLANES, GATHER_CHUNK)


def _load_chunk(ref, j):
    g = ref.shape[0]
    return ref[:, j].reshape(g * SUBLANES, GATHER_CHUNK)


def _load_chunked(ref):
    return jnp.concatenate([_load_chunk(ref, j) for j in range(ref.shape[1])], axis=1)


def _sc_gather_flat(flat, idx):
    m, w = idx.shape[1], flat.shape[1]
    mesh = plsc.VectorSubcoreMesh(core_axis_name="core", subcore_axis_name="subcore")

    @pl.kernel(out_type=jax.ShapeDtypeStruct((m, w), flat.dtype), mesh=mesh)
    def gather(x_hbm, i_hbm, o_hbm):
        def body(i_vmem, o_vmem):
            pltpu.sync_copy(x_hbm.at[i_vmem.at[0]], o_vmem)

        pltpu.emit_pipeline(
            body,
            grid=(m // SC_WINDOW,),
            in_specs=[pl.BlockSpec((1, SC_WINDOW), index_map=lambda i: (0, i))],
            out_specs=[pl.BlockSpec((SC_WINDOW, w), index_map=lambda i: (i, 0))],
            core_axis_name=("core", "subcore"),
            dimension_semantics=(pltpu.PARALLEL,),
        )(i_hbm, o_hbm)

    return gather(flat, idx)


def _sc_gather(src, rows):
    _, n_chunks, _, w = src.shape
    m = rows.shape[0]
    src_row = rows.reshape(m // SUBLANES, 1, SUBLANES)
    chunk = jnp.arange(n_chunks, dtype=jnp.int32).reshape(1, n_chunks, 1)
    idx = (src_row // SUBLANES) * (SUBLANES * n_chunks) + src_row % SUBLANES + SUBLANES * chunk
    out = _sc_gather_flat(src.reshape(-1, w), idx.reshape(1, -1))
    return out.reshape(m // SUBLANES, n_chunks, SUBLANES, w)


def _expert_kernel(be_ref, nv_ref, x_ref, wg_ref, wu_ref, wd_ref, o_ref, xb_ref, acc_ref):
    blk, f = pl.program_id(0), pl.program_id(1)
    last = pl.num_programs(1) - 1
    valid = blk < nv_ref[0]
    n_chunks = x_ref.shape[1]

    @pl.when(valid & (f == 0))
    def _():
        for j in range(n_chunks):
            xb_ref[:, j * GATHER_CHUNK:(j + 1) * GATHER_CHUNK] = _load_chunk(x_ref, j).astype(BF16)
        acc_ref[...] = jnp.zeros_like(acc_ref)

    @pl.when(valid)
    def _():
        xb = xb_ref[...]
        h = (_silu(_dot(xb, wg_ref[0].astype(BF16))) * _dot(xb, wu_ref[0].astype(BF16))).astype(BF16)
        acc_ref[...] += _dot(h, wd_ref[0].astype(BF16))

    @pl.when(valid & (f == last))
    def _():
        _store_chunked(o_ref, acc_ref[...])

    @pl.when(jnp.logical_not(valid) & (f == last))
    def _():
        o_ref[...] = jnp.zeros_like(o_ref)


def _experts(block_expert, n_valid, x_slots, wg, wu, wd, *, tm, tf):
    d = wg.shape[1]
    cap = x_slots.shape[0] * SUBLANES
    nf = wg.shape[2] // tf
    fsel = lambda b, f, nv: jnp.where(b < nv[0], f, nf - 1)
    slots = _chunked_spec(tm, d, lambda b, f, be, nv: (b, 0, 0, 0))
    return pl.pallas_call(
        _expert_kernel,
        grid_spec=pltpu.PrefetchScalarGridSpec(
            num_scalar_prefetch=2, grid=(cap // tm, nf),
            in_specs=[slots,
                      pl.BlockSpec((1, d, tf), lambda b, f, be, nv: (be[b], 0, fsel(b, f, nv))),
                      pl.BlockSpec((1, d, tf), lambda b, f, be, nv: (be[b], 0, fsel(b, f, nv))),
                      pl.BlockSpec((1, tf, d), lambda b, f, be, nv: (be[b], fsel(b, f, nv), 0))],
            out_specs=slots,
            scratch_shapes=[pltpu.VMEM((tm, d), BF16), pltpu.VMEM((tm, d), F32)]),
        out_shape=_chunked_shape(cap, d),
        compiler_params=_params(("arbitrary", "arbitrary")),
        name="moe_experts",
    )(block_expert, n_valid, x_slots, wg, wu, wd)


def _combine_kernel(y1_ref, y2_ref, x_ref, route_ref, g_ref, o_ref):
    route = route_ref[...]
    y = x_ref[...] + (_load_chunked(y1_ref) * route[:, 2:3] + _load_chunked(y2_ref) * route[:, 3:4])
    o_ref[...] = _rms(y, g_ref[...])


def _combine(y_pairs, x2, route, g):
    n, d = x2.shape
    tm = min(ROW_TILE, n)
    row = lambda c: pl.BlockSpec((tm, c), lambda i: (i, 0))
    second = n // tm
    return pl.pallas_call(
        _combine_kernel,
        grid=(n // tm,),
        in_specs=[_chunked_spec(tm, d), _chunked_spec(tm, d, lambda i: (i + second, 0, 0, 0)),
                  row(d), row(LANES), _resident((1, d))],
        out_specs=row(d),
        out_shape=jax.ShapeDtypeStruct((n, d), F32),
        compiler_params=_params(("parallel",)),
        name="moe_combine",
    )(y_pairs, y_pairs, x2, route, g)


def _dispatch(route, tm):
    n = route.shape[0]
    expert = route[:, :2].astype(jnp.int32).reshape(-1)
    onehot = (expert[:, None] == jnp.arange(N_EXPERTS, dtype=jnp.int32)[None, :]).astype(jnp.int32)
    csum = jnp.cumsum(onehot, axis=0)
    rank = jnp.sum(onehot * csum, axis=1) - 1
    counts = csum[-1]
    padded = (counts + tm - 1) // tm * tm
    pend = jnp.cumsum(padded)
    pstart = pend - padded
    dest = (jnp.sum(onehot * pstart[None, :], axis=1) + rank).astype(jnp.int32)
    step = max(tm, SLOT_ALIGN)
    cap = -(-(2 * n + N_EXPERTS * (tm - 1)) // step) * step
    n_blocks = cap // tm
    token = jnp.arange(2 * n, dtype=jnp.int32) // 2
    slot_token = jnp.zeros((n_blocks * tm,), jnp.int32).at[dest].set(token, unique_indices=True)
    block_expert = jnp.minimum(
        jnp.searchsorted(pend, jnp.arange(n_blocks, dtype=jnp.int32) * tm, side="right"), N_EXPERTS - 1
    ).astype(jnp.int32)
    n_valid = (pend[-1:] // tm).astype(jnp.int32)
    return slot_token, dest, block_expert, n_valid


def _run(x, caches, wts, *, tq_ab, tq_c, expert_rows):
    b, t, d = x.shape
    n = b * t
    xf = x.reshape(n, d)
    (qsb, ksb, vsb, qfx, kfx, vfx, ksb16, vsb16, kfx16, vfx16, logf) = _proj_ab(
        xf, wts["g_ab"], wts["w_in_ab"], wts["w_f"], wts["b_f"])
    r3 = lambda a: a.reshape(b, t, -1)
    logf3 = r3(logf)
    if caches is None:
        c = _cumsum_time(logf3.transpose(0, 2, 1)).transpose(0, 2, 1)
        c_q, c_p = c, c
        kp_sb, vp_sb, kp_fx, vp_fx = r3(ksb16), r3(vsb16), r3(kfx16), r3(vfx16)
        td = tk = min(ATTN_TILE, t)
        base, per_q = 0, 1
    else:
        p_sb_k, p_sb_v, p_fx_k, p_fx_v, p_logf = caches[:5]
        past = p_logf.shape[1]
        pad = (-(past + t)) % LANES
        allf = jnp.concatenate([p_logf.astype(F32), logf3, jnp.zeros((b, pad, N_FOX_HEADS), F32)], axis=1)
        c = _cumsum_time(allf.transpose(0, 2, 1)).transpose(0, 2, 1)
        c_q, c_p = c[:, past:past + t], c[:, :past]
        kp_sb, vp_sb, kp_fx, vp_fx = (a.reshape(b, past, -1) for a in (p_sb_k, p_sb_v, p_fx_k, p_fx_v))
        td, tk = t, min(SAMPLE_ATTN_TILE, past)
        base, per_q = past // tk, 0
    kw = dict(tq=tq_ab, td=td, tk=tk, base=base, per_q=per_q)
    o_sb = _attn_ab(_sb_kernel, r3(qsb), r3(ksb16), r3(vsb16), kp_sb, vp_sb, [], [], name="attn_sb", **kw)
    extra, extra_specs = _fox_bias_operands(c_q, c_p, tq_ab, td, tk)
    o_fx = _attn_ab(_fox_kernel, r3(qfx), r3(kfx16), r3(vfx16), kp_fx, vp_fx, extra, extra_specs,
                    name="attn_fox", **kw)
    x1 = _ffn_dense(xf, o_sb.reshape(n, SB_W), o_fx.reshape(n, FOX_W), wts["w_out_ab"], wts["g_ffn"],
                    wts["w_gate"], wts["w_up"], wts["w_down"])

    qc, kc, vc, kc16, vc16 = _proj_c(x1, wts["g_c"], wts["w_in_c"])
    if caches is None:
        kprev, vprev, pos0 = r3(kc16), r3(vc16), 0
        new_k, new_v = r3(kc)[:, t - SWA_CACHE:], r3(vc)[:, t - SWA_CACHE:]
    else:
        p_k, p_v = (a.reshape(b, SWA_CACHE, SWA_KV_W) for a in caches[5:])
        kprev, vprev, pos0 = p_k, p_v, caches[0].shape[1]
        new_k = jnp.concatenate([p_k, r3(kc)], axis=1)[:, t:]
        new_v = jnp.concatenate([p_v, r3(vc)], axis=1)[:, t:]
    o_c = _swa(wts["sinks"], r3(qc), kprev, vprev, r3(kc16), r3(vc16), tq=tq_c, pos0=pos0,
               prev_is_cache=caches is not None)

    x2, xn, route = _route(x1, o_c.reshape(n, SWA_Q_W), wts["w_out_c"], wts["g_moe"], wts["w_router"])
    slot_token, dest, block_expert, n_valid = _dispatch(route, expert_rows)
    x_slots = _sc_gather(xn, slot_token)
    y_slots = _experts(block_expert, n_valid, x_slots, wts["w_gate_moe"], wts["w_up_moe"], wts["w_down_moe"],
                       tm=expert_rows, tf=512)
    y_pairs = _sc_gather(y_slots, dest.reshape(n, 2).T.reshape(-1))
    y = _combine(y_pairs, x2, route, wts["g_final"])

    heads = lambda a, h: a.reshape(1, b, -1, h, HEAD_DIM)
    states = (heads(ksb, N_SB_HEADS), heads(vsb, N_SB_HEADS), heads(kfx, N_FOX_HEADS), heads(vfx, N_FOX_HEADS),
              logf3[None], heads(new_k, N_SWA_KV_HEADS), heads(new_v, N_SWA_KV_HEADS))
    return y.reshape(b, t, d), states


def kernel(x_prompt, x_sample, cache_sb_k, cache_sb_v, cache_fox_k, cache_fox_v, cache_fox_logf, cache_swa_k,
           cache_swa_v, norm_mix_ab, w_in_ab, b_forget, w_out_ab, norm_ffn_dense, w_gate_dense, w_up_dense,
           w_down_dense, norm_mix_c, w_in_c, sinks, w_out_c, norm_ffn_moe, w_router, w_gate_moe, w_up_moe,
           w_down_moe, norm_final):
    d = x_prompt.shape[-1]
    main_w = 3 * SB_W + 3 * FOX_W
    w_f = jnp.zeros((d, LANES), F32).at[:, :N_FOX_HEADS].set(w_in_ab[0][:, main_w:])
    wr = jnp.zeros((d, LANES), F32).at[:, :N_EXPERTS].set(w_router[0])
    wr_hi = wr.astype(BF16)
    wr_mid = (wr - wr_hi.astype(F32)).astype(BF16)
    wts = dict(
        g_ab=norm_mix_ab[0][None], w_in_ab=w_in_ab[0][:, :main_w].astype(BF16), w_f=w_f.astype(BF16),
        b_f=b_forget[0][None], w_out_ab=w_out_ab[0].astype(BF16), g_ffn=norm_ffn_dense[0][None],
        w_gate=w_gate_dense[0].astype(BF16), w_up=w_up_dense[0].astype(BF16), w_down=w_down_dense[0].astype(BF16),
        g_c=norm_mix_c[0][None], w_in_c=w_in_c[0].astype(BF16), sinks=sinks[0], w_out_c=w_out_c[0].astype(BF16),
        g_moe=norm_ffn_moe[0][None], w_router=jnp.concatenate([wr_hi, wr_mid], axis=0),
        w_gate_moe=w_gate_moe[0], w_up_moe=w_up_moe[0], w_down_moe=w_down_moe[0], g_final=norm_final[None])
    t_p, t_s = x_prompt.shape[1], x_sample.shape[1]
    y_p, st_p = _run(x_prompt, None, wts, tq_ab=min(ATTN_ROWS, t_p), tq_c=min(SWA_TILE, t_p),
                     expert_rows=EXPERT_ROWS)
    caches = (cache_sb_k[0], cache_sb_v[0], cache_fox_k[0], cache_fox_v[0], cache_fox_logf[0],
              cache_swa_k[0], cache_swa_v[0])
    y_s, st_s = _run(x_sample, caches, wts, tq_ab=t_s, tq_c=t_s, expert_rows=SAMPLE_EXPERT_ROWS)
    return (y_p, y_s) + st_p + st_s
```

```python
import functools

import jax
import jax.numpy as jnp
from jax import lax
from jax.experimental import pallas as pl
from jax.experimental.pallas import tpu as pltpu
from jax.experimental.pallas import tpu_sc as plsc

F32 = jnp.float32
BF16 = jnp.bfloat16

HEAD_DIM = 64
LANES = 128
N_SB_HEADS = 8
N_FOX_HEADS = 8
N_SWA_HEADS = 16
N_SWA_KV_HEADS = 2
SWA_GROUP = N_SWA_HEADS // N_SWA_KV_HEADS
CHUNK = 64
WINDOW_CHUNKS = 2
SWA_CACHE = 128
N_EXPERTS = 8
RMS_EPS = 1e-6
NEG_INF = -1e30
ATTN_SCALE = HEAD_DIM ** -0.5
SB_W = N_SB_HEADS * HEAD_DIM
FOX_W = N_FOX_HEADS * HEAD_DIM
SWA_Q_W = N_SWA_HEADS * HEAD_DIM
SWA_KV_W = N_SWA_KV_HEADS * HEAD_DIM

VMEM_LIMIT_BYTES = 56 * 1024 * 1024
ROW_TILE = 512
ATTN_TILE = 512
ATTN_ROWS = 512
SAMPLE_ATTN_TILE = 2048
CUMSUM_SUB = 256
SWA_TILE = 256
SAMPLE_EXPERT_ROWS = 256
LOG2_E = 1.4426950408889634
EXPERT_ROWS = 1024
SUBLANES = 8
GATHER_CHUNK = 256
SC_WINDOW = 128
SLOT_ALIGN = 1024
NT_DIMS = (((1,), (1,)), ((), ()))


def _params(sem):
    return pltpu.CompilerParams(dimension_semantics=sem, vmem_limit_bytes=VMEM_LIMIT_BYTES)


def _resident(shape):
    n = len(shape)
    return pl.BlockSpec(shape, lambda *_: (0,) * n, pipeline_mode=pl.Buffered(1))


def _rms(x, g):
    return x * lax.rsqrt(jnp.mean(x * x, axis=-1, keepdims=True) + RMS_EPS) * g


def _log_sigmoid(z):
    return jnp.minimum(z, 0.0) - jnp.log(1.0 + jnp.exp(-jnp.abs(z)))


def _silu(a):
    return a * (1.0 / (1.0 + jnp.exp(-a)))


def _dot(a, b):
    return jnp.dot(a, b, preferred_element_type=F32)


def _dot_nt(a, b):
    return lax.dot_general(a, b, NT_DIMS, preferred_element_type=F32)


def _split3(x):
    hi = x.astype(BF16)
    r = x - hi.astype(F32)
    mid = r.astype(BF16)
    lo = (r - mid.astype(F32)).astype(BF16)
    return hi, mid, lo


def _proj_ab_kernel(x_ref, g_ref, w_ref, wf_ref, bf_ref,
                    qsb_ref, ksb_ref, vsb_ref, qfx_ref, kfx_ref, vfx_ref,
                    ksb16_ref, vsb16_ref, kfx16_ref, vfx16_ref, logf_ref):
    hn = _rms(x_ref[...], g_ref[...]).astype(BF16)
    f32_outs = (None, ksb_ref, vsb_ref, None, kfx_ref, vfx_ref)
    b16_outs = (qsb_ref, ksb16_ref, vsb16_ref, qfx_ref, kfx16_ref, vfx16_ref)
    for c in range(6):
        p = _dot(hn, w_ref[:, c * SB_W:(c + 1) * SB_W])
        if f32_outs[c] is None:
            b16_outs[c][...] = (p * ATTN_SCALE).astype(BF16)
        else:
            f32_outs[c][...] = p.reshape(p.shape[0], SB_W // HEAD_DIM, HEAD_DIM)
            b16_outs[c][...] = p.astype(BF16)
    f = _dot(hn, wf_ref[...])[:, :N_FOX_HEADS] + bf_ref[...]
    logf_ref[...] = _log_sigmoid(f)


def _proj_ab(x, g, w, wf, bf):
    n, d = x.shape
    tm = min(ROW_TILE, n)
    row = lambda c: pl.BlockSpec((tm, c), lambda i: (i, 0))
    f32o = jax.ShapeDtypeStruct((n, SB_W // HEAD_DIM, HEAD_DIM), F32)
    b16o = jax.ShapeDtypeStruct((n, SB_W), BF16)
    st = pl.BlockSpec((tm, SB_W // HEAD_DIM, HEAD_DIM), lambda i: (i, 0, 0))
    w16 = row(SB_W)
    return pl.pallas_call(
        _proj_ab_kernel,
        grid=(n // tm,),
        in_specs=[row(d), _resident((1, d)), _resident(w.shape), _resident(wf.shape), _resident((1, N_FOX_HEADS))],
        out_specs=[w16, st, st, w16, st, st, w16, w16, w16, w16, row(N_FOX_HEADS)],
        out_shape=[b16o, f32o, f32o, b16o, f32o, f32o, b16o, b16o, b16o, b16o,
                   jax.ShapeDtypeStruct((n, N_FOX_HEADS), F32)],
        compiler_params=_params(("parallel",)),
        name="proj_ab",
    )(x, g, w, wf, bf)


def _cumsum_kernel(x_ref, o_ref):
    t = x_ref.shape[-1]
    r = lax.broadcasted_iota(jnp.int32, (LANES, LANES), 0)
    c = lax.broadcasted_iota(jnp.int32, (LANES, LANES), 1)
    tri = (r <= c).astype(BF16)
    carry = jnp.zeros((x_ref.shape[1], 1), F32)
    for i in range(t // LANES):
        hi, mid, lo = _split3(x_ref[0, :, i * LANES:(i + 1) * LANES])
        cs = _dot(hi, tri) + _dot(mid, tri) + _dot(lo, tri) + carry
        o_ref[0, :, i * LANES:(i + 1) * LANES] = cs
        carry = cs[:, LANES - 1:LANES]


def _cumsum_time(x):
    b, h, t = x.shape
    spec = pl.BlockSpec((1, h, t), lambda i: (i, 0, 0))
    return pl.pallas_call(
        _cumsum_kernel, grid=(b,), in_specs=[spec], out_specs=spec,
        out_shape=jax.ShapeDtypeStruct(x.shape, F32),
        compiler_params=_params(("parallel",)), name="cumsum_time",
    )(x)


def _stacked_heads(q):
    lane = lax.broadcasted_iota(jnp.int32, q.shape, 1)
    zero = jnp.zeros_like(q)
    return jnp.concatenate([jnp.where(lane < HEAD_DIM, q, zero), jnp.where(lane >= HEAD_DIM, q, zero)], axis=0)


def _unstack_heads(acc):
    tq = acc.shape[0] // 2
    lane = lax.broadcasted_iota(jnp.int32, (tq, LANES), 1)
    return jnp.where(lane < HEAD_DIM, acc[:tq], acc[tq:])


def _causal_mask(tq, td, first_row, strict):
    row = first_row + lax.broadcasted_iota(jnp.int32, (2 * tq, td), 0) % tq
    col = lax.broadcasted_iota(jnp.int32, (2 * tq, td), 1)
    return col < row if strict else col <= row


def _strict_upper(n):
    j = lax.broadcasted_iota(jnp.int32, (n, n), 0)
    s = lax.broadcasted_iota(jnp.int32, (n, n), 1)
    return (j > s).astype(BF16)


def _sb_block(qk, v, total, acc_ref, u, vis):
    sub = u.shape[0]
    z = qk * LOG2_E
    neg_abs = lax.bitcast_convert_type(lax.bitcast_convert_type(z, jnp.uint32) | jnp.uint32(0x80000000), F32)
    ls = jnp.minimum(z, 0.0) - jnp.log2(1.0 + jnp.exp2(neg_abs))
    lk = ls - z
    if vis is not None:
        lk = jnp.where(vis, lk, 0.0)
    lk = lk.astype(BF16)
    for s in reversed(range(qk.shape[1] // sub)):
        cols = slice(s * sub, (s + 1) * sub)
        after = _dot(lk[:, cols], u)
        w = jnp.exp2(ls[:, cols] + after + total)
        if vis is not None:
            w = jnp.where(vis[:, cols], w, 0.0)
        acc_ref[...] += _dot(w.astype(BF16), v[cols, :])
        total = total + after[:, 0:1] + lk[:, s * sub:s * sub + 1].astype(F32)
    return total


def _sb_kernel(q_ref, kd_ref, vd_ref, kp_ref, vp_ref, o_ref, acc_ref, *, tk, base, per_q):
    tq, td = q_ref.shape[1], kd_ref.shape[1]
    qs = _stacked_heads(q_ref[0])
    first_row = (pl.program_id(2) % (td // tq)) * tq
    vis = _causal_mask(tq, td, first_row, strict=True)
    u_diag = _strict_upper(min(CUMSUM_SUB, td))
    u_past = _strict_upper(CUMSUM_SUB)
    acc_ref[...] = jnp.zeros_like(acc_ref)
    total = _sb_block(_dot_nt(qs, kd_ref[0]), vd_ref[0], jnp.zeros((2 * tq, 1), F32), acc_ref, u_diag, vis)

    n_past = base + pl.program_id(2) // (td // tq) * per_q

    def body(jj, total):
        start = pl.multiple_of((n_past - 1 - jj) * tk, tk)
        k = kp_ref[0, pl.ds(start, tk), :].astype(BF16)
        v = vp_ref[0, pl.ds(start, tk), :].astype(BF16)
        return _sb_block(_dot_nt(qs, k), v, total, acc_ref, u_past, None)

    lax.fori_loop(0, n_past, body, total)
    o_ref[0] = _unstack_heads(acc_ref[...]).astype(o_ref.dtype)


def _fox_block(qk, v, ck, m, l, acc_ref, vis):
    tq = qk.shape[0] // 2
    s = jnp.concatenate([qk[:tq] - ck[0:1, :], qk[tq:] - ck[1:2, :]], axis=0)
    if vis is not None:
        s = jnp.where(vis, s, NEG_INF)
    m_new = jnp.maximum(m, jnp.max(s, axis=1, keepdims=True))
    alpha = jnp.exp(m - m_new)
    p = jnp.exp(s - m_new)
    acc_ref[...] = alpha * acc_ref[...] + _dot(p.astype(BF16), v)
    return m_new, alpha * l + jnp.sum(p, axis=1, keepdims=True)


def _fox_kernel(q_ref, kd_ref, vd_ref, kp_ref, vp_ref, ckd_ref, ckp_ref, o_ref, acc_ref, *, tk, base, per_q):
    tq, td = q_ref.shape[1], kd_ref.shape[1]
    qs = _stacked_heads(q_ref[0])
    first_row = (pl.program_id(2) % (td // tq)) * tq
    vis = _causal_mask(tq, td, first_row, strict=False)
    acc_ref[...] = jnp.zeros_like(acc_ref)
    m0 = jnp.full((2 * tq, 1), NEG_INF, F32)
    l0 = jnp.zeros((2 * tq, 1), F32)
    stats = _fox_block(_dot_nt(qs, kd_ref[0]), vd_ref[0], ckd_ref[0, 0], m0, l0, acc_ref, vis)

    n_past = base + pl.program_id(2) // (td // tq) * per_q

    def body(j, stats):
        start = pl.multiple_of(j * tk, tk)
        k = kp_ref[0, pl.ds(start, tk), :].astype(BF16)
        v = vp_ref[0, pl.ds(start, tk), :].astype(BF16)
        return _fox_block(_dot_nt(qs, k), v, ckp_ref[0, 0, j], stats[0], stats[1], acc_ref, None)

    _, l = lax.fori_loop(0, n_past, body, stats)
    o_ref[0] = _unstack_heads(acc_ref[...] / l).astype(o_ref.dtype)


def _attn_ab(kernel_fn, q, kd, vd, kp, vp, extra, extra_specs, *, tq, td, tk, base, per_q, name):
    b, t, w = q.shape
    tp = kp.shape[1]
    grid = (b, w // LANES, t // tq)
    qspec = pl.BlockSpec((1, tq, LANES), lambda bi, hp, i: (bi, i, hp))
    dspec = pl.BlockSpec((1, td, LANES), lambda bi, hp, i: (bi, i // (td // tq), hp))
    pspec = pl.BlockSpec((1, tp, LANES), lambda bi, hp, i: (bi, 0, hp))
    return pl.pallas_call(
        functools.partial(kernel_fn, tk=tk, base=base, per_q=per_q),
        grid=grid,
        in_specs=[qspec, dspec, dspec, pspec, pspec] + extra_specs,
        out_specs=qspec,
        out_shape=jax.ShapeDtypeStruct(q.shape, BF16),
        scratch_shapes=[pltpu.VMEM((2 * tq, LANES), F32)],
        compiler_params=_params(("parallel", "parallel", "arbitrary")),
        name=name,
    )(q, kd, vd, kp, vp, *extra)


def _fox_bias_operands(c_q, c_p, tq, td, tk):
    b, t, h = c_q.shape
    tp = c_p.shape[1]
    hp = h // 2
    ckd = c_q.reshape(b, t, hp, 2).transpose(0, 2, 3, 1)
    ckp = c_p.reshape(b, tp // tk, tk, hp, 2).transpose(0, 3, 1, 4, 2)
    specs = [pl.BlockSpec((1, 1, 2, td), lambda bi, p, i: (bi, p, 0, i // (td // tq))),
             pl.BlockSpec((1, 1, tp // tk, 2, tk), lambda bi, p, i: (bi, p, 0, 0, 0))]
    return [ckd, ckp], specs


def _ffn_dense_kernel(x_ref, osb_ref, ofx_ref, wo_ref, g_ref, wg_ref, wu_ref, wd_ref, o_ref, *, tf):
    o_ref[...] = x_ref[...] + _dot(osb_ref[...], wo_ref[:SB_W, :]) + _dot(ofx_ref[...], wo_ref[SB_W:, :])
    hn = _rms(o_ref[...], g_ref[...]).astype(BF16)
    for c in range(wg_ref.shape[1] // tf):
        a = _dot(hn, wg_ref[:, c * tf:(c + 1) * tf])
        u = _dot(hn, wu_ref[:, c * tf:(c + 1) * tf])
        o_ref[...] += _dot((_silu(a) * u).astype(BF16), wd_ref[c * tf:(c + 1) * tf, :])


def _ffn_dense(x, osb, ofx, wo, g, wg, wu, wd):
    n, d = x.shape
    tm = min(ROW_TILE, n)
    row = lambda c: pl.BlockSpec((tm, c), lambda i: (i, 0))
    return pl.pallas_call(
        functools.partial(_ffn_dense_kernel, tf=256),
        grid=(n // tm,),
        in_specs=[row(d), row(SB_W), row(FOX_W), _resident(wo.shape), _resident((1, d)),
                  _resident(wg.shape), _resident(wu.shape), _resident(wd.shape)],
        out_specs=row(d),
        out_shape=jax.ShapeDtypeStruct((n, d), F32),
        compiler_params=_params(("parallel",)),
        name="ffn_dense",
    )(x, osb, ofx, wo, g, wg, wu, wd)


def _proj_c_kernel(x_ref, g_ref, w_ref, q_ref, k_ref, v_ref, k16_ref, v16_ref):
    hn = _rms(x_ref[...], g_ref[...]).astype(BF16)
    q_ref[...] = (_dot(hn, w_ref[:, :SWA_Q_W]) * ATTN_SCALE).astype(BF16)
    k = _dot(hn, w_ref[:, SWA_Q_W:SWA_Q_W + SWA_KV_W])
    v = _dot(hn, w_ref[:, SWA_Q_W + SWA_KV_W:])
    k_ref[...] = k
    v_ref[...] = v
    k16_ref[...] = k.astype(BF16)
    v16_ref[...] = v.astype(BF16)


def _proj_c(x, g, w):
    n, d = x.shape
    tm = min(ROW_TILE, n)
    row = lambda c: pl.BlockSpec((tm, c), lambda i: (i, 0))
    kv32 = jax.ShapeDtypeStruct((n, SWA_KV_W), F32)
    kv16 = jax.ShapeDtypeStruct((n, SWA_KV_W), BF16)
    return pl.pallas_call(
        _proj_c_kernel,
        grid=(n // tm,),
        in_specs=[row(d), _resident((1, d)), _resident(w.shape)],
        out_specs=[row(SWA_Q_W)] + [row(SWA_KV_W)] * 4,
        out_shape=[jax.ShapeDtypeStruct((n, SWA_Q_W), BF16), kv32, kv32, kv16, kv16],
        compiler_params=_params(("parallel",)),
        name="proj_c",
    )(x, g, w)


def _swap_halves(a):
    return jnp.concatenate([a[:, HEAD_DIM:], a[:, :HEAD_DIM]], axis=1)


def _swa_kernel(sink_ref, q_ref, kprev_ref, vprev_ref, kcur_ref, vcur_ref, o_ref, *, pos0):
    tq = q_ref.shape[1]
    band = (WINDOW_CHUNKS + 1) * CHUNK
    stack = SWA_GROUP // 2
    rows = stack * CHUNK
    first = pos0 + pl.program_id(1) * tq
    kw = jnp.concatenate([kprev_ref[0].astype(BF16), kcur_ref[0].astype(BF16)], axis=0)
    vw = jnp.concatenate([vprev_ref[0].astype(BF16), vcur_ref[0].astype(BF16)], axis=0)
    k_by_half = (kw, _swap_halves(kw))
    v_by_half = (vw, _swap_halves(vw))
    q = q_ref[0]
    in_low = lax.broadcasted_iota(jnp.int32, q.shape, 1) % LANES < HEAD_DIM
    q_by_half = (jnp.where(in_low, q, jnp.zeros_like(q)), jnp.where(in_low, jnp.zeros_like(q), q))
    r_id = lax.broadcasted_iota(jnp.int32, (rows, band), 0)
    c_id = lax.broadcasted_iota(jnp.int32, (rows, band), 1)
    dist = jnp.abs(WINDOW_CHUNKS * CHUNK + r_id % CHUNK - c_id).astype(F32)
    head_in_stack = lax.broadcasted_iota(jnp.int32, (rows, 1), 0) // CHUNK
    col = lax.broadcasted_iota(jnp.int32, (1, band), 1)
    lane = lax.broadcasted_iota(jnp.int32, (CHUNK, LANES), 1)

    def per_head(values):
        out = jnp.full((rows, 1), values[-1], F32)
        for i in range(stack - 2, -1, -1):
            out = jnp.where(head_in_stack == i, values[i], out)
        return out

    heads = {(g, half): [g * SWA_GROUP + 2 * i + half for i in range(stack)]
             for g in range(N_SWA_KV_HEADS) for half in range(2)}
    bias = {key: per_head([2.0 ** (-8.0 * (h + 1) / N_SWA_HEADS) for h in hs]) * dist for key, hs in heads.items()}
    sink = {key: per_head([sink_ref[h] for h in hs]) for key, hs in heads.items()}

    for c in range(tq // CHUNK):
        in_range = first - WINDOW_CHUNKS * CHUNK + c * CHUNK + col >= 0
        win = slice(c * CHUNK, c * CHUNK + band)
        outs = {}
        for (g, half), hs in heads.items():
            qs = jnp.concatenate([q_by_half[half][c * CHUNK:(c + 1) * CHUNK, (h // 2) * LANES:(h // 2 + 1) * LANES]
                                  for h in hs], axis=0)
            s = _dot_nt(qs, k_by_half[g ^ half][win]) - bias[g, half]
            s = jnp.where(in_range, s, NEG_INF)
            m = jnp.maximum(jnp.max(s, axis=1, keepdims=True), sink[g, half])
            p = jnp.exp(s - m)
            l = jnp.sum(p, axis=1, keepdims=True) + jnp.exp(sink[g, half] - m)
            outs[g, half] = _dot(p.astype(BF16), v_by_half[g ^ half][win]) / l
        for g in range(N_SWA_KV_HEADS):
            for i in range(stack):
                pair = (g * SWA_GROUP + 2 * i) // 2
                o_ref[0, c * CHUNK:(c + 1) * CHUNK, pair * LANES:(pair + 1) * LANES] = jnp.where(
                    lane < HEAD_DIM, outs[g, 0][i * CHUNK:(i + 1) * CHUNK],
                    outs[g, 1][i * CHUNK:(i + 1) * CHUNK]).astype(o_ref.dtype)


def _swa(sinks, q, kprev, vprev, kcur, vcur, *, tq, pos0, prev_is_cache):
    b, t, _ = q.shape
    per = tq // SWA_CACHE
    prev_map = ((lambda bi, i, s: (bi, 0, 0)) if prev_is_cache
                else (lambda bi, i, s: (bi, jnp.maximum(i * per - 1, 0), 0)))
    cur = lambda w: pl.BlockSpec((1, tq, w), lambda bi, i, s: (bi, i, 0))
    prev = pl.BlockSpec((1, SWA_CACHE, SWA_KV_W), prev_map)
    return pl.pallas_call(
        functools.partial(_swa_kernel, pos0=pos0),
        grid_spec=pltpu.PrefetchScalarGridSpec(
            num_scalar_prefetch=1, grid=(b, t // tq),
            in_specs=[cur(SWA_Q_W), prev, prev, cur(SWA_KV_W), cur(SWA_KV_W)],
            out_specs=cur(SWA_Q_W)),
        out_shape=jax.ShapeDtypeStruct(q.shape, BF16),
        compiler_params=_params(("parallel", "arbitrary")),
        name="swa",
    )(sinks, q, kprev, vprev, kcur, vcur)


def _route_kernel(x_ref, o_ref, wo_ref, g_ref, wr_ref, x2_ref, xn_ref, route_ref):
    x2 = x_ref[...] + _dot(o_ref[...], wo_ref[...])
    xn = _rms(x2, g_ref[...])
    x2_ref[...] = x2
    _store_chunked(xn_ref, xn)
    hi = xn.astype(BF16)
    mid = (xn - hi.astype(F32)).astype(BF16)
    d = xn.shape[1]
    logits = _dot(hi, wr_ref[:d, :]) + _dot(hi, wr_ref[d:, :]) + _dot(mid, wr_ref[:d, :])
    lane = lax.broadcasted_iota(jnp.int32, logits.shape, 1).astype(F32)
    lg = jnp.where(lane < N_EXPERTS, logits, -jnp.inf)
    m1 = jnp.max(lg, axis=1, keepdims=True)
    i1 = jnp.min(jnp.where(lg == m1, lane, float(LANES)), axis=1, keepdims=True)
    lg2 = jnp.where(lane == i1, -jnp.inf, lg)
    m2 = jnp.max(lg2, axis=1, keepdims=True)
    i2 = jnp.min(jnp.where(lg2 == m2, lane, float(LANES)), axis=1, keepdims=True)
    e = jnp.exp(m2 - m1)
    g1 = 1.0 / (1.0 + e)
    g2 = e / (1.0 + e)
    route = jnp.where(lane == 0, i1, jnp.where(lane == 1, i2, jnp.where(lane == 2, g1, jnp.where(lane == 3, g2, 0.0))))
    route_ref[...] = route


def _route(x, o, wo, g, wr):
    n, d = x.shape
    tm = min(ROW_TILE, n)
    row = lambda c: pl.BlockSpec((tm, c), lambda i: (i, 0))
    xs = jax.ShapeDtypeStruct((n, d), F32)
    return pl.pallas_call(
        _route_kernel,
        grid=(n // tm,),
        in_specs=[row(d), row(SWA_Q_W), _resident(wo.shape), _resident((1, d)), _resident(wr.shape)],
        out_specs=[row(d), _chunked_spec(tm, d), row(LANES)],
        out_shape=[xs, _chunked_shape(n, d), jax.ShapeDtypeStruct((n, LANES), F32)],
        compiler_params=_params(("parallel",)),
        name="route",
    )(x, o, wo, g, wr)


def _chunked_shape(rows, d):
    return jax.ShapeDtypeStruct((rows // SUBLANES, d // GATHER_CHUNK, SUBLANES, GATHER_CHUNK), F32)


def _chunked_spec(tm, d, index_map=lambda i: (i, 0, 0, 0)):
    return pl.BlockSpec((tm // SUBLANES, d // GATHER_CHUNK, SUBLANES, GATHER_CHUNK), index_map)


def _store_chunked(ref, x):
    tm = x.shape[0]
    for j in range(x.shape[1] // GATHER_CHUNK):
        ref[:, j] = x[:, j * GATHER_CHUNK:(j + 1) * GATHER_CHUNK].reshape(tm // SUBLANES, SUBLANES, GATHER_CHUNK)


def _load_chunk(ref, j):
    g = ref.shape[0]
    return ref[:, j].reshape(g * SUBLANES, GATHER_CHUNK)


def _load_chunked(ref):
    return jnp.concatenate([_load_chunk(ref, j) for j in range(ref.shape[1])], axis=1)


def _sc_gather_flat(flat, idx):
    m, w = idx.shape[1], flat.shape[1]
    mesh = plsc.VectorSubcoreMesh(core_axis_name="core", subcore_axis_name="subcore")

    @pl.kernel(out_type=jax.ShapeDtypeStruct((m, w), flat.dtype), mesh=mesh)
    def gather(x_hbm, i_hbm, o_hbm):
        def body(i_vmem, o_vmem):
            pltpu.sync_copy(x_hbm.at[i_vmem.at[0]], o_vmem)

        pltpu.emit_pipeline(
            body,
            grid=(m // SC_WINDOW,),
            in_specs=[pl.BlockSpec((1, SC_WINDOW), index_map=lambda i: (0, i))],
            out_specs=[pl.BlockSpec((SC_WINDOW, w), index_map=lambda i: (i, 0))],
            core_axis_name=("core", "subcore"),
            dimension_semantics=(pltpu.PARALLEL,),
        )(i_hbm, o_hbm)

    return gather(flat, idx)


def _sc_gather(src, rows):
    _, n_chunks, _, w = src.shape
    m = rows.shape[0]
    src_row = rows.reshape(m // SUBLANES, 1, SUBLANES)
    chunk = jnp.arange(n_chunks, dtype=jnp.int32).reshape(1, n_chunks, 1)
    idx = (src_row // SUBLANES) * (SUBLANES * n_chunks) + src_row % SUBLANES + SUBLANES * chunk
    out = _sc_gather_flat(src.reshape(-1, w), idx.reshape(1, -1))
    return out.reshape(m // SUBLANES, n_chunks, SUBLANES, w)


def _expert_kernel(be_ref, nv_ref, x_ref, wg_ref, wu_ref, wd_ref, o_ref, xb_ref, acc_ref):
    blk, f = pl.program_id(0), pl.program_id(1)
    last = pl.num_programs(1) - 1
    valid = blk < nv_ref[0]
    n_chunks = x_ref.shape[1]

    @pl.when(valid & (f == 0))
    def _():
        for j in range(n_chunks):
            xb_ref[:, j * GATHER_CHUNK:(j + 1) * GATHER_CHUNK] = _load_chunk(x_ref, j).astype(BF16)
        acc_ref[...] = jnp.zeros_like(acc_ref)

    @pl.when(valid)
    def _():
        xb = xb_ref[...]
        h = (_silu(_dot(xb, wg_ref[0].astype(BF16))) * _dot(xb, wu_ref[0].astype(BF16))).astype(BF16)
        acc_ref[...] += _dot(h, wd_ref[0].astype(BF16))

    @pl.when(valid & (f == last))
    def _():
        _store_chunked(o_ref, acc_ref[...])

    @pl.when(jnp.logical_not(valid) & (f == last))
    def _():
        o_ref[...] = jnp.zeros_like(o_ref)


def _experts(block_expert, n_valid, x_slots, wg, wu, wd, *, tm, tf):
    d = wg.shape[1]
    cap = x_slots.shape[0] * SUBLANES
    nf = wg.shape[2] // tf
    fsel = lambda b, f, nv: jnp.where(b < nv[0], f, nf - 1)
    slots = _chunked_spec(tm, d, lambda b, f, be, nv: (b, 0, 0, 0))
    return pl.pallas_call(
        _expert_kernel,
        grid_spec=pltpu.PrefetchScalarGridSpec(
            num_scalar_prefetch=2, grid=(cap // tm, nf),
            in_specs=[slots,
                      pl.BlockSpec((1, d, tf), lambda b, f, be, nv: (be[b], 0, fsel(b, f, nv))),
                      pl.BlockSpec((1, d, tf), lambda b, f, be, nv: (be[b], 0, fsel(b, f, nv))),
                      pl.BlockSpec((1, tf, d), lambda b, f, be, nv: (be[b], fsel(b, f, nv), 0))],
            out_specs=slots,
            scratch_shapes=[pltpu.VMEM((tm, d), BF16), pltpu.VMEM((tm, d), F32)]),
        out_shape=_chunked_shape(cap, d),
        compiler_params=_params(("arbitrary", "arbitrary")),
        name="moe_experts",
    )(block_expert, n_valid, x_slots, wg, wu, wd)


def _combine_kernel(y1_ref, y2_ref, x_ref, route_ref, g_ref, o_ref):
    route = route_ref[...]
    y = x_ref[...] + (_load_chunked(y1_ref) * route[:, 2:3] + _load_chunked(y2_ref) * route[:, 3:4])
    o_ref[...] = _rms(y, g_ref[...])


def _combine(y_pairs, x2, route, g):
    n, d = x2.shape
    tm = min(ROW_TILE, n)
    row = lambda c: pl.BlockSpec((tm, c), lambda i: (i, 0))
    second = n // tm
    return pl.pallas_call(
        _combine_kernel,
        grid=(n // tm,),
        in_specs=[_chunked_spec(tm, d), _chunked_spec(tm, d, lambda i: (i + second, 0, 0, 0)),
                  row(d), row(LANES), _resident((1, d))],
        out_specs=row(d),
        out_shape=jax.ShapeDtypeStruct((n, d), F32),
        compiler_params=_params(("parallel",)),
        name="moe_combine",
    )(y_pairs, y_pairs, x2, route, g)


def _dispatch(route, tm):
    n = route.shape[0]
    expert = route[:, :2].astype(jnp.int32).reshape(-1)
    onehot = (expert[:, None] == jnp.arange(N_EXPERTS, dtype=jnp.int32)[None, :]).astype(jnp.int32)
    csum = jnp.cumsum(onehot, axis=0)
    rank = jnp.sum(onehot * csum, axis=1) - 1
    counts = csum[-1]
    padded = (counts + tm - 1) // tm * tm
    pend = jnp.cumsum(padded)
    pstart = pend - padded
    dest = (jnp.sum(onehot * pstart[None, :], axis=1) + rank).astype(jnp.int32)
    step = max(tm, SLOT_ALIGN)
    cap = -(-(2 * n + N_EXPERTS * (tm - 1)) // step) * step
    n_blocks = cap // tm
    token = jnp.arange(2 * n, dtype=jnp.int32) // 2
    slot_token = jnp.zeros((n_blocks * tm,), jnp.int32).at[dest].set(token, unique_indices=True)
    block_expert = jnp.minimum(
        jnp.searchsorted(pend, jnp.arange(n_blocks, dtype=jnp.int32) * tm, side="right"), N_EXPERTS - 1
    ).astype(jnp.int32)
    n_valid = (pend[-1:] // tm).astype(jnp.int32)
    return slot_token, dest, block_expert, n_valid


def _run(x, caches, wts, *, tq_ab, tq_c, expert_rows):
    b, t, d = x.shape
    n = b * t
    xf = x.reshape(n, d)
    (qsb, ksb, vsb, qfx, kfx, vfx, ksb16, vsb16, kfx16, vfx16, logf) = _proj_ab(
        xf, wts["g_ab"], wts["w_in_ab"], wts["w_f"], wts["b_f"])
    r3 = lambda a: a.reshape(b, t, -1)
    logf3 = r3(logf)
    if caches is None:
        c = _cumsum_time(logf3.transpose(0, 2, 1)).transpose(0, 2, 1)
        c_q, c_p = c, c
        kp_sb, vp_sb, kp_fx, vp_fx = r3(ksb16), r3(vsb16), r3(kfx16), r3(vfx16)
        td = tk = min(ATTN_TILE, t)
        base, per_q = 0, 1
    else:
        p_sb_k, p_sb_v, p_fx_k, p_fx_v, p_logf = caches[:5]
        past = p_logf.shape[1]
        pad = (-(past + t)) % LANES
        allf = jnp.concatenate([p_logf.astype(F32), logf3, jnp.zeros((b, pad, N_FOX_HEADS), F32)], axis=1)
        c = _cumsum_time(allf.transpose(0, 2, 1)).transpose(0, 2, 1)
        c_q, c_p = c[:, past:past + t], c[:, :past]
        kp_sb, vp_sb, kp_fx, vp_fx = (a.reshape(b, past, -1) for a in (p_sb_k, p_sb_v, p_fx_k, p_fx_v))
        td, tk = t, min(SAMPLE_ATTN_TILE, past)
        base, per_q = past // tk, 0
    kw = dict(tq=tq_ab, td=td, tk=tk, base=base, per_q=per_q)
    o_sb = _attn_ab(_sb_kernel, r3(qsb), r3(ksb16), r3(vsb16), kp_sb, vp_sb, [], [], name="attn_sb", **kw)
    extra, extra_specs = _fox_bias_operands(c_q, c_p, tq_ab, td, tk)
    o_fx = _attn_ab(_fox_kernel, r3(qfx), r3(kfx16), r3(vfx16), kp_fx, vp_fx, extra, extra_specs,
                    name="attn_fox", **kw)
    x1 = _ffn_dense(xf, o_sb.reshape(n, SB_W), o_fx.reshape(n, FOX_W), wts["w_out_ab"], wts["g_ffn"],
                    wts["w_gate"], wts["w_up"], wts["w_down"])

    qc, kc, vc, kc16, vc16 = _proj_c(x1, wts["g_c"], wts["w_in_c"])
    if caches is None:
        kprev, vprev, pos0 = r3(kc16), r3(vc16), 0
        new_k, new_v = r3(kc)[:, t - SWA_CACHE:], r3(vc)[:, t - SWA_CACHE:]
    else:
        p_k, p_v = (a.reshape(b, SWA_CACHE, SWA_KV_W) for a in caches[5:])
        kprev, vprev, pos0 = p_k, p_v, caches[0].shape[1]
        new_k = jnp.concatenate([p_k, r3(kc)], axis=1)[:, t:]
        new_v = jnp.concatenate([p_v, r3(vc)], axis=1)[:, t:]
    o_c = _swa(wts["sinks"], r3(qc), kprev, vprev, r3(kc16), r3(vc16), tq=tq_c, pos0=pos0,
               prev_is_cache=caches is not None)

    x2, xn, route = _route(x1, o_c.reshape(n, SWA_Q_W), wts["w_out_c"], wts["g_moe"], wts["w_router"])
    slot_token, dest, block_expert, n_valid = _dispatch(route, expert_rows)
    x_slots = _sc_gather(xn, slot_token)
    y_slots = _experts(block_expert, n_valid, x_slots, wts["w_gate_moe"], wts["w_up_moe"], wts["w_down_moe"],
                       tm=expert_rows, tf=512)
    y_pairs = _sc_gather(y_slots, dest.reshape(n, 2).T.reshape(-1))
    y = _combine(y_pairs, x2, route, wts["g_final"])

    heads = lambda a, h: a.reshape(1, b, -1, h, HEAD_DIM)
    states = (heads(ksb, N_SB_HEADS), heads(vsb, N_SB_HEADS), heads(kfx, N_FOX_HEADS), heads(vfx, N_FOX_HEADS),
              logf3[None], heads(new_k, N_SWA_KV_HEADS), heads(new_v, N_SWA_KV_HEADS))
    return y.reshape(b, t, d), states


def kernel(x_prompt, x_sample, cache_sb_k, cache_sb_v, cache_fox_k, cache_fox_v, cache_fox_logf, cache_swa_k,
           cache_swa_v, norm_mix_ab, w_in_ab, b_forget, w_out_ab, norm_ffn_dense, w_gate_dense, w_up_dense,
           w_down_dense, norm_mix_c, w_in_c, sinks, w_out_c, norm_ffn_moe, w_router, w_gate_moe, w_up_moe,
           w_down_moe, norm_final):
    d = x_prompt.shape[-1]
    main_w = 3 * SB_W + 3 * FOX_W
    w_f = jnp.zeros((d, LANES), F32).at[:, :N_FOX_HEADS].set(w_in_ab[0][:, main_w:])
    wr = jnp.zeros((d, LANES), F32).at[:, :N_EXPERTS].set(w_router[0])
    wr_hi = wr.astype(BF16)
    wr_mid = (wr - wr_hi.astype(F32)).astype(BF16)
    wts = dict(
        g_ab=norm_mix_ab[0][None], w_in_ab=w_in_ab[0][:, :main_w].astype(BF16), w_f=w_f.astype(BF16),
        b_f=b_forget[0][None], w_out_ab=w_out_ab[0].astype(BF16), g_ffn=norm_ffn_dense[0][None],
        w_gate=w_gate_dense[0].astype(BF16), w_up=w_up_dense[0].astype(BF16), w_down=w_down_dense[0].astype(BF16),
        g_c=norm_mix_c[0][None], w_in_c=w_in_c[0].astype(BF16), sinks=sinks[0], w_out_c=w_out_c[0].astype(BF16),
        g_moe=norm_ffn_moe[0][None], w_router=jnp.concatenate([wr_hi, wr_mid], axis=0),
        w_gate_moe=w_gate_moe[0], w_up_moe=w_up_moe[0], w_down_moe=w_down_moe[0], g_final=norm_final[None])
    t_p, t_s = x_prompt.shape[1], x_sample.shape[1]
    y_p, st_p = _run(x_prompt, None, wts, tq_ab=min(ATTN_ROWS, t_p), tq_c=min(SWA_TILE, t_p),
                     expert_rows=EXPERT_ROWS)
    caches = (cache_sb_k[0], cache_sb_v[0], cache_fox_k[0], cache_fox_v[0], cache_fox_logf[0],
              cache_swa_k[0], cache_swa_v[0])
    y_s, st_s = _run(x_sample, caches, wts, tq_ab=t_s, tq_c=t_s, expert_rows=SAMPLE_EXPERT_ROWS)
    return (y_p, y_s) + st_p + st_s
```

```python
import functools

import jax
import jax.numpy as jnp
from jax import lax
from jax.experimental import pallas as pl
from jax.experimental.pallas import tpu as pltpu
from jax.experimental.pallas import tpu_sc as plsc

F32 = jnp.float32
BF16 = jnp.bfloat16

HEAD_DIM = 64
LANES = 128
N_SB_HEADS = 8
N_FOX_HEADS = 8
N_SWA_HEADS = 16
N_SWA_KV_HEADS = 2
SWA_GROUP = N_SWA_HEADS // N_SWA_KV_HEADS
CHUNK = 64
WINDOW_CHUNKS = 2
SWA_CACHE = 128
N_EXPERTS = 8
RMS_EPS = 1e-6
NEG_INF = -1e30
ATTN_SCALE = HEAD_DIM ** -0.5
SB_W = N_SB_HEADS * HEAD_DIM
FOX_W = N_FOX_HEADS * HEAD_DIM
SWA_Q_W = N_SWA_HEADS * HEAD_DIM
SWA_KV_W = N_SWA_KV_HEADS * HEAD_DIM

VMEM_LIMIT_BYTES = 56 * 1024 * 1024
ROW_TILE = 512
ATTN_TILE = 512
ATTN_ROWS = 512
SAMPLE_ATTN_TILE = 2048
CUMSUM_SUB = 256
SWA_TILE = 256
SAMPLE_EXPERT_ROWS = 256
LOG2_E = 1.4426950408889634
EXPERT_ROWS = 1024
EXPERT_COLS = 512
SAMPLE_EXPERT_COLS = 896
SUBLANES = 8
GATHER_CHUNK = 256
SC_WINDOW = 128
SLOT_ALIGN = 1024
NT_DIMS = (((1,), (1,)), ((), ()))


def _params(sem):
    return pltpu.CompilerParams(dimension_semantics=sem, vmem_limit_bytes=VMEM_LIMIT_BYTES)


def _resident(shape):
    n = len(shape)
    return pl.BlockSpec(shape, lambda *_: (0,) * n, pipeline_mode=pl.Buffered(1))


def _rms(x, g):
    return x * lax.rsqrt(jnp.mean(x * x, axis=-1, keepdims=True) + RMS_EPS) * g


def _log_sigmoid(z):
    return jnp.minimum(z, 0.0) - jnp.log(1.0 + jnp.exp(-jnp.abs(z)))


def _silu(a):
    return a * (1.0 / (1.0 + jnp.exp(-a)))


def _dot(a, b):
    return jnp.dot(a, b, preferred_element_type=F32)


def _dot_nt(a, b):
    return lax.dot_general(a, b, NT_DIMS, preferred_element_type=F32)


def _split3(x):
    hi = x.astype(BF16)
    r = x - hi.astype(F32)
    mid = r.astype(BF16)
    lo = (r - mid.astype(F32)).astype(BF16)
    return hi, mid, lo


def _proj_ab_kernel(x_ref, g_ref, w_ref, wf_ref, bf_ref,
                    qsb_ref, ksb_ref, vsb_ref, qfx_ref, kfx_ref, vfx_ref,
                    ksb16_ref, vsb16_ref, kfx16_ref, vfx16_ref, logf_ref):
    hn = _rms(x_ref[...], g_ref[...]).astype(BF16)
    f32_outs = (None, ksb_ref, vsb_ref, None, kfx_ref, vfx_ref)
    b16_outs = (qsb_ref, ksb16_ref, vsb16_ref, qfx_ref, kfx16_ref, vfx16_ref)
    for c in range(6):
        p = _dot(hn, w_ref[:, c * SB_W:(c + 1) * SB_W])
        if f32_outs[c] is None:
            b16_outs[c][...] = (p * ATTN_SCALE).astype(BF16)
        else:
            f32_outs[c][...] = p.reshape(p.shape[0], SB_W // HEAD_DIM, HEAD_DIM)
            b16_outs[c][...] = p.astype(BF16)
    f = _dot(hn, wf_ref[...])[:, :N_FOX_HEADS] + bf_ref[...]
    logf_ref[...] = _log_sigmoid(f)


def _proj_ab(x, g, w, wf, bf):
    n, d = x.shape
    tm = min(ROW_TILE, n)
    row = lambda c: pl.BlockSpec((tm, c), lambda i: (i, 0))
    f32o = jax.ShapeDtypeStruct((n, SB_W // HEAD_DIM, HEAD_DIM), F32)
    b16o = jax.ShapeDtypeStruct((n, SB_W), BF16)
    st = pl.BlockSpec((tm, SB_W // HEAD_DIM, HEAD_DIM), lambda i: (i, 0, 0))
    w16 = row(SB_W)
    return pl.pallas_call(
        _proj_ab_kernel,
        grid=(n // tm,),
        in_specs=[row(d), _resident((1, d)), _resident(w.shape), _resident(wf.shape), _resident((1, N_FOX_HEADS))],
        out_specs=[w16, st, st, w16, st, st, w16, w16, w16, w16, row(N_FOX_HEADS)],
        out_shape=[b16o, f32o, f32o, b16o, f32o, f32o, b16o, b16o, b16o, b16o,
                   jax.ShapeDtypeStruct((n, N_FOX_HEADS), F32)],
        compiler_params=_params(("parallel",)),
        name="proj_ab",
    )(x, g, w, wf, bf)


def _cumsum_kernel(x_ref, o_ref):
    t = x_ref.shape[-1]
    r = lax.broadcasted_iota(jnp.int32, (LANES, LANES), 0)
    c = lax.broadcasted_iota(jnp.int32, (LANES, LANES), 1)
    tri = (r <= c).astype(BF16)
    carry = jnp.zeros((x_ref.shape[1], 1), F32)
    for i in range(t // LANES):
        hi, mid, lo = _split3(x_ref[0, :, i * LANES:(i + 1) * LANES])
        cs = _dot(hi, tri) + _dot(mid, tri) + _dot(lo, tri) + carry
        o_ref[0, :, i * LANES:(i + 1) * LANES] = cs
        carry = cs[:, LANES - 1:LANES]


def _cumsum_time(x):
    b, h, t = x.shape
    spec = pl.BlockSpec((1, h, t), lambda i: (i, 0, 0))
    return pl.pallas_call(
        _cumsum_kernel, grid=(b,), in_specs=[spec], out_specs=spec,
        out_shape=jax.ShapeDtypeStruct(x.shape, F32),
        compiler_params=_params(("parallel",)), name="cumsum_time",
    )(x)


def _stacked_heads(q):
    lane = lax.broadcasted_iota(jnp.int32, q.shape, 1)
    zero = jnp.zeros_like(q)
    return jnp.concatenate([jnp.where(lane < HEAD_DIM, q, zero), jnp.where(lane >= HEAD_DIM, q, zero)], axis=0)


def _unstack_heads(acc):
    tq = acc.shape[0] // 2
    lane = lax.broadcasted_iota(jnp.int32, (tq, LANES), 1)
    return jnp.where(lane < HEAD_DIM, acc[:tq], acc[tq:])


def _causal_mask(tq, td, first_row, strict):
    row = first_row + lax.broadcasted_iota(jnp.int32, (2 * tq, td), 0) % tq
    col = lax.broadcasted_iota(jnp.int32, (2 * tq, td), 1)
    return col < row if strict else col <= row


def _strict_upper(n):
    j = lax.broadcasted_iota(jnp.int32, (n, n), 0)
    s = lax.broadcasted_iota(jnp.int32, (n, n), 1)
    return (j > s).astype(BF16)


def _sb_block(qk, v, total, acc_ref, u, vis):
    sub = u.shape[0]
    z = qk * LOG2_E
    neg_abs = lax.bitcast_convert_type(lax.bitcast_convert_type(z, jnp.uint32) | jnp.uint32(0x80000000), F32)
    ls = jnp.minimum(z, 0.0) - jnp.log2(1.0 + jnp.exp2(neg_abs))
    lk = ls - z
    if vis is not None:
        lk = jnp.where(vis, lk, 0.0)
    lk = lk.astype(BF16)
    for s in reversed(range(qk.shape[1] // sub)):
        cols = slice(s * sub, (s + 1) * sub)
        after = _dot(lk[:, cols], u)
        w = jnp.exp2(ls[:, cols] + after + total)
        if vis is not None:
            w = jnp.where(vis[:, cols], w, 0.0)
        acc_ref[...] += _dot(w.astype(BF16), v[cols, :])
        total = total + after[:, 0:1] + lk[:, s * sub:s * sub + 1].astype(F32)
    return total


def _sb_kernel(q_ref, kd_ref, vd_ref, kp_ref, vp_ref, o_ref, acc_ref, *, tk, base, per_q):
    tq, td = q_ref.shape[1], kd_ref.shape[1]
    qs = _stacked_heads(q_ref[0])
    first_row = (pl.program_id(2) % (td // tq)) * tq
    vis = _causal_mask(tq, td, first_row, strict=True)
    u_diag = _strict_upper(min(CUMSUM_SUB, td))
    u_past = _strict_upper(CUMSUM_SUB)
    acc_ref[...] = jnp.zeros_like(acc_ref)
    total = _sb_block(_dot_nt(qs, kd_ref[0]), vd_ref[0], jnp.zeros((2 * tq, 1), F32), acc_ref, u_diag, vis)

    n_past = base + pl.program_id(2) // (td // tq) * per_q

    def body(jj, total):
        start = pl.multiple_of((n_past - 1 - jj) * tk, tk)
        k = kp_ref[0, pl.ds(start, tk), :].astype(BF16)
        v = vp_ref[0, pl.ds(start, tk), :].astype(BF16)
        return _sb_block(_dot_nt(qs, k), v, total, acc_ref, u_past, None)

    lax.fori_loop(0, n_past, body, total)
    o_ref[0] = _unstack_heads(acc_ref[...]).astype(o_ref.dtype)


def _fox_block(qk, v, ck, m, l, acc_ref, vis):
    tq = qk.shape[0] // 2
    s = jnp.concatenate([qk[:tq] - ck[0:1, :], qk[tq:] - ck[1:2, :]], axis=0)
    if vis is not None:
        s = jnp.where(vis, s, NEG_INF)
    m_new = jnp.maximum(m, jnp.max(s, axis=1, keepdims=True))
    alpha = jnp.exp(m - m_new)
    p = jnp.exp(s - m_new)
    acc_ref[...] = alpha * acc_ref[...] + _dot(p.astype(BF16), v)
    return m_new, alpha * l + jnp.sum(p, axis=1, keepdims=True)


def _fox_kernel(q_ref, kd_ref, vd_ref, kp_ref, vp_ref, ckd_ref, ckp_ref, o_ref, acc_ref, *, tk, base, per_q):
    tq, td = q_ref.shape[1], kd_ref.shape[1]
    qs = _stacked_heads(q_ref[0])
    first_row = (pl.program_id(2) % (td // tq)) * tq
    vis = _causal_mask(tq, td, first_row, strict=False)
    acc_ref[...] = jnp.zeros_like(acc_ref)
    m0 = jnp.full((2 * tq, 1), NEG_INF, F32)
    l0 = jnp.zeros((2 * tq, 1), F32)
    stats = _fox_block(_dot_nt(qs, kd_ref[0]), vd_ref[0], ckd_ref[0, 0], m0, l0, acc_ref, vis)

    n_past = base + pl.program_id(2) // (td // tq) * per_q

    def body(j, stats):
        start = pl.multiple_of(j * tk, tk)
        k = kp_ref[0, pl.ds(start, tk), :].astype(BF16)
        v = vp_ref[0, pl.ds(start, tk), :].astype(BF16)
        return _fox_block(_dot_nt(qs, k), v, ckp_ref[0, 0, j], stats[0], stats[1], acc_ref, None)

    _, l = lax.fori_loop(0, n_past, body, stats)
    o_ref[0] = _unstack_heads(acc_ref[...] / l).astype(o_ref.dtype)


def _attn_ab(kernel_fn, q, kd, vd, kp, vp, extra, extra_specs, *, tq, td, tk, base, per_q, name):
    b, t, w = q.shape
    tp = kp.shape[1]
    grid = (b, w // LANES, t // tq)
    qspec = pl.BlockSpec((1, tq, LANES), lambda bi, hp, i: (bi, i, hp))
    dspec = pl.BlockSpec((1, td, LANES), lambda bi, hp, i: (bi, i // (td // tq), hp))
    pspec = pl.BlockSpec((1, tp, LANES), lambda bi, hp, i: (bi, 0, hp))
    return pl.pallas_call(
        functools.partial(kernel_fn, tk=tk, base=base, per_q=per_q),
        grid=grid,
        in_specs=[qspec, dspec, dspec, pspec, pspec] + extra_specs,
        out_specs=qspec,
        out_shape=jax.ShapeDtypeStruct(q.shape, BF16),
        scratch_shapes=[pltpu.VMEM((2 * tq, LANES), F32)],
        compiler_params=_params(("parallel", "parallel", "arbitrary")),
        name=name,
    )(q, kd, vd, kp, vp, *extra)


def _fox_bias_operands(c_q, c_p, tq, td, tk):
    b, t, h = c_q.shape
    tp = c_p.shape[1]
    hp = h // 2
    ckd = c_q.reshape(b, t, hp, 2).transpose(0, 2, 3, 1)
    ckp = c_p.reshape(b, tp // tk, tk, hp, 2).transpose(0, 3, 1, 4, 2)
    specs = [pl.BlockSpec((1, 1, 2, td), lambda bi, p, i: (bi, p, 0, i // (td // tq))),
             pl.BlockSpec((1, 1, tp // tk, 2, tk), lambda bi, p, i: (bi, p, 0, 0, 0))]
    return [ckd, ckp], specs


def _ffn_dense_kernel(x_ref, osb_ref, ofx_ref, wo_ref, g_ref, wg_ref, wu_ref, wd_ref, o_ref, *, tf):
    o_ref[...] = x_ref[...] + _dot(osb_ref[...], wo_ref[:SB_W, :]) + _dot(ofx_ref[...], wo_ref[SB_W:, :])
    hn = _rms(o_ref[...], g_ref[...]).astype(BF16)
    for c in range(wg_ref.shape[1] // tf):
        a = _dot(hn, wg_ref[:, c * tf:(c + 1) * tf])
        u = _dot(hn, wu_ref[:, c * tf:(c + 1) * tf])
        o_ref[...] += _dot((_silu(a) * u).astype(BF16), wd_ref[c * tf:(c + 1) * tf, :])


def _ffn_dense(x, osb, ofx, wo, g, wg, wu, wd):
    n, d = x.shape
    tm = min(ROW_TILE, n)
    row = lambda c: pl.BlockSpec((tm, c), lambda i: (i, 0))
    return pl.pallas_call(
        functools.partial(_ffn_dense_kernel, tf=256),
        grid=(n // tm,),
        in_specs=[row(d), row(SB_W), row(FOX_W), _resident(wo.shape), _resident((1, d)),
                  _resident(wg.shape), _resident(wu.shape), _resident(wd.shape)],
        out_specs=row(d),
        out_shape=jax.ShapeDtypeStruct((n, d), F32),
        compiler_params=_params(("parallel",)),
        name="ffn_dense",
    )(x, osb, ofx, wo, g, wg, wu, wd)


def _proj_c_kernel(x_ref, g_ref, w_ref, q_ref, k_ref, v_ref, k16_ref, v16_ref):
    hn = _rms(x_ref[...], g_ref[...]).astype(BF16)
    q_ref[...] = (_dot(hn, w_ref[:, :SWA_Q_W]) * ATTN_SCALE).astype(BF16)
    k = _dot(hn, w_ref[:, SWA_Q_W:SWA_Q_W + SWA_KV_W])
    v = _dot(hn, w_ref[:, SWA_Q_W + SWA_KV_W:])
    k_ref[...] = k
    v_ref[...] = v
    k16_ref[...] = k.astype(BF16)
    v16_ref[...] = v.astype(BF16)


def _proj_c(x, g, w):
    n, d = x.shape
    tm = min(ROW_TILE, n)
    row = lambda c: pl.BlockSpec((tm, c), lambda i: (i, 0))
    kv32 = jax.ShapeDtypeStruct((n, SWA_KV_W), F32)
    kv16 = jax.ShapeDtypeStruct((n, SWA_KV_W), BF16)
    return pl.pallas_call(
        _proj_c_kernel,
        grid=(n // tm,),
        in_specs=[row(d), _resident((1, d)), _resident(w.shape)],
        out_specs=[row(SWA_Q_W)] + [row(SWA_KV_W)] * 4,
        out_shape=[jax.ShapeDtypeStruct((n, SWA_Q_W), BF16), kv32, kv32, kv16, kv16],
        compiler_params=_params(("parallel",)),
        name="proj_c",
    )(x, g, w)


def _swap_halves(a):
    return jnp.concatenate([a[:, HEAD_DIM:], a[:, :HEAD_DIM]], axis=1)


def _swa_kernel(sink_ref, q_ref, kprev_ref, vprev_ref, kcur_ref, vcur_ref, o_ref, *, pos0):
    tq = q_ref.shape[1]
    band = (WINDOW_CHUNKS + 1) * CHUNK
    stack = SWA_GROUP // 2
    rows = stack * CHUNK
    first = pos0 + pl.program_id(1) * tq
    kw = jnp.concatenate([kprev_ref[0].astype(BF16), kcur_ref[0].astype(BF16)], axis=0)
    vw = jnp.concatenate([vprev_ref[0].astype(BF16), vcur_ref[0].astype(BF16)], axis=0)
    k_by_half = (kw, _swap_halves(kw))
    v_by_half = (vw, _swap_halves(vw))
    q = q_ref[0]
    in_low = lax.broadcasted_iota(jnp.int32, q.shape, 1) % LANES < HEAD_DIM
    q_by_half = (jnp.where(in_low, q, jnp.zeros_like(q)), jnp.where(in_low, jnp.zeros_like(q), q))
    r_id = lax.broadcasted_iota(jnp.int32, (rows, band), 0)
    c_id = lax.broadcasted_iota(jnp.int32, (rows, band), 1)
    dist = jnp.abs(WINDOW_CHUNKS * CHUNK + r_id % CHUNK - c_id).astype(F32)
    head_in_stack = lax.broadcasted_iota(jnp.int32, (rows, 1), 0) // CHUNK
    col = lax.broadcasted_iota(jnp.int32, (1, band), 1)
    lane = lax.broadcasted_iota(jnp.int32, (CHUNK, LANES), 1)

    def per_head(values):
        out = jnp.full((rows, 1), values[-1], F32)
        for i in range(stack - 2, -1, -1):
            out = jnp.where(head_in_stack == i, values[i], out)
        return out

    heads = {(g, half): [g * SWA_GROUP + 2 * i + half for i in range(stack)]
             for g in range(N_SWA_KV_HEADS) for half in range(2)}
    bias = {key: per_head([2.0 ** (-8.0 * (h + 1) / N_SWA_HEADS) for h in hs]) * dist for key, hs in heads.items()}
    sink = {key: per_head([sink_ref[h] for h in hs]) for key, hs in heads.items()}

    for c in range(tq // CHUNK):
        in_range = first - WINDOW_CHUNKS * CHUNK + c * CHUNK + col >= 0
        win = slice(c * CHUNK, c * CHUNK + band)
        outs = {}
        for (g, half), hs in heads.items():
            qs = jnp.concatenate([q_by_half[half][c * CHUNK:(c + 1) * CHUNK, (h // 2) * LANES:(h // 2 + 1) * LANES]
                                  for h in hs], axis=0)
            s = _dot_nt(qs, k_by_half[g ^ half][win]) - bias[g, half]
            s = jnp.where(in_range, s, NEG_INF)
            m = jnp.maximum(jnp.max(s, axis=1, keepdims=True), sink[g, half])
            p = jnp.exp(s - m)
            l = jnp.sum(p, axis=1, keepdims=True) + jnp.exp(sink[g, half] - m)
            outs[g, half] = _dot(p.astype(BF16), v_by_half[g ^ half][win]) / l
        for g in range(N_SWA_KV_HEADS):
            for i in range(stack):
                pair = (g * SWA_GROUP + 2 * i) // 2
                o_ref[0, c * CHUNK:(c + 1) * CHUNK, pair * LANES:(pair + 1) * LANES] = jnp.where(
                    lane < HEAD_DIM, outs[g, 0][i * CHUNK:(i + 1) * CHUNK],
                    outs[g, 1][i * CHUNK:(i + 1) * CHUNK]).astype(o_ref.dtype)


def _swa(sinks, q, kprev, vprev, kcur, vcur, *, tq, pos0, prev_is_cache):
    b, t, _ = q.shape
    per = tq // SWA_CACHE
    prev_map = ((lambda bi, i, s: (bi, 0, 0)) if prev_is_cache
                else (lambda bi, i, s: (bi, jnp.maximum(i * per - 1, 0), 0)))
    cur = lambda w: pl.BlockSpec((1, tq, w), lambda bi, i, s: (bi, i, 0))
    prev = pl.BlockSpec((1, SWA_CACHE, SWA_KV_W), prev_map)
    return pl.pallas_call(
        functools.partial(_swa_kernel, pos0=pos0),
        grid_spec=pltpu.PrefetchScalarGridSpec(
            num_scalar_prefetch=1, grid=(b, t // tq),
            in_specs=[cur(SWA_Q_W), prev, prev, cur(SWA_KV_W), cur(SWA_KV_W)],
            out_specs=cur(SWA_Q_W)),
        out_shape=jax.ShapeDtypeStruct(q.shape, BF16),
        compiler_params=_params(("parallel", "arbitrary")),
        name="swa",
    )(sinks, q, kprev, vprev, kcur, vcur)


def _route_kernel(x_ref, o_ref, wo_ref, g_ref, wr_ref, x2_ref, xn_ref, route_ref):
    x2 = x_ref[...] + _dot(o_ref[...], wo_ref[...])
    xn = _rms(x2, g_ref[...])
    x2_ref[...] = x2
    _store_chunked(xn_ref, xn)
    hi = xn.astype(BF16)
    mid = (xn - hi.astype(F32)).astype(BF16)
    d = xn.shape[1]
    logits = _dot(hi, wr_ref[:d, :]) + _dot(hi, wr_ref[d:, :]) + _dot(mid, wr_ref[:d, :])
    lane = lax.broadcasted_iota(jnp.int32, logits.shape, 1).astype(F32)
    lg = jnp.where(lane < N_EXPERTS, logits, -jnp.inf)
    m1 = jnp.max(lg, axis=1, keepdims=True)
    i1 = jnp.min(jnp.where(lg == m1, lane, float(LANES)), axis=1, keepdims=True)
    lg2 = jnp.where(lane == i1, -jnp.inf, lg)
    m2 = jnp.max(lg2, axis=1, keepdims=True)
    i2 = jnp.min(jnp.where(lg2 == m2, lane, float(LANES)), axis=1, keepdims=True)
    e = jnp.exp(m2 - m1)
    g1 = 1.0 / (1.0 + e)
    g2 = e / (1.0 + e)
    route = jnp.where(lane == 0, i1, jnp.where(lane == 1, i2, jnp.where(lane == 2, g1, jnp.where(lane == 3, g2, 0.0))))
    route_ref[...] = route


def _route(x, o, wo, g, wr):
    n, d = x.shape
    tm = min(ROW_TILE, n)
    row = lambda c: pl.BlockSpec((tm, c), lambda i: (i, 0))
    xs = jax.ShapeDtypeStruct((n, d), F32)
    return pl.pallas_call(
        _route_kernel,
        grid=(n // tm,),
        in_specs=[row(d), row(SWA_Q_W), _resident(wo.shape), _resident((1, d)), _resident(wr.shape)],
        out_specs=[row(d), _chunked_spec(tm, d), row(LANES)],
        out_shape=[xs, _chunked_shape(n, d), jax.ShapeDtypeStruct((n, LANES), F32)],
        compiler_params=_params(("parallel",)),
        name="route",
    )(x, o, wo, g, wr)


def _chunked_shape(rows, d):
    return jax.ShapeDtypeStruct((rows // SUBLANES, d // GATHER_CHUNK, SUBLANES, GATHER_CHUNK), F32)


def _chunked_spec(tm, d, index_map=lambda i: (i, 0, 0, 0)):
    return pl.BlockSpec((tm // SUBLANES, d // GATHER_CHUNK, SUBLANES, GATHER_CHUNK), index_map)


def _store_chunked(ref, x):
    tm = x.shape[0]
    for j in range(x.shape[1] // GATHER_CHUNK):
        ref[:, j] = x[:, j * GATHER_CHUNK:(j + 1) * GATHER_CHUNK].reshape(tm // SUBLANES, SUBLANES, GATHER_CHUNK)


def _load_chunk(ref, j):
    g = ref.shape[0]
    return ref[:, j].reshape(g * SUBLANES, GATHER_CHUNK)


def _load_chunked(ref):
    return jnp.concatenate([_load_chunk(ref, j) for j in range(ref.shape[1])], axis=1)


def _sc_gather_flat(flat, idx):
    m, w = idx.shape[1], flat.shape[1]
    mesh = plsc.VectorSubcoreMesh(core_axis_name="core", subcore_axis_name="subcore")

    @pl.kernel(out_type=jax.ShapeDtypeStruct((m, w), flat.dtype), mesh=mesh)
    def gather(x_hbm, i_hbm, o_hbm):
        def body(i_vmem, o_vmem):
            pltpu.sync_copy(x_hbm.at[i_vmem.at[0]], o_vmem)

        pltpu.emit_pipeline(
            body,
            grid=(m // SC_WINDOW,),
            in_specs=[pl.BlockSpec((1, SC_WINDOW), index_map=lambda i: (0, i))],
            out_specs=[pl.BlockSpec((SC_WINDOW, w), index_map=lambda i: (i, 0))],
            core_axis_name=("core", "subcore"),
            dimension_semantics=(pltpu.PARALLEL,),
        )(i_hbm, o_hbm)

    return gather(flat, idx)


def _sc_gather(src, rows):
    _, n_chunks, _, w = src.shape
    m = rows.shape[0]
    src_row = rows.reshape(m // SUBLANES, 1, SUBLANES)
    chunk = jnp.arange(n_chunks, dtype=jnp.int32).reshape(1, n_chunks, 1)
    idx = (src_row // SUBLANES) * (SUBLANES * n_chunks) + src_row % SUBLANES + SUBLANES * chunk
    out = _sc_gather_flat(src.reshape(-1, w), idx.reshape(1, -1))
    return out.reshape(m // SUBLANES, n_chunks, SUBLANES, w)


def _expert_kernel(be_ref, nv_ref, x_ref, wg_ref, wu_ref, wd_ref, o_ref, xb_ref, acc_ref):
    blk, f = pl.program_id(0), pl.program_id(1)
    last = pl.num_programs(1) - 1
    valid = blk < nv_ref[0]
    n_chunks = x_ref.shape[1]

    @pl.when(valid & (f == 0))
    def _():
        for j in range(n_chunks):
            xb_ref[:, j * GATHER_CHUNK:(j + 1) * GATHER_CHUNK] = _load_chunk(x_ref, j).astype(BF16)
        acc_ref[...] = jnp.zeros_like(acc_ref)

    @pl.when(valid)
    def _():
        xb = xb_ref[...]
        h = (_silu(_dot(xb, wg_ref[0].astype(BF16))) * _dot(xb, wu_ref[0].astype(BF16))).astype(BF16)
        acc_ref[...] += _dot(h, wd_ref[0].astype(BF16))

    @pl.when(valid & (f == last))
    def _():
        _store_chunked(o_ref, acc_ref[...])

    @pl.when(jnp.logical_not(valid) & (f == last))
    def _():
        o_ref[...] = jnp.zeros_like(o_ref)


def _experts(block_expert, n_valid, x_slots, wg, wu, wd, *, tm, tf):
    d = wg.shape[1]
    cap = x_slots.shape[0] * SUBLANES
    nf = wg.shape[2] // tf
    fsel = lambda b, f, nv: jnp.where(b < nv[0], f, nf - 1)
    slots = _chunked_spec(tm, d, lambda b, f, be, nv: (b, 0, 0, 0))
    return pl.pallas_call(
        _expert_kernel,
        grid_spec=pltpu.PrefetchScalarGridSpec(
            num_scalar_prefetch=2, grid=(cap // tm, nf),
            in_specs=[slots,
                      pl.BlockSpec((1, d, tf), lambda b, f, be, nv: (be[b], 0, fsel(b, f, nv))),
                      pl.BlockSpec((1, d, tf), lambda b, f, be, nv: (be[b], 0, fsel(b, f, nv))),
                      pl.BlockSpec((1, tf, d), lambda b, f, be, nv: (be[b], fsel(b, f, nv), 0))],
            out_specs=slots,
            scratch_shapes=[pltpu.VMEM((tm, d), BF16), pltpu.VMEM((tm, d), F32)]),
        out_shape=_chunked_shape(cap, d),
        compiler_params=_params(("arbitrary", "arbitrary")),
        name="moe_experts",
    )(block_expert, n_valid, x_slots, wg, wu, wd)


def _combine_kernel(y1_ref, y2_ref, x_ref, route_ref, g_ref, o_ref):
    route = route_ref[...]
    y = x_ref[...] + (_load_chunked(y1_ref) * route[:, 2:3] + _load_chunked(y2_ref) * route[:, 3:4])
    o_ref[...] = _rms(y, g_ref[...])


def _combine(y_pairs, x2, route, g):
    n, d = x2.shape
    tm = min(ROW_TILE, n)
    row = lambda c: pl.BlockSpec((tm, c), lambda i: (i, 0))
    second = n // tm
    return pl.pallas_call(
        _combine_kernel,
        grid=(n // tm,),
        in_specs=[_chunked_spec(tm, d), _chunked_spec(tm, d, lambda i: (i + second, 0, 0, 0)),
                  row(d), row(LANES), _resident((1, d))],
        out_specs=row(d),
        out_shape=jax.ShapeDtypeStruct((n, d), F32),
        compiler_params=_params(("parallel",)),
        name="moe_combine",
    )(y_pairs, y_pairs, x2, route, g)


def _dispatch(route, tm):
    n = route.shape[0]
    expert = route[:, :2].astype(jnp.int32).reshape(-1)
    onehot = (expert[:, None] == jnp.arange(N_EXPERTS, dtype=jnp.int32)[None, :]).astype(jnp.int32)
    csum = jnp.cumsum(onehot, axis=0)
    rank = jnp.sum(onehot * csum, axis=1) - 1
    counts = csum[-1]
    padded = (counts + tm - 1) // tm * tm
    pend = jnp.cumsum(padded)
    pstart = pend - padded
    dest = (jnp.sum(onehot * pstart[None, :], axis=1) + rank).astype(jnp.int32)
    step = max(tm, SLOT_ALIGN)
    cap = -(-(2 * n + N_EXPERTS * (tm - 1)) // step) * step
    n_blocks = cap // tm
    token = jnp.arange(2 * n, dtype=jnp.int32) // 2
    slot_token = jnp.zeros((n_blocks * tm,), jnp.int32).at[dest].set(token, unique_indices=True)
    block_expert = jnp.minimum(
        jnp.searchsorted(pend, jnp.arange(n_blocks, dtype=jnp.int32) * tm, side="right"), N_EXPERTS - 1
    ).astype(jnp.int32)
    n_valid = (pend[-1:] // tm).astype(jnp.int32)
    return slot_token, dest, block_expert, n_valid


def _run(x, caches, wts, *, tq_ab, tq_c, expert_rows):
    b, t, d = x.shape
    n = b * t
    xf = x.reshape(n, d)
    (qsb, ksb, vsb, qfx, kfx, vfx, ksb16, vsb16, kfx16, vfx16, logf) = _proj_ab(
        xf, wts["g_ab"], wts["w_in_ab"], wts["w_f"], wts["b_f"])
    r3 = lambda a: a.reshape(b, t, -1)
    logf3 = r3(logf)
    if caches is None:
        c = _cumsum_time(logf3.transpose(0, 2, 1)).transpose(0, 2, 1)
        c_q, c_p = c, c
        kp_sb, vp_sb, kp_fx, vp_fx = r3(ksb16), r3(vsb16), r3(kfx16), r3(vfx16)
        td = tk = min(ATTN_TILE, t)
        base, per_q = 0, 1
    else:
        p_sb_k, p_sb_v, p_fx_k, p_fx_v, p_logf = caches[:5]
        past = p_logf.shape[1]
        pad = (-(past + t)) % LANES
        allf = jnp.concatenate([p_logf.astype(F32), logf3, jnp.zeros((b, pad, N_FOX_HEADS), F32)], axis=1)
        c = _cumsum_time(allf.transpose(0, 2, 1)).transpose(0, 2, 1)
        c_q, c_p = c[:, past:past + t], c[:, :past]
        kp_sb, vp_sb, kp_fx, vp_fx = (a.reshape(b, past, -1) for a in (p_sb_k, p_sb_v, p_fx_k, p_fx_v))
        td, tk = t, min(SAMPLE_ATTN_TILE, past)
        base, per_q = past // tk, 0
    kw = dict(tq=tq_ab, td=td, tk=tk, base=base, per_q=per_q)
    o_sb = _attn_ab(_sb_kernel, r3(qsb), r3(ksb16), r3(vsb16), kp_sb, vp_sb, [], [], name="attn_sb", **kw)
    extra, extra_specs = _fox_bias_operands(c_q, c_p, tq_ab, td, tk)
    o_fx = _attn_ab(_fox_kernel, r3(qfx), r3(kfx16), r3(vfx16), kp_fx, vp_fx, extra, extra_specs,
                    name="attn_fox", **kw)
    x1 = _ffn_dense(xf, o_sb.reshape(n, SB_W), o_fx.reshape(n, FOX_W), wts["w_out_ab"], wts["g_ffn"],
                    wts["w_gate"], wts["w_up"], wts["w_down"])

    qc, kc, vc, kc16, vc16 = _proj_c(x1, wts["g_c"], wts["w_in_c"])
    if caches is None:
        kprev, vprev, pos0 = r3(kc16), r3(vc16), 0
        new_k, new_v = r3(kc)[:, t - SWA_CACHE:], r3(vc)[:, t - SWA_CACHE:]
    else:
        p_k, p_v = (a.reshape(b, SWA_CACHE, SWA_KV_W) for a in caches[5:])
        kprev, vprev, pos0 = p_k, p_v, caches[0].shape[1]
        new_k = jnp.concatenate([p_k, r3(kc)], axis=1)[:, t:]
        new_v = jnp.concatenate([p_v, r3(vc)], axis=1)[:, t:]
    o_c = _swa(wts["sinks"], r3(qc), kprev, vprev, r3(kc16), r3(vc16), tq=tq_c, pos0=pos0,
               prev_is_cache=caches is not None)

    x2, xn, route = _route(x1, o_c.reshape(n, SWA_Q_W), wts["w_out_c"], wts["g_moe"], wts["w_router"])
    slot_token, dest, block_expert, n_valid = _dispatch(route, expert_rows)
    x_slots = _sc_gather(xn, slot_token)
    heads = lambda a, h: a.reshape(1, b, -1, h, HEAD_DIM)
    states = (heads(ksb, N_SB_HEADS), heads(vsb, N_SB_HEADS), heads(kfx, N_FOX_HEADS), heads(vfx, N_FOX_HEADS),
              logf3[None], heads(new_k, N_SWA_KV_HEADS), heads(new_v, N_SWA_KV_HEADS))
    return dict(x_slots=x_slots, dest=dest, block_expert=block_expert, n_valid=n_valid, x2=x2, route=route,
                states=states, shape=(b, t, d), expert_rows=expert_rows)


def _moe_experts(ctx, wts):
    b, t, _ = ctx["shape"]
    y_slots = _experts(ctx["block_expert"], ctx["n_valid"], ctx["x_slots"], wts["w_gate_moe"], wts["w_up_moe"],
                       wts["w_down_moe"], tm=ctx["expert_rows"],
                       tf=EXPERT_COLS if ctx["expert_rows"] == EXPERT_ROWS else SAMPLE_EXPERT_COLS)
    pair_rows = ctx["dest"].reshape(b * t, 2).T.reshape(-1)
    return dict(ctx, y_pairs=_sc_gather(y_slots, pair_rows))


def _moe_combine(ctx, wts):
    y = _combine(ctx["y_pairs"], ctx["x2"], ctx["route"], wts["g_final"])
    return y.reshape(ctx["shape"]), ctx["states"]


def kernel(x_prompt, x_sample, cache_sb_k, cache_sb_v, cache_fox_k, cache_fox_v, cache_fox_logf, cache_swa_k,
           cache_swa_v, norm_mix_ab, w_in_ab, b_forget, w_out_ab, norm_ffn_dense, w_gate_dense, w_up_dense,
           w_down_dense, norm_mix_c, w_in_c, sinks, w_out_c, norm_ffn_moe, w_router, w_gate_moe, w_up_moe,
           w_down_moe, norm_final):
    d = x_prompt.shape[-1]
    main_w = 3 * SB_W + 3 * FOX_W
    w_f = jnp.zeros((d, LANES), F32).at[:, :N_FOX_HEADS].set(w_in_ab[0][:, main_w:])
    wr = jnp.zeros((d, LANES), F32).at[:, :N_EXPERTS].set(w_router[0])
    wr_hi = wr.astype(BF16)
    wr_mid = (wr - wr_hi.astype(F32)).astype(BF16)
    wts = dict(
        g_ab=norm_mix_ab[0][None], w_in_ab=w_in_ab[0][:, :main_w].astype(BF16), w_f=w_f.astype(BF16),
        b_f=b_forget[0][None], w_out_ab=w_out_ab[0].astype(BF16), g_ffn=norm_ffn_dense[0][None],
        w_gate=w_gate_dense[0].astype(BF16), w_up=w_up_dense[0].astype(BF16), w_down=w_down_dense[0].astype(BF16),
        g_c=norm_mix_c[0][None], w_in_c=w_in_c[0].astype(BF16), sinks=sinks[0], w_out_c=w_out_c[0].astype(BF16),
        g_moe=norm_ffn_moe[0][None], w_router=jnp.concatenate([wr_hi, wr_mid], axis=0),
        w_gate_moe=w_gate_moe[0], w_up_moe=w_up_moe[0], w_down_moe=w_down_moe[0], g_final=norm_final[None])
    t_p, t_s = x_prompt.shape[1], x_sample.shape[1]
    caches = (cache_sb_k[0], cache_sb_v[0], cache_fox_k[0], cache_fox_v[0], cache_fox_logf[0],
              cache_swa_k[0], cache_swa_v[0])
    prompt = _run(x_prompt, None, wts, tq_ab=min(ATTN_ROWS, t_p), tq_c=min(SWA_TILE, t_p), expert_rows=EXPERT_ROWS)
    sample = _run(x_sample, caches, wts, tq_ab=t_s, tq_c=t_s, expert_rows=SAMPLE_EXPERT_ROWS)
    prompt = _moe_experts(prompt, wts)
    sample = _moe_experts(sample, wts)
    y_p, st_p = _moe_combine(prompt, wts)
    y_s, st_s = _moe_combine(sample, wts)
    return (y_p, y_s) + st_p + st_s
```

```python
import functools

import jax
import jax.numpy as jnp
from jax import lax
from jax.experimental import pallas as pl
from jax.experimental.pallas import tpu as pltpu
from jax.experimental.pallas import tpu_sc as plsc

F32 = jnp.float32
BF16 = jnp.bfloat16

HEAD_DIM = 64
LANES = 128
N_SB_HEADS = 8
N_FOX_HEADS = 8
N_SWA_HEADS = 16
N_SWA_KV_HEADS = 2
SWA_GROUP = N_SWA_HEADS // N_SWA_KV_HEADS
CHUNK = 64
WINDOW_CHUNKS = 2
SWA_CACHE = 128
N_EXPERTS = 8
RMS_EPS = 1e-6
NEG_INF = -1e30
ATTN_SCALE = HEAD_DIM ** -0.5
SB_W = N_SB_HEADS * HEAD_DIM
FOX_W = N_FOX_HEADS * HEAD_DIM
SWA_Q_W = N_SWA_HEADS * HEAD_DIM
SWA_KV_W = N_SWA_KV_HEADS * HEAD_DIM

VMEM_LIMIT_BYTES = 56 * 1024 * 1024
ROW_TILE = 512
ATTN_TILE = 512
ATTN_ROWS = 512
SAMPLE_ATTN_TILE = 2048
CUMSUM_SUB = 256
SWA_TILE = 256
SAMPLE_EXPERT_ROWS = 256
LOG2_E = 1.4426950408889634
EXPERT_ROWS = 1024
EXPERT_COLS = 512
SAMPLE_EXPERT_COLS = 896
GATHER_CHUNK = 256
SC_WINDOW = 128
SLOT_ALIGN = 1024
NT_DIMS = (((1,), (1,)), ((), ()))


def _params(sem):
    return pltpu.CompilerParams(dimension_semantics=sem, vmem_limit_bytes=VMEM_LIMIT_BYTES)


def _resident(shape):
    n = len(shape)
    return pl.BlockSpec(shape, lambda *_: (0,) * n, pipeline_mode=pl.Buffered(1))


def _rms(x, g):
    return x * lax.rsqrt(jnp.mean(x * x, axis=-1, keepdims=True) + RMS_EPS) * g


def _log_sigmoid(z):
    return jnp.minimum(z, 0.0) - jnp.log(1.0 + jnp.exp(-jnp.abs(z)))


def _silu(a):
    return a * (1.0 / (1.0 + jnp.exp(-a)))


def _dot(a, b):
    return jnp.dot(a, b, preferred_element_type=F32)


def _dot_nt(a, b):
    return lax.dot_general(a, b, NT_DIMS, preferred_element_type=F32)


def _split3(x):
    hi = x.astype(BF16)
    r = x - hi.astype(F32)
    mid = r.astype(BF16)
    lo = (r - mid.astype(F32)).astype(BF16)
    return hi, mid, lo


def _proj_ab_kernel(x_ref, g_ref, w_ref, wf_ref, bf_ref,
                    qsb_ref, ksb_ref, vsb_ref, qfx_ref, kfx_ref, vfx_ref,
                    ksb16_ref, vsb16_ref, kfx16_ref, vfx16_ref, logf_ref):
    hn = _rms(x_ref[...], g_ref[...]).astype(BF16)
    f32_outs = (None, ksb_ref, vsb_ref, None, kfx_ref, vfx_ref)
    b16_outs = (qsb_ref, ksb16_ref, vsb16_ref, qfx_ref, kfx16_ref, vfx16_ref)
    for c in range(6):
        p = _dot(hn, w_ref[:, c * SB_W:(c + 1) * SB_W])
        if f32_outs[c] is None:
            b16_outs[c][...] = (p * ATTN_SCALE).astype(BF16)
        else:
            f32_outs[c][...] = p.reshape(p.shape[0], SB_W // HEAD_DIM, HEAD_DIM)
            b16_outs[c][...] = p.astype(BF16)
    f = _dot(hn, wf_ref[...])[:, :N_FOX_HEADS] + bf_ref[...]
    logf_ref[...] = _log_sigmoid(f)


def _proj_ab(x, g, w, wf, bf):
    n, d = x.shape
    tm = min(ROW_TILE, n)
    row = lambda c: pl.BlockSpec((tm, c), lambda i: (i, 0))
    f32o = jax.ShapeDtypeStruct((n, SB_W // HEAD_DIM, HEAD_DIM), F32)
    b16o = jax.ShapeDtypeStruct((n, SB_W), BF16)
    st = pl.BlockSpec((tm, SB_W // HEAD_DIM, HEAD_DIM), lambda i: (i, 0, 0))
    w16 = row(SB_W)
    return pl.pallas_call(
        _proj_ab_kernel,
        grid=(n // tm,),
        in_specs=[row(d), _resident((1, d)), _resident(w.shape), _resident(wf.shape), _resident((1, N_FOX_HEADS))],
        out_specs=[w16, st, st, w16, st, st, w16, w16, w16, w16, row(N_FOX_HEADS)],
        out_shape=[b16o, f32o, f32o, b16o, f32o, f32o, b16o, b16o, b16o, b16o,
                   jax.ShapeDtypeStruct((n, N_FOX_HEADS), F32)],
        compiler_params=_params(("parallel",)),
        name="proj_ab",
    )(x, g, w, wf, bf)


def _cumsum_kernel(x_ref, o_ref):
    t = x_ref.shape[-1]
    r = lax.broadcasted_iota(jnp.int32, (LANES, LANES), 0)
    c = lax.broadcasted_iota(jnp.int32, (LANES, LANES), 1)
    tri = (r <= c).astype(BF16)
    carry = jnp.zeros((x_ref.shape[1], 1), F32)
    for i in range(t // LANES):
        hi, mid, lo = _split3(x_ref[0, :, i * LANES:(i + 1) * LANES])
        cs = _dot(hi, tri) + _dot(mid, tri) + _dot(lo, tri) + carry
        o_ref[0, :, i * LANES:(i + 1) * LANES] = cs
        carry = cs[:, LANES - 1:LANES]


def _cumsum_time(x):
    b, h, t = x.shape
    spec = pl.BlockSpec((1, h, t), lambda i: (i, 0, 0))
    return pl.pallas_call(
        _cumsum_kernel, grid=(b,), in_specs=[spec], out_specs=spec,
        out_shape=jax.ShapeDtypeStruct(x.shape, F32),
        compiler_params=_params(("parallel",)), name="cumsum_time",
    )(x)


def _stacked_heads(q):
    lane = lax.broadcasted_iota(jnp.int32, q.shape, 1)
    zero = jnp.zeros_like(q)
    return jnp.concatenate([jnp.where(lane < HEAD_DIM, q, zero), jnp.where(lane >= HEAD_DIM, q, zero)], axis=0)


def _unstack_heads(acc):
    tq = acc.shape[0] // 2
    lane = lax.broadcasted_iota(jnp.int32, (tq, LANES), 1)
    return jnp.where(lane < HEAD_DIM, acc[:tq], acc[tq:])


def _causal_mask(tq, td, first_row, strict):
    row = first_row + lax.broadcasted_iota(jnp.int32, (2 * tq, td), 0) % tq
    col = lax.broadcasted_iota(jnp.int32, (2 * tq, td), 1)
    return col < row if strict else col <= row


def _strict_upper(n):
    j = lax.broadcasted_iota(jnp.int32, (n, n), 0)
    s = lax.broadcasted_iota(jnp.int32, (n, n), 1)
    return (j > s).astype(BF16)


def _sb_block(qk, v, total, acc_ref, u, vis):
    sub = u.shape[0]
    z = qk * LOG2_E
    neg_abs = lax.bitcast_convert_type(lax.bitcast_convert_type(z, jnp.uint32) | jnp.uint32(0x80000000), F32)
    ls = jnp.minimum(z, 0.0) - jnp.log2(1.0 + jnp.exp2(neg_abs))
    lk = ls - z
    if vis is not None:
        lk = jnp.where(vis, lk, 0.0)
    lk = lk.astype(BF16)
    for s in reversed(range(qk.shape[1] // sub)):
        cols = slice(s * sub, (s + 1) * sub)
        after = _dot(lk[:, cols], u)
        w = jnp.exp2(ls[:, cols] + after + total)
        if vis is not None:
            w = jnp.where(vis[:, cols], w, 0.0)
        acc_ref[...] += _dot(w.astype(BF16), v[cols, :])
        total = total + after[:, 0:1] + lk[:, s * sub:s * sub + 1].astype(F32)
    return total


def _sb_kernel(q_ref, kd_ref, vd_ref, kp_ref, vp_ref, o_ref, acc_ref, *, tk, base, per_q):
    tq, td = q_ref.shape[1], kd_ref.shape[1]
    qs = _stacked_heads(q_ref[0])
    first_row = (pl.program_id(2) % (td // tq)) * tq
    vis = _causal_mask(tq, td, first_row, strict=True)
    u_diag = _strict_upper(min(CUMSUM_SUB, td))
    u_past = _strict_upper(CUMSUM_SUB)
    acc_ref[...] = jnp.zeros_like(acc_ref)
    total = _sb_block(_dot_nt(qs, kd_ref[0]), vd_ref[0], jnp.zeros((2 * tq, 1), F32), acc_ref, u_diag, vis)

    n_past = base + pl.program_id(2) // (td // tq) * per_q

    def body(jj, total):
        start = pl.multiple_of((n_past - 1 - jj) * tk, tk)
        k = kp_ref[0, pl.ds(start, tk), :].astype(BF16)
        v = vp_ref[0, pl.ds(start, tk), :].astype(BF16)
        return _sb_block(_dot_nt(qs, k), v, total, acc_ref, u_past, None)

    lax.fori_loop(0, n_past, body, total)
    o_ref[0] = _unstack_heads(acc_ref[...]).astype(o_ref.dtype)


def _fox_block(qk, v, ck, m, l, acc_ref, vis):
    tq = qk.shape[0] // 2
    s = jnp.concatenate([qk[:tq] - ck[0:1, :], qk[tq:] - ck[1:2, :]], axis=0)
    if vis is not None:
        s = jnp.where(vis, s, NEG_INF)
    m_new = jnp.maximum(m, jnp.max(s, axis=1, keepdims=True))
    alpha = jnp.exp(m - m_new)
    p = jnp.exp(s - m_new)
    acc_ref[...] = alpha * acc_ref[...] + _dot(p.astype(BF16), v)
    return m_new, alpha * l + jnp.sum(p, axis=1, keepdims=True)


def _fox_kernel(q_ref, kd_ref, vd_ref, kp_ref, vp_ref, ckd_ref, ckp_ref, o_ref, acc_ref, *, tk, base, per_q):
    tq, td = q_ref.shape[1], kd_ref.shape[1]
    qs = _stacked_heads(q_ref[0])
    first_row = (pl.program_id(2) % (td // tq)) * tq
    vis = _causal_mask(tq, td, first_row, strict=False)
    acc_ref[...] = jnp.zeros_like(acc_ref)
    m0 = jnp.full((2 * tq, 1), NEG_INF, F32)
    l0 = jnp.zeros((2 * tq, 1), F32)
    stats = _fox_block(_dot_nt(qs, kd_ref[0]), vd_ref[0], ckd_ref[0, 0], m0, l0, acc_ref, vis)

    n_past = base + pl.program_id(2) // (td // tq) * per_q

    def body(j, stats):
        start = pl.multiple_of(j * tk, tk)
        k = kp_ref[0, pl.ds(start, tk), :].astype(BF16)
        v = vp_ref[0, pl.ds(start, tk), :].astype(BF16)
        return _fox_block(_dot_nt(qs, k), v, ckp_ref[0, 0, j], stats[0], stats[1], acc_ref, None)

    _, l = lax.fori_loop(0, n_past, body, stats)
    o_ref[0] = _unstack_heads(acc_ref[...] / l).astype(o_ref.dtype)


def _attn_ab(kernel_fn, q, kd, vd, kp, vp, extra, extra_specs, *, tq, td, tk, base, per_q, name):
    b, t, w = q.shape
    tp = kp.shape[1]
    grid = (b, w // LANES, t // tq)
    qspec = pl.BlockSpec((1, tq, LANES), lambda bi, hp, i: (bi, i, hp))
    dspec = pl.BlockSpec((1, td, LANES), lambda bi, hp, i: (bi, i // (td // tq), hp))
    pspec = pl.BlockSpec((1, tp, LANES), lambda bi, hp, i: (bi, 0, hp))
    return pl.pallas_call(
        functools.partial(kernel_fn, tk=tk, base=base, per_q=per_q),
        grid=grid,
        in_specs=[qspec, dspec, dspec, pspec, pspec] + extra_specs,
        out_specs=qspec,
        out_shape=jax.ShapeDtypeStruct(q.shape, BF16),
        scratch_shapes=[pltpu.VMEM((2 * tq, LANES), F32)],
        compiler_params=_params(("parallel", "parallel", "arbitrary")),
        name=name,
    )(q, kd, vd, kp, vp, *extra)


def _fox_bias_operands(c_q, c_p, tq, td, tk):
    b, t, h = c_q.shape
    tp = c_p.shape[1]
    hp = h // 2
    ckd = c_q.reshape(b, t, hp, 2).transpose(0, 2, 3, 1)
    ckp = c_p.reshape(b, tp // tk, tk, hp, 2).transpose(0, 3, 1, 4, 2)
    specs = [pl.BlockSpec((1, 1, 2, td), lambda bi, p, i: (bi, p, 0, i // (td // tq))),
             pl.BlockSpec((1, 1, tp // tk, 2, tk), lambda bi, p, i: (bi, p, 0, 0, 0))]
    return [ckd, ckp], specs


def _ffn_dense_kernel(x_ref, osb_ref, ofx_ref, wo_ref, g_ref, wg_ref, wu_ref, wd_ref, o_ref, *, tf):
    o_ref[...] = x_ref[...] + _dot(osb_ref[...], wo_ref[:SB_W, :]) + _dot(ofx_ref[...], wo_ref[SB_W:, :])
    hn = _rms(o_ref[...], g_ref[...]).astype(BF16)
    for c in range(wg_ref.shape[1] // tf):
        a = _dot(hn, wg_ref[:, c * tf:(c + 1) * tf])
        u = _dot(hn, wu_ref[:, c * tf:(c + 1) * tf])
        o_ref[...] += _dot((_silu(a) * u).astype(BF16), wd_ref[c * tf:(c + 1) * tf, :])


def _ffn_dense(x, osb, ofx, wo, g, wg, wu, wd):
    n, d = x.shape
    tm = min(ROW_TILE, n)
    row = lambda c: pl.BlockSpec((tm, c), lambda i: (i, 0))
    return pl.pallas_call(
        functools.partial(_ffn_dense_kernel, tf=256),
        grid=(n // tm,),
        in_specs=[row(d), row(SB_W), row(FOX_W), _resident(wo.shape), _resident((1, d)),
                  _resident(wg.shape), _resident(wu.shape), _resident(wd.shape)],
        out_specs=row(d),
        out_shape=jax.ShapeDtypeStruct((n, d), F32),
        compiler_params=_params(("parallel",)),
        name="ffn_dense",
    )(x, osb, ofx, wo, g, wg, wu, wd)


def _proj_c_kernel(x_ref, g_ref, w_ref, q_ref, k_ref, v_ref, k16_ref, v16_ref):
    hn = _rms(x_ref[...], g_ref[...]).astype(BF16)
    q_ref[...] = (_dot(hn, w_ref[:, :SWA_Q_W]) * ATTN_SCALE).astype(BF16)
    k = _dot(hn, w_ref[:, SWA_Q_W:SWA_Q_W + SWA_KV_W])
    v = _dot(hn, w_ref[:, SWA_Q_W + SWA_KV_W:])
    k_ref[...] = k
    v_ref[...] = v
    k16_ref[...] = k.astype(BF16)
    v16_ref[...] = v.astype(BF16)


def _proj_c(x, g, w):
    n, d = x.shape
    tm = min(ROW_TILE, n)
    row = lambda c: pl.BlockSpec((tm, c), lambda i: (i, 0))
    kv32 = jax.ShapeDtypeStruct((n, SWA_KV_W), F32)
    kv16 = jax.ShapeDtypeStruct((n, SWA_KV_W), BF16)
    return pl.pallas_call(
        _proj_c_kernel,
        grid=(n // tm,),
        in_specs=[row(d), _resident((1, d)), _resident(w.shape)],
        out_specs=[row(SWA_Q_W)] + [row(SWA_KV_W)] * 4,
        out_shape=[jax.ShapeDtypeStruct((n, SWA_Q_W), BF16), kv32, kv32, kv16, kv16],
        compiler_params=_params(("parallel",)),
        name="proj_c",
    )(x, g, w)


def _swap_halves(a):
    return jnp.concatenate([a[:, HEAD_DIM:], a[:, :HEAD_DIM]], axis=1)


def _swa_kernel(sink_ref, q_ref, kprev_ref, vprev_ref, kcur_ref, vcur_ref, o_ref, *, pos0):
    tq = q_ref.shape[1]
    band = (WINDOW_CHUNKS + 1) * CHUNK
    stack = SWA_GROUP // 2
    rows = stack * CHUNK
    first = pos0 + pl.program_id(1) * tq
    kw = jnp.concatenate([kprev_ref[0].astype(BF16), kcur_ref[0].astype(BF16)], axis=0)
    vw = jnp.concatenate([vprev_ref[0].astype(BF16), vcur_ref[0].astype(BF16)], axis=0)
    k_by_half = (kw, _swap_halves(kw))
    v_by_half = (vw, _swap_halves(vw))
    q = q_ref[0]
    in_low = lax.broadcasted_iota(jnp.int32, q.shape, 1) % LANES < HEAD_DIM
    q_by_half = (jnp.where(in_low, q, jnp.zeros_like(q)), jnp.where(in_low, jnp.zeros_like(q), q))
    r_id = lax.broadcasted_iota(jnp.int32, (rows, band), 0)
    c_id = lax.broadcasted_iota(jnp.int32, (rows, band), 1)
    dist = jnp.abs(WINDOW_CHUNKS * CHUNK + r_id % CHUNK - c_id).astype(F32)
    head_in_stack = lax.broadcasted_iota(jnp.int32, (rows, 1), 0) // CHUNK
    col = lax.broadcasted_iota(jnp.int32, (1, band), 1)
    lane = lax.broadcasted_iota(jnp.int32, (CHUNK, LANES), 1)

    def per_head(values):
        out = jnp.full((rows, 1), values[-1], F32)
        for i in range(stack - 2, -1, -1):
            out = jnp.where(head_in_stack == i, values[i], out)
        return out

    heads = {(g, half): [g * SWA_GROUP + 2 * i + half for i in range(stack)]
             for g in range(N_SWA_KV_HEADS) for half in range(2)}
    bias = {key: per_head([2.0 ** (-8.0 * (h + 1) / N_SWA_HEADS) for h in hs]) * dist for key, hs in heads.items()}
    sink = {key: per_head([sink_ref[h] for h in hs]) for key, hs in heads.items()}

    for c in range(tq // CHUNK):
        in_range = first - WINDOW_CHUNKS * CHUNK + c * CHUNK + col >= 0
        win = slice(c * CHUNK, c * CHUNK + band)
        outs = {}
        for (g, half), hs in heads.items():
            qs = jnp.concatenate([q_by_half[half][c * CHUNK:(c + 1) * CHUNK, (h // 2) * LANES:(h // 2 + 1) * LANES]
                                  for h in hs], axis=0)
            s = _dot_nt(qs, k_by_half[g ^ half][win]) - bias[g, half]
            s = jnp.where(in_range, s, NEG_INF)
            m = jnp.maximum(jnp.max(s, axis=1, keepdims=True), sink[g, half])
            p = jnp.exp(s - m)
            l = jnp.sum(p, axis=1, keepdims=True) + jnp.exp(sink[g, half] - m)
            outs[g, half] = _dot(p.astype(BF16), v_by_half[g ^ half][win]) / l
        for g in range(N_SWA_KV_HEADS):
            for i in range(stack):
                pair = (g * SWA_GROUP + 2 * i) // 2
                o_ref[0, c * CHUNK:(c + 1) * CHUNK, pair * LANES:(pair + 1) * LANES] = jnp.where(
                    lane < HEAD_DIM, outs[g, 0][i * CHUNK:(i + 1) * CHUNK],
                    outs[g, 1][i * CHUNK:(i + 1) * CHUNK]).astype(o_ref.dtype)


def _swa(sinks, q, kprev, vprev, kcur, vcur, *, tq, pos0, prev_is_cache):
    b, t, _ = q.shape
    per = tq // SWA_CACHE
    prev_map = ((lambda bi, i, s: (bi, 0, 0)) if prev_is_cache
                else (lambda bi, i, s: (bi, jnp.maximum(i * per - 1, 0), 0)))
    cur = lambda w: pl.BlockSpec((1, tq, w), lambda bi, i, s: (bi, i, 0))
    prev = pl.BlockSpec((1, SWA_CACHE, SWA_KV_W), prev_map)
    return pl.pallas_call(
        functools.partial(_swa_kernel, pos0=pos0),
        grid_spec=pltpu.PrefetchScalarGridSpec(
            num_scalar_prefetch=1, grid=(b, t // tq),
            in_specs=[cur(SWA_Q_W), prev, prev, cur(SWA_KV_W), cur(SWA_KV_W)],
            out_specs=cur(SWA_Q_W)),
        out_shape=jax.ShapeDtypeStruct(q.shape, BF16),
        compiler_params=_params(("parallel", "arbitrary")),
        name="swa",
    )(sinks, q, kprev, vprev, kcur, vcur)


def _route_kernel(x_ref, o_ref, wo_ref, g_ref, wr_ref, x2_ref, xn_ref, route_ref):
    x2 = x_ref[...] + _dot(o_ref[...], wo_ref[...])
    xn = _rms(x2, g_ref[...])
    x2_ref[...] = x2
    _store_chunked(xn_ref, xn)
    hi = xn.astype(BF16)
    mid = (xn - hi.astype(F32)).astype(BF16)
    d = xn.shape[1]
    logits = _dot(hi, wr_ref[:d, :]) + _dot(hi, wr_ref[d:, :]) + _dot(mid, wr_ref[:d, :])
    lane = lax.broadcasted_iota(jnp.int32, logits.shape, 1).astype(F32)
    lg = jnp.where(lane < N_EXPERTS, logits, -jnp.inf)
    m1 = jnp.max(lg, axis=1, keepdims=True)
    i1 = jnp.min(jnp.where(lg == m1, lane, float(LANES)), axis=1, keepdims=True)
    lg2 = jnp.where(lane == i1, -jnp.inf, lg)
    m2 = jnp.max(lg2, axis=1, keepdims=True)
    i2 = jnp.min(jnp.where(lg2 == m2, lane, float(LANES)), axis=1, keepdims=True)
    e = jnp.exp(m2 - m1)
    g1 = 1.0 / (1.0 + e)
    g2 = e / (1.0 + e)
    route = jnp.where(lane == 0, i1, jnp.where(lane == 1, i2, jnp.where(lane == 2, g1, jnp.where(lane == 3, g2, 0.0))))
    route_ref[...] = route


def _route(x, o, wo, g, wr):
    n, d = x.shape
    tm = min(ROW_TILE, n)
    row = lambda c: pl.BlockSpec((tm, c), lambda i: (i, 0))
    xs = jax.ShapeDtypeStruct((n, d), F32)
    return pl.pallas_call(
        _route_kernel,
        grid=(n // tm,),
        in_specs=[row(d), row(SWA_Q_W), _resident(wo.shape), _resident((1, d)), _resident(wr.shape)],
        out_specs=[row(d), _chunked_spec(tm, d), row(LANES)],
        out_shape=[xs, _chunked_shape(n, d), jax.ShapeDtypeStruct((n, LANES), F32)],
        compiler_params=_params(("parallel",)),
        name="route",
    )(x, o, wo, g, wr)


def _chunked_shape(rows, d):
    return jax.ShapeDtypeStruct((d // GATHER_CHUNK, rows, GATHER_CHUNK), F32)


def _chunked_spec(tm, d, index_map=lambda i: (0, i, 0)):
    return pl.BlockSpec((d // GATHER_CHUNK, tm, GATHER_CHUNK), index_map)


def _store_chunked(ref, x):
    for j in range(x.shape[1] // GATHER_CHUNK):
        ref[j] = x[:, j * GATHER_CHUNK:(j + 1) * GATHER_CHUNK]


def _load_chunked(ref):
    return jnp.concatenate([ref[j] for j in range(ref.shape[0])], axis=1)


def _sc_gather_flat(flat, idx):
    m, w = idx.shape[1], flat.shape[1]
    mesh = plsc.VectorSubcoreMesh(core_axis_name="core", subcore_axis_name="subcore")

    @pl.kernel(out_type=jax.ShapeDtypeStruct((m, w), flat.dtype), mesh=mesh)
    def gather(x_hbm, i_hbm, o_hbm):
        def body(i_vmem, o_vmem):
            pltpu.sync_copy(x_hbm.at[i_vmem.at[0]], o_vmem)

        pltpu.emit_pipeline(
            body,
            grid=(m // SC_WINDOW,),
            in_specs=[pl.BlockSpec((1, SC_WINDOW), index_map=lambda i: (0, i))],
            out_specs=[pl.BlockSpec((SC_WINDOW, w), index_map=lambda i: (i, 0))],
            core_axis_name=("core", "subcore"),
            dimension_semantics=(pltpu.PARALLEL,),
        )(i_hbm, o_hbm)

    return gather(flat, idx)


def _gather_indices(rows, n_chunks, src_rows):
    return jnp.concatenate([rows + j * src_rows for j in range(n_chunks)]).reshape(1, -1)


def _sc_gather(src, idx):
    n_chunks, _, w = src.shape
    out = _sc_gather_flat(src.reshape(-1, w), idx)
    return out.reshape(n_chunks, idx.shape[1] // n_chunks, w)


def _expert_kernel(be_ref, nv_ref, x_ref, wg_ref, wu_ref, wd_ref, o_ref, xb_ref, acc_ref):
    blk, f = pl.program_id(0), pl.program_id(1)
    last = pl.num_programs(1) - 1
    valid = blk < nv_ref[0]

    @pl.when(valid & (f == 0))
    def _():
        for j in range(x_ref.shape[0]):
            xb_ref[:, j * GATHER_CHUNK:(j + 1) * GATHER_CHUNK] = x_ref[j].astype(BF16)
        acc_ref[...] = jnp.zeros_like(acc_ref)

    @pl.when(valid)
    def _():
        xb = xb_ref[...]
        h = (_silu(_dot(xb, wg_ref[0].astype(BF16))) * _dot(xb, wu_ref[0].astype(BF16))).astype(BF16)
        acc_ref[...] += _dot(h, wd_ref[0].astype(BF16))

    @pl.when(valid & (f == last))
    def _():
        _store_chunked(o_ref, acc_ref[...])

    @pl.when(jnp.logical_not(valid) & (f == last))
    def _():
        o_ref[...] = jnp.zeros_like(o_ref)


def _experts(block_expert, n_valid, x_slots, wg, wu, wd, *, tm, tf):
    d = wg.shape[1]
    cap = x_slots.shape[1]
    nf = wg.shape[2] // tf
    fsel = lambda b, f, nv: jnp.where(b < nv[0], f, nf - 1)
    slots = _chunked_spec(tm, d, lambda b, f, be, nv: (0, b, 0))
    return pl.pallas_call(
        _expert_kernel,
        grid_spec=pltpu.PrefetchScalarGridSpec(
            num_scalar_prefetch=2, grid=(cap // tm, nf),
            in_specs=[slots,
                      pl.BlockSpec((1, d, tf), lambda b, f, be, nv: (be[b], 0, fsel(b, f, nv))),
                      pl.BlockSpec((1, d, tf), lambda b, f, be, nv: (be[b], 0, fsel(b, f, nv))),
                      pl.BlockSpec((1, tf, d), lambda b, f, be, nv: (be[b], fsel(b, f, nv), 0))],
            out_specs=slots,
            scratch_shapes=[pltpu.VMEM((tm, d), BF16), pltpu.VMEM((tm, d), F32)]),
        out_shape=_chunked_shape(cap, d),
        compiler_params=_params(("arbitrary", "arbitrary")),
        name="moe_experts",
    )(block_expert, n_valid, x_slots, wg, wu, wd)


def _combine_kernel(y1_ref, y2_ref, x_ref, route_ref, g_ref, o_ref):
    route = route_ref[...]
    y = x_ref[...] + (_load_chunked(y1_ref) * route[:, 2:3] + _load_chunked(y2_ref) * route[:, 3:4])
    o_ref[...] = _rms(y, g_ref[...])


def _combine(y_pairs, x2, route, g):
    n, d = x2.shape
    tm = min(ROW_TILE, n)
    row = lambda c: pl.BlockSpec((tm, c), lambda i: (i, 0))
    second = n // tm
    return pl.pallas_call(
        _combine_kernel,
        grid=(n // tm,),
        in_specs=[_chunked_spec(tm, d), _chunked_spec(tm, d, lambda i: (0, i + second, 0)),
                  row(d), row(LANES), _resident((1, d))],
        out_specs=row(d),
        out_shape=jax.ShapeDtypeStruct((n, d), F32),
        compiler_params=_params(("parallel",)),
        name="moe_combine",
    )(y_pairs, y_pairs, x2, route, g)


def _dispatch(route, tm):
    n = route.shape[0]
    expert = route[:, :2].astype(jnp.int32).reshape(-1)
    onehot = (expert[:, None] == jnp.arange(N_EXPERTS, dtype=jnp.int32)[None, :]).astype(jnp.int32)
    csum = jnp.cumsum(onehot, axis=0)
    rank = jnp.sum(onehot * csum, axis=1) - 1
    counts = csum[-1]
    padded = (counts + tm - 1) // tm * tm
    pend = jnp.cumsum(padded)
    pstart = pend - padded
    dest = (jnp.sum(onehot * pstart[None, :], axis=1) + rank).astype(jnp.int32)
    step = max(tm, SLOT_ALIGN)
    cap = -(-(2 * n + N_EXPERTS * (tm - 1)) // step) * step
    n_blocks = cap // tm
    token = jnp.arange(2 * n, dtype=jnp.int32) // 2
    slot_token = jnp.zeros((n_blocks * tm,), jnp.int32).at[dest].set(token, unique_indices=True)
    block_expert = jnp.minimum(
        jnp.searchsorted(pend, jnp.arange(n_blocks, dtype=jnp.int32) * tm, side="right"), N_EXPERTS - 1
    ).astype(jnp.int32)
    n_valid = (pend[-1:] // tm).astype(jnp.int32)
    return slot_token, dest, block_expert, n_valid


def _run(x, caches, wts, *, tq_ab, tq_c, expert_rows):
    b, t, d = x.shape
    n = b * t
    xf = x.reshape(n, d)
    (qsb, ksb, vsb, qfx, kfx, vfx, ksb16, vsb16, kfx16, vfx16, logf) = _proj_ab(
        xf, wts["g_ab"], wts["w_in_ab"], wts["w_f"], wts["b_f"])
    r3 = lambda a: a.reshape(b, t, -1)
    logf3 = r3(logf)
    if caches is None:
        c = _cumsum_time(logf3.transpose(0, 2, 1)).transpose(0, 2, 1)
        c_q, c_p = c, c
        kp_sb, vp_sb, kp_fx, vp_fx = r3(ksb16), r3(vsb16), r3(kfx16), r3(vfx16)
        td = tk = min(ATTN_TILE, t)
        base, per_q = 0, 1
    else:
        p_sb_k, p_sb_v, p_fx_k, p_fx_v, p_logf = caches[:5]
        past = p_logf.shape[1]
        pad = (-(past + t)) % LANES
        allf = jnp.concatenate([p_logf.astype(F32), logf3, jnp.zeros((b, pad, N_FOX_HEADS), F32)], axis=1)
        c = _cumsum_time(allf.transpose(0, 2, 1)).transpose(0, 2, 1)
        c_q, c_p = c[:, past:past + t], c[:, :past]
        kp_sb, vp_sb, kp_fx, vp_fx = (a.reshape(b, past, -1) for a in (p_sb_k, p_sb_v, p_fx_k, p_fx_v))
        td, tk = t, min(SAMPLE_ATTN_TILE, past)
        base, per_q = past // tk, 0
    kw = dict(tq=tq_ab, td=td, tk=tk, base=base, per_q=per_q)
    o_sb = _attn_ab(_sb_kernel, r3(qsb), r3(ksb16), r3(vsb16), kp_sb, vp_sb, [], [], name="attn_sb", **kw)
    extra, extra_specs = _fox_bias_operands(c_q, c_p, tq_ab, td, tk)
    o_fx = _attn_ab(_fox_kernel, r3(qfx), r3(kfx16), r3(vfx16), kp_fx, vp_fx, extra, extra_specs,
                    name="attn_fox", **kw)
    x1 = _ffn_dense(xf, o_sb.reshape(n, SB_W), o_fx.reshape(n, FOX_W), wts["w_out_ab"], wts["g_ffn"],
                    wts["w_gate"], wts["w_up"], wts["w_down"])

    qc, kc, vc, kc16, vc16 = _proj_c(x1, wts["g_c"], wts["w_in_c"])
    if caches is None:
        kprev, vprev, pos0 = r3(kc16), r3(vc16), 0
        new_k, new_v = r3(kc)[:, t - SWA_CACHE:], r3(vc)[:, t - SWA_CACHE:]
    else:
        p_k, p_v = (a.reshape(b, SWA_CACHE, SWA_KV_W) for a in caches[5:])
        kprev, vprev, pos0 = p_k, p_v, caches[0].shape[1]
        new_k = jnp.concatenate([p_k, r3(kc)], axis=1)[:, t:]
        new_v = jnp.concatenate([p_v, r3(vc)], axis=1)[:, t:]
    o_c = _swa(wts["sinks"], r3(qc), kprev, vprev, r3(kc16), r3(vc16), tq=tq_c, pos0=pos0,
               prev_is_cache=caches is not None)

    x2, xn, route = _route(x1, o_c.reshape(n, SWA_Q_W), wts["w_out_c"], wts["g_moe"], wts["w_router"])
    slot_token, dest, block_expert, n_valid = _dispatch(route, expert_rows)
    heads = lambda a, h: a.reshape(1, b, -1, h, HEAD_DIM)
    states = (heads(ksb, N_SB_HEADS), heads(vsb, N_SB_HEADS), heads(kfx, N_FOX_HEADS), heads(vfx, N_FOX_HEADS),
              logf3[None], heads(new_k, N_SWA_KV_HEADS), heads(new_v, N_SWA_KV_HEADS))
    n_chunks = d // GATHER_CHUNK
    pair_rows = dest.reshape(n, 2).T.reshape(-1)
    return dict(xn=xn, slot_idx=_gather_indices(slot_token, n_chunks, n),
                pair_idx=_gather_indices(pair_rows, n_chunks, slot_token.shape[0]),
                block_expert=block_expert, n_valid=n_valid, x2=x2, route=route, states=states, shape=(b, t, d),
                expert_rows=expert_rows)


def _moe_experts(ctx, wts):
    x_slots = _sc_gather(ctx["xn"], ctx["slot_idx"])
    y_slots = _experts(ctx["block_expert"], ctx["n_valid"], x_slots, wts["w_gate_moe"], wts["w_up_moe"],
                       wts["w_down_moe"], tm=ctx["expert_rows"],
                       tf=EXPERT_COLS if ctx["expert_rows"] == EXPERT_ROWS else SAMPLE_EXPERT_COLS)
    return dict(ctx, y_slots=y_slots)


def _moe_combine(ctx, wts):
    y_pairs = _sc_gather(ctx["y_slots"], ctx["pair_idx"])
    y = _combine(y_pairs, ctx["x2"], ctx["route"], wts["g_final"])
    return y.reshape(ctx["shape"]), ctx["states"]


def _issue_after(first, later):
    return lax.optimization_barrier((first, later))


def kernel(x_prompt, x_sample, cache_sb_k, cache_sb_v, cache_fox_k, cache_fox_v, cache_fox_logf, cache_swa_k,
           cache_swa_v, norm_mix_ab, w_in_ab, b_forget, w_out_ab, norm_ffn_dense, w_gate_dense, w_up_dense,
           w_down_dense, norm_mix_c, w_in_c, sinks, w_out_c, norm_ffn_moe, w_router, w_gate_moe, w_up_moe,
           w_down_moe, norm_final):
    d = x_prompt.shape[-1]
    main_w = 3 * SB_W + 3 * FOX_W
    w_f = jnp.zeros((d, LANES), F32).at[:, :N_FOX_HEADS].set(w_in_ab[0][:, main_w:])
    wr = jnp.zeros((d, LANES), F32).at[:, :N_EXPERTS].set(w_router[0])
    wr_hi = wr.astype(BF16)
    wr_mid = (wr - wr_hi.astype(F32)).astype(BF16)
    wts = dict(
        g_ab=norm_mix_ab[0][None], w_in_ab=w_in_ab[0][:, :main_w].astype(BF16), w_f=w_f.astype(BF16),
        b_f=b_forget[0][None], w_out_ab=w_out_ab[0].astype(BF16), g_ffn=norm_ffn_dense[0][None],
        w_gate=w_gate_dense[0].astype(BF16), w_up=w_up_dense[0].astype(BF16), w_down=w_down_dense[0].astype(BF16),
        g_c=norm_mix_c[0][None], w_in_c=w_in_c[0].astype(BF16), sinks=sinks[0], w_out_c=w_out_c[0].astype(BF16),
        g_moe=norm_ffn_moe[0][None], w_router=jnp.concatenate([wr_hi, wr_mid], axis=0),
        w_gate_moe=w_gate_moe[0], w_up_moe=w_up_moe[0], w_down_moe=w_down_moe[0], g_final=norm_final[None])
    t_p, t_s = x_prompt.shape[1], x_sample.shape[1]
    caches = (cache_sb_k[0], cache_sb_v[0], cache_fox_k[0], cache_fox_v[0], cache_fox_logf[0],
              cache_swa_k[0], cache_swa_v[0])
    prompt = _run(x_prompt, None, wts, tq_ab=min(ATTN_ROWS, t_p), tq_c=min(SWA_TILE, t_p), expert_rows=EXPERT_ROWS)
    prompt["slot_idx"], (x_sample, caches) = _issue_after(prompt["slot_idx"], (x_sample, caches))
    sample = _run(x_sample, caches, wts, tq_ab=t_s, tq_c=t_s, expert_rows=SAMPLE_EXPERT_ROWS)
    prompt = _moe_experts(prompt, wts)
    prompt["y_slots"], sample["xn"] = _issue_after(prompt["y_slots"], sample["xn"])
    sample = _moe_experts(sample, wts)
    y_p, st_p = _moe_combine(prompt, wts)
    y_s, st_s = _moe_combine(sample, wts)
    return (y_p, y_s) + st_p + st_s
```

```python
import functools

import jax
import jax.numpy as jnp
from jax import lax
from jax.experimental import pallas as pl
from jax.experimental.pallas import tpu as pltpu
from jax.experimental.pallas import tpu_sc as plsc

F32 = jnp.float32
BF16 = jnp.bfloat16

HEAD_DIM = 64
LANES = 128
N_SB_HEADS = 8
N_FOX_HEADS = 8
N_SWA_HEADS = 16
N_SWA_KV_HEADS = 2
SWA_GROUP = N_SWA_HEADS // N_SWA_KV_HEADS
CHUNK = 64
WINDOW_CHUNKS = 2
SWA_CACHE = 128
N_EXPERTS = 8
RMS_EPS = 1e-6
NEG_INF = -1e30
ATTN_SCALE = HEAD_DIM ** -0.5
SB_W = N_SB_HEADS * HEAD_DIM
FOX_W = N_FOX_HEADS * HEAD_DIM
SWA_Q_W = N_SWA_HEADS * HEAD_DIM
SWA_KV_W = N_SWA_KV_HEADS * HEAD_DIM

VMEM_LIMIT_BYTES = 56 * 1024 * 1024
ROW_TILE = 512
ATTN_TILE = 512
ATTN_ROWS = 512
SAMPLE_ATTN_TILE = 2048
CUMSUM_SUB = 256
SWA_TILE = 256
SAMPLE_EXPERT_ROWS = 256
LOG2_E = 1.4426950408889634
EXPERT_ROWS = 1024
EXPERT_COLS = 512
SAMPLE_EXPERT_COLS = 896
GATHER_CHUNK = 256
SC_WINDOW = 128
SLOT_ALIGN = 1024
NT_DIMS = (((1,), (1,)), ((), ()))


def _params(sem):
    return pltpu.CompilerParams(dimension_semantics=sem, vmem_limit_bytes=VMEM_LIMIT_BYTES)


def _resident(shape):
    n = len(shape)
    return pl.BlockSpec(shape, lambda *_: (0,) * n, pipeline_mode=pl.Buffered(1))


def _rms(x, g):
    return x * lax.rsqrt(jnp.mean(x * x, axis=-1, keepdims=True) + RMS_EPS) * g


def _log_sigmoid(z):
    return jnp.minimum(z, 0.0) - jnp.log(1.0 + jnp.exp(-jnp.abs(z)))


def _silu(a):
    return a * (1.0 / (1.0 + jnp.exp(-a)))


def _dot(a, b):
    return jnp.dot(a, b, preferred_element_type=F32)


def _dot_nt(a, b):
    return lax.dot_general(a, b, NT_DIMS, preferred_element_type=F32)


def _split3(x):
    hi = x.astype(BF16)
    r = x - hi.astype(F32)
    mid = r.astype(BF16)
    lo = (r - mid.astype(F32)).astype(BF16)
    return hi, mid, lo


def _proj_ab_kernel(x_ref, g_ref, w_ref, wf_ref, bf_ref,
                    qsb_ref, ksb_ref, vsb_ref, qfx_ref, kfx_ref, vfx_ref,
                    ksb16_ref, vsb16_ref, kfx16_ref, vfx16_ref, logf_ref):
    hn = _rms(x_ref[...], g_ref[...]).astype(BF16)
    f32_outs = (None, ksb_ref, vsb_ref, None, kfx_ref, vfx_ref)
    b16_outs = (qsb_ref, ksb16_ref, vsb16_ref, qfx_ref, kfx16_ref, vfx16_ref)
    for c in range(6):
        p = _dot(hn, w_ref[:, c * SB_W:(c + 1) * SB_W])
        if f32_outs[c] is None:
            b16_outs[c][...] = (p * ATTN_SCALE).astype(BF16)
        else:
            f32_outs[c][...] = p.reshape(p.shape[0], SB_W // HEAD_DIM, HEAD_DIM)
            b16_outs[c][...] = p.astype(BF16)
    f = _dot(hn, wf_ref[...])[:, :N_FOX_HEADS] + bf_ref[...]
    logf_ref[...] = _log_sigmoid(f)


def _proj_ab(x, g, w, wf, bf):
    n, d = x.shape
    tm = min(ROW_TILE, n)
    row = lambda c: pl.BlockSpec((tm, c), lambda i: (i, 0))
    f32o = jax.ShapeDtypeStruct((n, SB_W // HEAD_DIM, HEAD_DIM), F32)
    b16o = jax.ShapeDtypeStruct((n, SB_W), BF16)
    st = pl.BlockSpec((tm, SB_W // HEAD_DIM, HEAD_DIM), lambda i: (i, 0, 0))
    w16 = row(SB_W)
    return pl.pallas_call(
        _proj_ab_kernel,
        grid=(n // tm,),
        in_specs=[row(d), _resident((1, d)), _resident(w.shape), _resident(wf.shape), _resident((1, N_FOX_HEADS))],
        out_specs=[w16, st, st, w16, st, st, w16, w16, w16, w16, row(N_FOX_HEADS)],
        out_shape=[b16o, f32o, f32o, b16o, f32o, f32o, b16o, b16o, b16o, b16o,
                   jax.ShapeDtypeStruct((n, N_FOX_HEADS), F32)],
        compiler_params=_params(("parallel",)),
        name="proj_ab",
    )(x, g, w, wf, bf)


def _cumsum_kernel(x_ref, o_ref):
    t = x_ref.shape[-1]
    r = lax.broadcasted_iota(jnp.int32, (LANES, LANES), 0)
    c = lax.broadcasted_iota(jnp.int32, (LANES, LANES), 1)
    tri = (r <= c).astype(BF16)
    carry = jnp.zeros((x_ref.shape[1], 1), F32)
    for i in range(t // LANES):
        hi, mid, lo = _split3(x_ref[0, :, i * LANES:(i + 1) * LANES])
        cs = _dot(hi, tri) + _dot(mid, tri) + _dot(lo, tri) + carry
        o_ref[0, :, i * LANES:(i + 1) * LANES] = cs
        carry = cs[:, LANES - 1:LANES]


def _cumsum_time(x):
    b, h, t = x.shape
    spec = pl.BlockSpec((1, h, t), lambda i: (i, 0, 0))
    return pl.pallas_call(
        _cumsum_kernel, grid=(b,), in_specs=[spec], out_specs=spec,
        out_shape=jax.ShapeDtypeStruct(x.shape, F32),
        compiler_params=_params(("parallel",)), name="cumsum_time",
    )(x)


def _stacked_heads(q):
    lane = lax.broadcasted_iota(jnp.int32, q.shape, 1)
    zero = jnp.zeros_like(q)
    return jnp.concatenate([jnp.where(lane < HEAD_DIM, q, zero), jnp.where(lane >= HEAD_DIM, q, zero)], axis=0)


def _unstack_heads(acc):
    tq = acc.shape[0] // 2
    lane = lax.broadcasted_iota(jnp.int32, (tq, LANES), 1)
    return jnp.where(lane < HEAD_DIM, acc[:tq], acc[tq:])


def _causal_mask(tq, td, first_row, strict):
    row = first_row + lax.broadcasted_iota(jnp.int32, (2 * tq, td), 0) % tq
    col = lax.broadcasted_iota(jnp.int32, (2 * tq, td), 1)
    return col < row if strict else col <= row


def _strict_upper(n):
    j = lax.broadcasted_iota(jnp.int32, (n, n), 0)
    s = lax.broadcasted_iota(jnp.int32, (n, n), 1)
    return (j > s).astype(BF16)


def _sb_block(qk, v, total, acc_ref, u, vis):
    sub = u.shape[0]
    z = qk * LOG2_E
    neg_abs = lax.bitcast_convert_type(lax.bitcast_convert_type(z, jnp.uint32) | jnp.uint32(0x80000000), F32)
    ls = jnp.minimum(z, 0.0) - jnp.log2(1.0 + jnp.exp2(neg_abs))
    lk = ls - z
    if vis is not None:
        lk = jnp.where(vis, lk, 0.0)
    lk = lk.astype(BF16)
    for s in reversed(range(qk.shape[1] // sub)):
        cols = slice(s * sub, (s + 1) * sub)
        after = _dot(lk[:, cols], u)
        w = jnp.exp2(ls[:, cols] + after + total)
        if vis is not None:
            w = jnp.where(vis[:, cols], w, 0.0)
        acc_ref[...] += _dot(w.astype(BF16), v[cols, :])
        total = total + after[:, 0:1] + lk[:, s * sub:s * sub + 1].astype(F32)
    return total


def _sb_kernel(q_ref, kd_ref, vd_ref, kp_ref, vp_ref, o_ref, acc_ref, *, tk, base, per_q):
    tq, td = q_ref.shape[1], kd_ref.shape[1]
    qs = _stacked_heads(q_ref[0])
    first_row = (pl.program_id(2) % (td // tq)) * tq
    vis = _causal_mask(tq, td, first_row, strict=True)
    u_diag = _strict_upper(min(CUMSUM_SUB, td))
    u_past = _strict_upper(CUMSUM_SUB)
    acc_ref[...] = jnp.zeros_like(acc_ref)
    total = _sb_block(_dot_nt(qs, kd_ref[0]), vd_ref[0], jnp.zeros((2 * tq, 1), F32), acc_ref, u_diag, vis)

    n_past = base + pl.program_id(2) // (td // tq) * per_q

    def body(jj, total):
        start = pl.multiple_of((n_past - 1 - jj) * tk, tk)
        k = kp_ref[0, pl.ds(start, tk), :].astype(BF16)
        v = vp_ref[0, pl.ds(start, tk), :].astype(BF16)
        return _sb_block(_dot_nt(qs, k), v, total, acc_ref, u_past, None)

    lax.fori_loop(0, n_past, body, total)
    o_ref[0] = _unstack_heads(acc_ref[...]).astype(o_ref.dtype)


def _fox_block(qk, v, ck, m, l, acc_ref, vis):
    tq = qk.shape[0] // 2
    s = jnp.concatenate([qk[:tq] - ck[0:1, :], qk[tq:] - ck[1:2, :]], axis=0)
    if vis is not None:
        s = jnp.where(vis, s, NEG_INF)
    m_new = jnp.maximum(m, jnp.max(s, axis=1, keepdims=True))
    alpha = jnp.exp(m - m_new)
    p = jnp.exp(s - m_new)
    acc_ref[...] = alpha * acc_ref[...] + _dot(p.astype(BF16), v)
    return m_new, alpha * l + jnp.sum(p, axis=1, keepdims=True)


def _fox_kernel(q_ref, kd_ref, vd_ref, kp_ref, vp_ref, ckd_ref, ckp_ref, o_ref, acc_ref, *, tk, base, per_q):
    tq, td = q_ref.shape[1], kd_ref.shape[1]
    qs = _stacked_heads(q_ref[0])
    first_row = (pl.program_id(2) % (td // tq)) * tq
    vis = _causal_mask(tq, td, first_row, strict=False)
    acc_ref[...] = jnp.zeros_like(acc_ref)
    m0 = jnp.full((2 * tq, 1), NEG_INF, F32)
    l0 = jnp.zeros((2 * tq, 1), F32)
    stats = _fox_block(_dot_nt(qs, kd_ref[0]), vd_ref[0], ckd_ref[0, 0], m0, l0, acc_ref, vis)

    n_past = base + pl.program_id(2) // (td // tq) * per_q

    def body(j, stats):
        start = pl.multiple_of(j * tk, tk)
        k = kp_ref[0, pl.ds(start, tk), :].astype(BF16)
        v = vp_ref[0, pl.ds(start, tk), :].astype(BF16)
        return _fox_block(_dot_nt(qs, k), v, ckp_ref[0, 0, j], stats[0], stats[1], acc_ref, None)

    _, l = lax.fori_loop(0, n_past, body, stats)
    o_ref[0] = _unstack_heads(acc_ref[...] / l).astype(o_ref.dtype)


def _attn_ab(kernel_fn, q, kd, vd, kp, vp, extra, extra_specs, *, tq, td, tk, base, per_q, name):
    b, t, w = q.shape
    tp = kp.shape[1]
    grid = (b, w // LANES, t // tq)
    qspec = pl.BlockSpec((1, tq, LANES), lambda bi, hp, i: (bi, i, hp))
    dspec = pl.BlockSpec((1, td, LANES), lambda bi, hp, i: (bi, i // (td // tq), hp))
    pspec = pl.BlockSpec((1, tp, LANES), lambda bi, hp, i: (bi, 0, hp))
    return pl.pallas_call(
        functools.partial(kernel_fn, tk=tk, base=base, per_q=per_q),
        grid=grid,
        in_specs=[qspec, dspec, dspec, pspec, pspec] + extra_specs,
        out_specs=qspec,
        out_shape=jax.ShapeDtypeStruct(q.shape, BF16),
        scratch_shapes=[pltpu.VMEM((2 * tq, LANES), F32)],
        compiler_params=_params(("parallel", "parallel", "arbitrary")),
        name=name,
    )(q, kd, vd, kp, vp, *extra)


def _fox_bias_operands(c_q, c_p, tq, td, tk):
    b, t, h = c_q.shape
    tp = c_p.shape[1]
    hp = h // 2
    ckd = c_q.reshape(b, t, hp, 2).transpose(0, 2, 3, 1)
    ckp = c_p.reshape(b, tp // tk, tk, hp, 2).transpose(0, 3, 1, 4, 2)
    specs = [pl.BlockSpec((1, 1, 2, td), lambda bi, p, i: (bi, p, 0, i // (td // tq))),
             pl.BlockSpec((1, 1, tp // tk, 2, tk), lambda bi, p, i: (bi, p, 0, 0, 0))]
    return [ckd, ckp], specs


def _ffn_dense_kernel(x_ref, osb_ref, ofx_ref, wo_ref, g_ref, wg_ref, wu_ref, wd_ref, o_ref, *, tf):
    o_ref[...] = x_ref[...] + _dot(osb_ref[...], wo_ref[:SB_W, :]) + _dot(ofx_ref[...], wo_ref[SB_W:, :])
    hn = _rms(o_ref[...], g_ref[...]).astype(BF16)
    for c in range(wg_ref.shape[1] // tf):
        a = _dot(hn, wg_ref[:, c * tf:(c + 1) * tf])
        u = _dot(hn, wu_ref[:, c * tf:(c + 1) * tf])
        o_ref[...] += _dot((_silu(a) * u).astype(BF16), wd_ref[c * tf:(c + 1) * tf, :])


def _ffn_dense(x, osb, ofx, wo, g, wg, wu, wd):
    n, d = x.shape
    tm = min(ROW_TILE, n)
    row = lambda c: pl.BlockSpec((tm, c), lambda i: (i, 0))
    return pl.pallas_call(
        functools.partial(_ffn_dense_kernel, tf=256),
        grid=(n // tm,),
        in_specs=[row(d), row(SB_W), row(FOX_W), _resident(wo.shape), _resident((1, d)),
                  _resident(wg.shape), _resident(wu.shape), _resident(wd.shape)],
        out_specs=row(d),
        out_shape=jax.ShapeDtypeStruct((n, d), F32),
        compiler_params=_params(("parallel",)),
        name="ffn_dense",
    )(x, osb, ofx, wo, g, wg, wu, wd)


def _proj_c_kernel(x_ref, g_ref, w_ref, q_ref, k_ref, v_ref, k16_ref, v16_ref):
    hn = _rms(x_ref[...], g_ref[...]).astype(BF16)
    q_ref[...] = (_dot(hn, w_ref[:, :SWA_Q_W]) * ATTN_SCALE).astype(BF16)
    k = _dot(hn, w_ref[:, SWA_Q_W:SWA_Q_W + SWA_KV_W])
    v = _dot(hn, w_ref[:, SWA_Q_W + SWA_KV_W:])
    k_ref[...] = k
    v_ref[...] = v
    k16_ref[...] = k.astype(BF16)
    v16_ref[...] = v.astype(BF16)


def _proj_c(x, g, w):
    n, d = x.shape
    tm = min(ROW_TILE, n)
    row = lambda c: pl.BlockSpec((tm, c), lambda i: (i, 0))
    kv32 = jax.ShapeDtypeStruct((n, SWA_KV_W), F32)
    kv16 = jax.ShapeDtypeStruct((n, SWA_KV_W), BF16)
    return pl.pallas_call(
        _proj_c_kernel,
        grid=(n // tm,),
        in_specs=[row(d), _resident((1, d)), _resident(w.shape)],
        out_specs=[row(SWA_Q_W)] + [row(SWA_KV_W)] * 4,
        out_shape=[jax.ShapeDtypeStruct((n, SWA_Q_W), BF16), kv32, kv32, kv16, kv16],
        compiler_params=_params(("parallel",)),
        name="proj_c",
    )(x, g, w)


def _swap_halves(a):
    return jnp.concatenate([a[:, HEAD_DIM:], a[:, :HEAD_DIM]], axis=1)


def _swa_kernel(sink_ref, q_ref, kprev_ref, vprev_ref, kcur_ref, vcur_ref, o_ref, *, pos0):
    tq = q_ref.shape[1]
    band = (WINDOW_CHUNKS + 1) * CHUNK
    stack = SWA_GROUP // 2
    rows = stack * CHUNK
    first = pos0 + pl.program_id(1) * tq
    kw = jnp.concatenate([kprev_ref[0].astype(BF16), kcur_ref[0].astype(BF16)], axis=0)
    vw = jnp.concatenate([vprev_ref[0].astype(BF16), vcur_ref[0].astype(BF16)], axis=0)
    k_by_half = (kw, _swap_halves(kw))
    v_by_half = (vw, _swap_halves(vw))
    q = q_ref[0]
    in_low = lax.broadcasted_iota(jnp.int32, q.shape, 1) % LANES < HEAD_DIM
    q_by_half = (jnp.where(in_low, q, jnp.zeros_like(q)), jnp.where(in_low, jnp.zeros_like(q), q))
    r_id = lax.broadcasted_iota(jnp.int32, (rows, band), 0)
    c_id = lax.broadcasted_iota(jnp.int32, (rows, band), 1)
    dist = jnp.abs(WINDOW_CHUNKS * CHUNK + r_id % CHUNK - c_id).astype(F32)
    head_in_stack = lax.broadcasted_iota(jnp.int32, (rows, 1), 0) // CHUNK
    col = lax.broadcasted_iota(jnp.int32, (1, band), 1)
    lane = lax.broadcasted_iota(jnp.int32, (CHUNK, LANES), 1)

    def per_head(values):
        out = jnp.full((rows, 1), values[-1], F32)
        for i in range(stack - 2, -1, -1):
            out = jnp.where(head_in_stack == i, values[i], out)
        return out

    heads = {(g, half): [g * SWA_GROUP + 2 * i + half for i in range(stack)]
             for g in range(N_SWA_KV_HEADS) for half in range(2)}
    bias = {key: per_head([2.0 ** (-8.0 * (h + 1) / N_SWA_HEADS) for h in hs]) * dist for key, hs in heads.items()}
    sink = {key: per_head([sink_ref[h] for h in hs]) for key, hs in heads.items()}

    for c in range(tq // CHUNK):
        in_range = first - WINDOW_CHUNKS * CHUNK + c * CHUNK + col >= 0
        win = slice(c * CHUNK, c * CHUNK + band)
        outs = {}
        for (g, half), hs in heads.items():
            qs = jnp.concatenate([q_by_half[half][c * CHUNK:(c + 1) * CHUNK, (h // 2) * LANES:(h // 2 + 1) * LANES]
                                  for h in hs], axis=0)
            s = _dot_nt(qs, k_by_half[g ^ half][win]) - bias[g, half]
            s = jnp.where(in_range, s, NEG_INF)
            m = jnp.maximum(jnp.max(s, axis=1, keepdims=True), sink[g, half])
            p = jnp.exp(s - m)
            l = jnp.sum(p, axis=1, keepdims=True) + jnp.exp(sink[g, half] - m)
            outs[g, half] = _dot(p.astype(BF16), v_by_half[g ^ half][win]) / l
        for g in range(N_SWA_KV_HEADS):
            for i in range(stack):
                pair = (g * SWA_GROUP + 2 * i) // 2
                o_ref[0, c * CHUNK:(c + 1) * CHUNK, pair * LANES:(pair + 1) * LANES] = jnp.where(
                    lane < HEAD_DIM, outs[g, 0][i * CHUNK:(i + 1) * CHUNK],
                    outs[g, 1][i * CHUNK:(i + 1) * CHUNK]).astype(o_ref.dtype)


def _swa(sinks, q, kprev, vprev, kcur, vcur, *, tq, pos0, prev_is_cache):
    b, t, _ = q.shape
    per = tq // SWA_CACHE
    prev_map = ((lambda bi, i, s: (bi, 0, 0)) if prev_is_cache
                else (lambda bi, i, s: (bi, jnp.maximum(i * per - 1, 0), 0)))
    cur = lambda w: pl.BlockSpec((1, tq, w), lambda bi, i, s: (bi, i, 0))
    prev = pl.BlockSpec((1, SWA_CACHE, SWA_KV_W), prev_map)
    return pl.pallas_call(
        functools.partial(_swa_kernel, pos0=pos0),
        grid_spec=pltpu.PrefetchScalarGridSpec(
            num_scalar_prefetch=1, grid=(b, t // tq),
            in_specs=[cur(SWA_Q_W), prev, prev, cur(SWA_KV_W), cur(SWA_KV_W)],
            out_specs=cur(SWA_Q_W)),
        out_shape=jax.ShapeDtypeStruct(q.shape, BF16),
        compiler_params=_params(("parallel", "arbitrary")),
        name="swa",
    )(sinks, q, kprev, vprev, kcur, vcur)


def _route_kernel(x_ref, o_ref, wo_ref, g_ref, wr_ref, x2_ref, xn_ref, route_ref):
    x2 = x_ref[...] + _dot(o_ref[...], wo_ref[...])
    xn = _rms(x2, g_ref[...])
    x2_ref[...] = x2
    _store_chunked(xn_ref, xn)
    hi = xn.astype(BF16)
    mid = (xn - hi.astype(F32)).astype(BF16)
    d = xn.shape[1]
    logits = _dot(hi, wr_ref[:d, :]) + _dot(hi, wr_ref[d:, :]) + _dot(mid, wr_ref[:d, :])
    lane = lax.broadcasted_iota(jnp.int32, logits.shape, 1).astype(F32)
    lg = jnp.where(lane < N_EXPERTS, logits, -jnp.inf)
    m1 = jnp.max(lg, axis=1, keepdims=True)
    i1 = jnp.min(jnp.where(lg == m1, lane, float(LANES)), axis=1, keepdims=True)
    lg2 = jnp.where(lane == i1, -jnp.inf, lg)
    m2 = jnp.max(lg2, axis=1, keepdims=True)
    i2 = jnp.min(jnp.where(lg2 == m2, lane, float(LANES)), axis=1, keepdims=True)
    e = jnp.exp(m2 - m1)
    g1 = 1.0 / (1.0 + e)
    g2 = e / (1.0 + e)
    route = jnp.where(lane == 0, i1, jnp.where(lane == 1, i2, jnp.where(lane == 2, g1, jnp.where(lane == 3, g2, 0.0))))
    route_ref[...] = route


def _route(x, o, wo, g, wr):
    n, d = x.shape
    tm = min(ROW_TILE, n)
    row = lambda c: pl.BlockSpec((tm, c), lambda i: (i, 0))
    xs = jax.ShapeDtypeStruct((n, d), F32)
    return pl.pallas_call(
        _route_kernel,
        grid=(n // tm,),
        in_specs=[row(d), row(SWA_Q_W), _resident(wo.shape), _resident((1, d)), _resident(wr.shape)],
        out_specs=[row(d), _chunked_spec(tm, d), row(LANES)],
        out_shape=[xs, _chunked_shape(n, d), jax.ShapeDtypeStruct((n, LANES), F32)],
        compiler_params=_params(("parallel",)),
        name="route",
    )(x, o, wo, g, wr)


def _chunked_shape(rows, d):
    return jax.ShapeDtypeStruct((d // GATHER_CHUNK, rows, GATHER_CHUNK), F32)


def _chunked_spec(tm, d, index_map=lambda i: (0, i, 0)):
    return pl.BlockSpec((d // GATHER_CHUNK, tm, GATHER_CHUNK), index_map)


def _store_chunked(ref, x):
    for j in range(x.shape[1] // GATHER_CHUNK):
        ref[j] = x[:, j * GATHER_CHUNK:(j + 1) * GATHER_CHUNK]


def _load_chunked(ref):
    return jnp.concatenate([ref[j] for j in range(ref.shape[0])], axis=1)


def _sc_gather_flat(flat, idx):
    m, w = idx.shape[1], flat.shape[1]
    mesh = plsc.VectorSubcoreMesh(core_axis_name="core", subcore_axis_name="subcore")

    @pl.kernel(out_type=jax.ShapeDtypeStruct((m, w), flat.dtype), mesh=mesh)
    def gather(x_hbm, i_hbm, o_hbm):
        def body(i_vmem, o_vmem):
            pltpu.sync_copy(x_hbm.at[i_vmem.at[0]], o_vmem)

        pltpu.emit_pipeline(
            body,
            grid=(m // SC_WINDOW,),
            in_specs=[pl.BlockSpec((1, SC_WINDOW), index_map=lambda i: (0, i))],
            out_specs=[pl.BlockSpec((SC_WINDOW, w), index_map=lambda i: (i, 0))],
            core_axis_name=("core", "subcore"),
            dimension_semantics=(pltpu.PARALLEL,),
        )(i_hbm, o_hbm)

    return gather(flat, idx)


def _gather_indices(rows, n_chunks, src_rows):
    return jnp.concatenate([rows + j * src_rows for j in range(n_chunks)]).reshape(1, -1)


def _sc_gather(src, idx):
    n_chunks, _, w = src.shape
    out = _sc_gather_flat(src.reshape(-1, w), idx)
    return out.reshape(n_chunks, idx.shape[1] // n_chunks, w)


def _expert_kernel(be_ref, nv_ref, x_ref, wg_ref, wu_ref, wd_ref, o_ref, xb_ref, acc_ref):
    blk, f = pl.program_id(0), pl.program_id(1)
    last = pl.num_programs(1) - 1
    valid = blk < nv_ref[0]

    @pl.when(valid & (f == 0))
    def _():
        for j in range(x_ref.shape[0]):
            xb_ref[:, j * GATHER_CHUNK:(j + 1) * GATHER_CHUNK] = x_ref[j].astype(BF16)
        acc_ref[...] = jnp.zeros_like(acc_ref)

    @pl.when(valid)
    def _():
        xb = xb_ref[...]
        h = (_silu(_dot(xb, wg_ref[0].astype(BF16))) * _dot(xb, wu_ref[0].astype(BF16))).astype(BF16)
        acc_ref[...] += _dot(h, wd_ref[0].astype(BF16))

    @pl.when(valid & (f == last))
    def _():
        _store_chunked(o_ref, acc_ref[...])

    @pl.when(jnp.logical_not(valid) & (f == last))
    def _():
        o_ref[...] = jnp.zeros_like(o_ref)


def _experts(block_expert, n_valid, x_slots, wg, wu, wd, *, tm, tf):
    d = wg.shape[1]
    cap = x_slots.shape[1]
    nf = wg.shape[2] // tf
    fsel = lambda b, f, nv: jnp.where(b < nv[0], f, nf - 1)
    slots = _chunked_spec(tm, d, lambda b, f, be, nv: (0, b, 0))
    return pl.pallas_call(
        _expert_kernel,
        grid_spec=pltpu.PrefetchScalarGridSpec(
            num_scalar_prefetch=2, grid=(cap // tm, nf),
            in_specs=[slots,
                      pl.BlockSpec((1, d, tf), lambda b, f, be, nv: (be[b], 0, fsel(b, f, nv))),
                      pl.BlockSpec((1, d, tf), lambda b, f, be, nv: (be[b], 0, fsel(b, f, nv))),
                      pl.BlockSpec((1, tf, d), lambda b, f, be, nv: (be[b], fsel(b, f, nv), 0))],
            out_specs=slots,
            scratch_shapes=[pltpu.VMEM((tm, d), BF16), pltpu.VMEM((tm, d), F32)]),
        out_shape=_chunked_shape(cap, d),
        compiler_params=_params(("arbitrary", "arbitrary")),
        name="moe_experts",
    )(block_expert, n_valid, x_slots, wg, wu, wd)


def _combine_kernel(y1_ref, y2_ref, x_ref, route_ref, g_ref, o_ref):
    route = route_ref[...]
    y = x_ref[...] + (_load_chunked(y1_ref) * route[:, 2:3] + _load_chunked(y2_ref) * route[:, 3:4])
    o_ref[...] = _rms(y, g_ref[...])


def _combine(y_pairs, x2, route, g):
    n, d = x2.shape
    tm = min(ROW_TILE, n)
    row = lambda c: pl.BlockSpec((tm, c), lambda i: (i, 0))
    second = n // tm
    return pl.pallas_call(
        _combine_kernel,
        grid=(n // tm,),
        in_specs=[_chunked_spec(tm, d), _chunked_spec(tm, d, lambda i: (0, i + second, 0)),
                  row(d), row(LANES), _resident((1, d))],
        out_specs=row(d),
        out_shape=jax.ShapeDtypeStruct((n, d), F32),
        compiler_params=_params(("parallel",)),
        name="moe_combine",
    )(y_pairs, y_pairs, x2, route, g)


def _dispatch(route, tm):
    n = route.shape[0]
    expert = route[:, :2].astype(jnp.int32).reshape(-1)
    onehot = (expert[:, None] == jnp.arange(N_EXPERTS, dtype=jnp.int32)[None, :]).astype(jnp.int32)
    csum = jnp.cumsum(onehot, axis=0)
    rank = jnp.sum(onehot * csum, axis=1) - 1
    counts = csum[-1]
    padded = (counts + tm - 1) // tm * tm
    pend = jnp.cumsum(padded)
    pstart = pend - padded
    dest = (jnp.sum(onehot * pstart[None, :], axis=1) + rank).astype(jnp.int32)
    step = max(tm, SLOT_ALIGN)
    cap = -(-(2 * n + N_EXPERTS * (tm - 1)) // step) * step
    n_blocks = cap // tm
    token = jnp.arange(2 * n, dtype=jnp.int32) // 2
    slot_token = jnp.zeros((n_blocks * tm,), jnp.int32).at[dest].set(token, unique_indices=True)
    block_expert = jnp.minimum(
        jnp.searchsorted(pend, jnp.arange(n_blocks, dtype=jnp.int32) * tm, side="right"), N_EXPERTS - 1
    ).astype(jnp.int32)
    n_valid = (pend[-1:] // tm).astype(jnp.int32)
    return slot_token, dest, block_expert, n_valid


def _run(x, caches, wts, *, tq_ab, tq_c, expert_rows):
    b, t, d = x.shape
    n = b * t
    xf = x.reshape(n, d)
    (qsb, ksb, vsb, qfx, kfx, vfx, ksb16, vsb16, kfx16, vfx16, logf) = _proj_ab(
        xf, wts["g_ab"], wts["w_in_ab"], wts["w_f"], wts["b_f"])
    r3 = lambda a: a.reshape(b, t, -1)
    logf3 = r3(logf)
    if caches is None:
        c = _cumsum_time(logf3.transpose(0, 2, 1)).transpose(0, 2, 1)
        c_q, c_p = c, c
        kp_sb, vp_sb, kp_fx, vp_fx = r3(ksb16), r3(vsb16), r3(kfx16), r3(vfx16)
        td = tk = min(ATTN_TILE, t)
        base, per_q = 0, 1
    else:
        p_sb_k, p_sb_v, p_fx_k, p_fx_v, p_logf = caches[:5]
        past = p_logf.shape[1]
        pad = (-(past + t)) % LANES
        allf = jnp.concatenate([p_logf.astype(F32), logf3, jnp.zeros((b, pad, N_FOX_HEADS), F32)], axis=1)
        c = _cumsum_time(allf.transpose(0, 2, 1)).transpose(0, 2, 1)
        c_q, c_p = c[:, past:past + t], c[:, :past]
        kp_sb, vp_sb, kp_fx, vp_fx = (a.reshape(b, past, -1) for a in (p_sb_k, p_sb_v, p_fx_k, p_fx_v))
        td, tk = t, min(SAMPLE_ATTN_TILE, past)
        base, per_q = past // tk, 0
    kw = dict(tq=tq_ab, td=td, tk=tk, base=base, per_q=per_q)
    o_sb = _attn_ab(_sb_kernel, r3(qsb), r3(ksb16), r3(vsb16), kp_sb, vp_sb, [], [], name="attn_sb", **kw)
    extra, extra_specs = _fox_bias_operands(c_q, c_p, tq_ab, td, tk)
    o_fx = _attn_ab(_fox_kernel, r3(qfx), r3(kfx16), r3(vfx16), kp_fx, vp_fx, extra, extra_specs,
                    name="attn_fox", **kw)
    x1 = _ffn_dense(xf, o_sb.reshape(n, SB_W), o_fx.reshape(n, FOX_W), wts["w_out_ab"], wts["g_ffn"],
                    wts["w_gate"], wts["w_up"], wts["w_down"])

    qc, kc, vc, kc16, vc16 = _proj_c(x1, wts["g_c"], wts["w_in_c"])
    if caches is None:
        kprev, vprev, pos0 = r3(kc16), r3(vc16), 0
        new_k, new_v = r3(kc)[:, t - SWA_CACHE:], r3(vc)[:, t - SWA_CACHE:]
    else:
        p_k, p_v = (a.reshape(b, SWA_CACHE, SWA_KV_W) for a in caches[5:])
        kprev, vprev, pos0 = p_k, p_v, caches[0].shape[1]
        new_k = jnp.concatenate([p_k, r3(kc)], axis=1)[:, t:]
        new_v = jnp.concatenate([p_v, r3(vc)], axis=1)[:, t:]
    o_c = _swa(wts["sinks"], r3(qc), kprev, vprev, r3(kc16), r3(vc16), tq=tq_c, pos0=pos0,
               prev_is_cache=caches is not None)

    x2, xn, route = _route(x1, o_c.reshape(n, SWA_Q_W), wts["w_out_c"], wts["g_moe"], wts["w_router"])
    slot_token, dest, block_expert, n_valid = _dispatch(route, expert_rows)
    heads = lambda a, h: a.reshape(1, b, -1, h, HEAD_DIM)
    states = (heads(ksb, N_SB_HEADS), heads(vsb, N_SB_HEADS), heads(kfx, N_FOX_HEADS), heads(vfx, N_FOX_HEADS),
              logf3[None], heads(new_k, N_SWA_KV_HEADS), heads(new_v, N_SWA_KV_HEADS))
    n_chunks = d // GATHER_CHUNK
    pair_rows = dest.reshape(n, 2).T.reshape(-1)
    return dict(xn=xn, slot_idx=_gather_indices(slot_token, n_chunks, n),
                pair_idx=_gather_indices(pair_rows, n_chunks, slot_token.shape[0]),
                block_expert=block_expert, n_valid=n_valid, x2=x2, route=route, states=states, shape=(b, t, d),
                expert_rows=expert_rows)


def _moe_experts(ctx, wts):
    x_slots = _sc_gather(ctx["xn"], ctx["slot_idx"])
    y_slots = _experts(ctx["block_expert"], ctx["n_valid"], x_slots, wts["w_gate_moe"], wts["w_up_moe"],
                       wts["w_down_moe"], tm=ctx["expert_rows"],
                       tf=EXPERT_COLS if ctx["expert_rows"] == EXPERT_ROWS else SAMPLE_EXPERT_COLS)
    return dict(ctx, y_slots=y_slots)


def _moe_combine(ctx, wts):
    y_pairs = _sc_gather(ctx["y_slots"], ctx["pair_idx"])
    y = _combine(y_pairs, ctx["x2"], ctx["route"], wts["g_final"])
    return y.reshape(ctx["shape"]), ctx["states"]


def _issue_after(first, later):
    return lax.optimization_barrier((first, later))


def kernel(x_prompt, x_sample, cache_sb_k, cache_sb_v, cache_fox_k, cache_fox_v, cache_fox_logf, cache_swa_k,
           cache_swa_v, norm_mix_ab, w_in_ab, b_forget, w_out_ab, norm_ffn_dense, w_gate_dense, w_up_dense,
           w_down_dense, norm_mix_c, w_in_c, sinks, w_out_c, norm_ffn_moe, w_router, w_gate_moe, w_up_moe,
           w_down_moe, norm_final):
    d = x_prompt.shape[-1]
    main_w = 3 * SB_W + 3 * FOX_W
    w_f = jnp.zeros((d, LANES), F32).at[:, :N_FOX_HEADS].set(w_in_ab[0][:, main_w:])
    wr = jnp.zeros((d, LANES), F32).at[:, :N_EXPERTS].set(w_router[0])
    wr_hi = wr.astype(BF16)
    wr_mid = (wr - wr_hi.astype(F32)).astype(BF16)
    wts = dict(
        g_ab=norm_mix_ab[0][None], w_in_ab=w_in_ab[0][:, :main_w].astype(BF16), w_f=w_f.astype(BF16),
        b_f=b_forget[0][None], w_out_ab=w_out_ab[0].astype(BF16), g_ffn=norm_ffn_dense[0][None],
        w_gate=w_gate_dense[0].astype(BF16), w_up=w_up_dense[0].astype(BF16), w_down=w_down_dense[0].astype(BF16),
        g_c=norm_mix_c[0][None], w_in_c=w_in_c[0].astype(BF16), sinks=sinks[0], w_out_c=w_out_c[0].astype(BF16),
        g_moe=norm_ffn_moe[0][None], w_router=jnp.concatenate([wr_hi, wr_mid], axis=0),
        w_gate_moe=w_gate_moe[0], w_up_moe=w_up_moe[0], w_down_moe=w_down_moe[0], g_final=norm_final[None])
    t_p, t_s = x_prompt.shape[1], x_sample.shape[1]
    caches = (cache_sb_k[0], cache_sb_v[0], cache_fox_k[0], cache_fox_v[0], cache_fox_logf[0],
              cache_swa_k[0], cache_swa_v[0])
    prompt = _run(x_prompt, None, wts, tq_ab=min(ATTN_ROWS, t_p), tq_c=min(SWA_TILE, t_p), expert_rows=EXPERT_ROWS)
    prompt["slot_idx"], x_sample = _issue_after(prompt["slot_idx"], x_sample)
    sample = _run(x_sample, caches, wts, tq_ab=t_s, tq_c=t_s, expert_rows=SAMPLE_EXPERT_ROWS)
    prompt = _moe_experts(prompt, wts)
    prompt["y_slots"], sample["xn"] = _issue_after(prompt["y_slots"], sample["xn"])
    sample = _moe_experts(sample, wts)
    y_p, st_p = _moe_combine(prompt, wts)
    y_s, st_s = _moe_combine(sample, wts)
    return (y_p, y_s) + st_p + st_s
```

```python
import functools

import jax
import jax.numpy as jnp
from jax import lax
from jax.experimental import pallas as pl
from jax.experimental.pallas import tpu as pltpu
from jax.experimental.pallas import tpu_sc as plsc

F32 = jnp.float32
BF16 = jnp.bfloat16

HEAD_DIM = 64
LANES = 128
N_SB_HEADS = 8
N_FOX_HEADS = 8
N_SWA_HEADS = 16
N_SWA_KV_HEADS = 2
SWA_GROUP = N_SWA_HEADS // N_SWA_KV_HEADS
CHUNK = 64
WINDOW_CHUNKS = 2
SWA_CACHE = 128
N_EXPERTS = 8
RMS_EPS = 1e-6
NEG_INF = -1e30
ATTN_SCALE = HEAD_DIM ** -0.5
SB_W = N_SB_HEADS * HEAD_DIM
FOX_W = N_FOX_HEADS * HEAD_DIM
SWA_Q_W = N_SWA_HEADS * HEAD_DIM
SWA_KV_W = N_SWA_KV_HEADS * HEAD_DIM

VMEM_LIMIT_BYTES = 56 * 1024 * 1024
ROW_TILE = 512
ATTN_TILE = 512
ATTN_ROWS = 512
SAMPLE_ATTN_TILE = 2048
CUMSUM_SUB = 256
SWA_TILE = 256
SAMPLE_EXPERT_ROWS = 256
LOG2_E = 1.4426950408889634
EXPERT_ROWS = 1024
EXPERT_COLS = 512
SAMPLE_EXPERT_COLS = 896
GATHER_CHUNK = 256
SC_WINDOW = 128
SLOT_ALIGN = 1024
NT_DIMS = (((1,), (1,)), ((), ()))


def _params(sem):
    return pltpu.CompilerParams(dimension_semantics=sem, vmem_limit_bytes=VMEM_LIMIT_BYTES)


def _resident(shape):
    n = len(shape)
    return pl.BlockSpec(shape, lambda *_: (0,) * n, pipeline_mode=pl.Buffered(1))


def _rms(x, g):
    return x * lax.rsqrt(jnp.mean(x * x, axis=-1, keepdims=True) + RMS_EPS) * g


def _log_sigmoid(z):
    return jnp.minimum(z, 0.0) - jnp.log(1.0 + jnp.exp(-jnp.abs(z)))


def _silu(a):
    return a * (1.0 / (1.0 + jnp.exp(-a)))


def _dot(a, b):
    return jnp.dot(a, b, preferred_element_type=F32)


def _dot_nt(a, b):
    return lax.dot_general(a, b, NT_DIMS, preferred_element_type=F32)


def _split3(x):
    hi = x.astype(BF16)
    r = x - hi.astype(F32)
    mid = r.astype(BF16)
    lo = (r - mid.astype(F32)).astype(BF16)
    return hi, mid, lo


def _proj_ab_kernel(x_ref, g_ref, w_ref, wf_ref, bf_ref,
                    qsb_ref, ksb_ref, vsb_ref, qfx_ref, kfx_ref, vfx_ref,
                    ksb16_ref, vsb16_ref, kfx16_ref, vfx16_ref, logf_ref):
    hn = _rms(x_ref[...], g_ref[...]).astype(BF16)
    f32_outs = (None, ksb_ref, vsb_ref, None, kfx_ref, vfx_ref)
    b16_outs = (qsb_ref, ksb16_ref, vsb16_ref, qfx_ref, kfx16_ref, vfx16_ref)
    for c in range(6):
        p = _dot(hn, w_ref[:, c * SB_W:(c + 1) * SB_W])
        if f32_outs[c] is None:
            b16_outs[c][...] = (p * ATTN_SCALE).astype(BF16)
        else:
            f32_outs[c][...] = p.reshape(p.shape[0], SB_W // HEAD_DIM, HEAD_DIM)
            b16_outs[c][...] = p.astype(BF16)
    f = _dot(hn, wf_ref[...])[:, :N_FOX_HEADS] + bf_ref[...]
    logf_ref[...] = _log_sigmoid(f)


def _proj_ab(x, g, w, wf, bf):
    n, d = x.shape
    tm = min(ROW_TILE, n)
    row = lambda c: pl.BlockSpec((tm, c), lambda i: (i, 0))
    f32o = jax.ShapeDtypeStruct((n, SB_W // HEAD_DIM, HEAD_DIM), F32)
    b16o = jax.ShapeDtypeStruct((n, SB_W), BF16)
    st = pl.BlockSpec((tm, SB_W // HEAD_DIM, HEAD_DIM), lambda i: (i, 0, 0))
    w16 = row(SB_W)
    return pl.pallas_call(
        _proj_ab_kernel,
        grid=(n // tm,),
        in_specs=[row(d), _resident((1, d)), _resident(w.shape), _resident(wf.shape), _resident((1, N_FOX_HEADS))],
        out_specs=[w16, st, st, w16, st, st, w16, w16, w16, w16, row(N_FOX_HEADS)],
        out_shape=[b16o, f32o, f32o, b16o, f32o, f32o, b16o, b16o, b16o, b16o,
                   jax.ShapeDtypeStruct((n, N_FOX_HEADS), F32)],
        compiler_params=_params(("parallel",)),
        name="proj_ab",
    )(x, g, w, wf, bf)


def _cumsum_kernel(x_ref, o_ref):
    t = x_ref.shape[-1]
    r = lax.broadcasted_iota(jnp.int32, (LANES, LANES), 0)
    c = lax.broadcasted_iota(jnp.int32, (LANES, LANES), 1)
    tri = (r <= c).astype(BF16)
    carry = jnp.zeros((x_ref.shape[1], 1), F32)
    for i in range(t // LANES):
        hi, mid, lo = _split3(x_ref[0, :, i * LANES:(i + 1) * LANES])
        cs = _dot(hi, tri) + _dot(mid, tri) + _dot(lo, tri) + carry
        o_ref[0, :, i * LANES:(i + 1) * LANES] = cs
        carry = cs[:, LANES - 1:LANES]


def _cumsum_time(x):
    b, h, t = x.shape
    spec = pl.BlockSpec((1, h, t), lambda i: (i, 0, 0))
    return pl.pallas_call(
        _cumsum_kernel, grid=(b,), in_specs=[spec], out_specs=spec,
        out_shape=jax.ShapeDtypeStruct(x.shape, F32),
        compiler_params=_params(("parallel",)), name="cumsum_time",
    )(x)


def _stacked_heads(q):
    lane = lax.broadcasted_iota(jnp.int32, q.shape, 1)
    zero = jnp.zeros_like(q)
    return jnp.concatenate([jnp.where(lane < HEAD_DIM, q, zero), jnp.where(lane >= HEAD_DIM, q, zero)], axis=0)


def _unstack_heads(acc):
    tq = acc.shape[0] // 2
    lane = lax.broadcasted_iota(jnp.int32, (tq, LANES), 1)
    return jnp.where(lane < HEAD_DIM, acc[:tq], acc[tq:])


def _causal_mask(tq, td, first_row, strict):
    row = first_row + lax.broadcasted_iota(jnp.int32, (2 * tq, td), 0) % tq
    col = lax.broadcasted_iota(jnp.int32, (2 * tq, td), 1)
    return col < row if strict else col <= row


def _strict_upper(n):
    j = lax.broadcasted_iota(jnp.int32, (n, n), 0)
    s = lax.broadcasted_iota(jnp.int32, (n, n), 1)
    return (j > s).astype(BF16)


def _sb_block(qk, v, total, acc_ref, u, vis):
    sub = u.shape[0]
    z = qk * LOG2_E
    neg_abs = lax.bitcast_convert_type(lax.bitcast_convert_type(z, jnp.uint32) | jnp.uint32(0x80000000), F32)
    ls = jnp.minimum(z, 0.0) - jnp.log2(1.0 + jnp.exp2(neg_abs))
    lk = ls - z
    if vis is not None:
        lk = jnp.where(vis, lk, 0.0)
    lk = lk.astype(BF16)
    for s in reversed(range(qk.shape[1] // sub)):
        cols = slice(s * sub, (s + 1) * sub)
        after = _dot(lk[:, cols], u)
        w = jnp.exp2(ls[:, cols] + after + total)
        if vis is not None:
            w = jnp.where(vis[:, cols], w, 0.0)
        acc_ref[...] += _dot(w.astype(BF16), v[cols, :])
        total = total + after[:, 0:1] + lk[:, s * sub:s * sub + 1].astype(F32)
    return total


def _sb_kernel(q_ref, kd_ref, vd_ref, kp_ref, vp_ref, o_ref, acc_ref, *, tk, base, per_q):
    tq, td = q_ref.shape[1], kd_ref.shape[1]
    qs = _stacked_heads(q_ref[0])
    first_row = (pl.program_id(2) % (td // tq)) * tq
    vis = _causal_mask(tq, td, first_row, strict=True)
    u_diag = _strict_upper(min(CUMSUM_SUB, td))
    u_past = _strict_upper(CUMSUM_SUB)
    acc_ref[...] = jnp.zeros_like(acc_ref)
    total = _sb_block(_dot_nt(qs, kd_ref[0]), vd_ref[0], jnp.zeros((2 * tq, 1), F32), acc_ref, u_diag, vis)

    n_past = base + pl.program_id(2) // (td // tq) * per_q

    def body(jj, total):
        start = pl.multiple_of((n_past - 1 - jj) * tk, tk)
        k = kp_ref[0, pl.ds(start, tk), :].astype(BF16)
        v = vp_ref[0, pl.ds(start, tk), :].astype(BF16)
        return _sb_block(_dot_nt(qs, k), v, total, acc_ref, u_past, None)

    lax.fori_loop(0, n_past, body, total)
    o_ref[0] = _unstack_heads(acc_ref[...]).astype(o_ref.dtype)


def _fox_block(qk, v, ck, m, l, acc_ref, vis):
    tq = qk.shape[0] // 2
    s = jnp.concatenate([qk[:tq] - ck[0:1, :], qk[tq:] - ck[1:2, :]], axis=0)
    if vis is not None:
        s = jnp.where(vis, s, NEG_INF)
    m_new = jnp.maximum(m, jnp.max(s, axis=1, keepdims=True))
    alpha = jnp.exp(m - m_new)
    p = jnp.exp(s - m_new)
    acc_ref[...] = alpha * acc_ref[...] + _dot(p.astype(BF16), v)
    return m_new, alpha * l + jnp.sum(p, axis=1, keepdims=True)


def _fox_kernel(q_ref, kd_ref, vd_ref, kp_ref, vp_ref, ckd_ref, ckp_ref, o_ref, acc_ref, *, tk, base, per_q):
    tq, td = q_ref.shape[1], kd_ref.shape[1]
    qs = _stacked_heads(q_ref[0])
    first_row = (pl.program_id(2) % (td // tq)) * tq
    vis = _causal_mask(tq, td, first_row, strict=False)
    acc_ref[...] = jnp.zeros_like(acc_ref)
    m0 = jnp.full((2 * tq, 1), NEG_INF, F32)
    l0 = jnp.zeros((2 * tq, 1), F32)
    stats = _fox_block(_dot_nt(qs, kd_ref[0]), vd_ref[0], ckd_ref[0, 0], m0, l0, acc_ref, vis)

    n_past = base + pl.program_id(2) // (td // tq) * per_q

    def body(j, stats):
        start = pl.multiple_of(j * tk, tk)
        k = kp_ref[0, pl.ds(start, tk), :].astype(BF16)
        v = vp_ref[0, pl.ds(start, tk), :].astype(BF16)
        return _fox_block(_dot_nt(qs, k), v, ckp_ref[0, 0, j], stats[0], stats[1], acc_ref, None)

    _, l = lax.fori_loop(0, n_past, body, stats)
    o_ref[0] = _unstack_heads(acc_ref[...] / l).astype(o_ref.dtype)


def _attn_ab(kernel_fn, q, kd, vd, kp, vp, extra, extra_specs, *, tq, td, tk, base, per_q, name):
    b, t, w = q.shape
    tp = kp.shape[1]
    grid = (b, w // LANES, t // tq)
    qspec = pl.BlockSpec((1, tq, LANES), lambda bi, hp, i: (bi, i, hp))
    dspec = pl.BlockSpec((1, td, LANES), lambda bi, hp, i: (bi, i // (td // tq), hp))
    pspec = pl.BlockSpec((1, tp, LANES), lambda bi, hp, i: (bi, 0, hp))
    return pl.pallas_call(
        functools.partial(kernel_fn, tk=tk, base=base, per_q=per_q),
        grid=grid,
        in_specs=[qspec, dspec, dspec, pspec, pspec] + extra_specs,
        out_specs=qspec,
        out_shape=jax.ShapeDtypeStruct(q.shape, BF16),
        scratch_shapes=[pltpu.VMEM((2 * tq, LANES), F32)],
        compiler_params=_params(("parallel", "parallel", "arbitrary")),
        name=name,
    )(q, kd, vd, kp, vp, *extra)


def _fox_bias_operands(c_q, c_p, tq, td, tk):
    b, t, h = c_q.shape
    tp = c_p.shape[1]
    hp = h // 2
    ckd = c_q.reshape(b, t, hp, 2).transpose(0, 2, 3, 1)
    ckp = c_p.reshape(b, tp // tk, tk, hp, 2).transpose(0, 3, 1, 4, 2)
    specs = [pl.BlockSpec((1, 1, 2, td), lambda bi, p, i: (bi, p, 0, i // (td // tq))),
             pl.BlockSpec((1, 1, tp // tk, 2, tk), lambda bi, p, i: (bi, p, 0, 0, 0))]
    return [ckd, ckp], specs


def _ffn_dense_kernel(x_ref, osb_ref, ofx_ref, wo_ref, g_ref, wg_ref, wu_ref, wd_ref, o_ref, *, tf):
    o_ref[...] = x_ref[...] + _dot(osb_ref[...], wo_ref[:SB_W, :]) + _dot(ofx_ref[...], wo_ref[SB_W:, :])
    hn = _rms(o_ref[...], g_ref[...]).astype(BF16)
    for c in range(wg_ref.shape[1] // tf):
        a = _dot(hn, wg_ref[:, c * tf:(c + 1) * tf])
        u = _dot(hn, wu_ref[:, c * tf:(c + 1) * tf])
        o_ref[...] += _dot((_silu(a) * u).astype(BF16), wd_ref[c * tf:(c + 1) * tf, :])


def _ffn_dense(x, osb, ofx, wo, g, wg, wu, wd):
    n, d = x.shape
    tm = min(ROW_TILE, n)
    row = lambda c: pl.BlockSpec((tm, c), lambda i: (i, 0))
    return pl.pallas_call(
        functools.partial(_ffn_dense_kernel, tf=256),
        grid=(n // tm,),
        in_specs=[row(d), row(SB_W), row(FOX_W), _resident(wo.shape), _resident((1, d)),
                  _resident(wg.shape), _resident(wu.shape), _resident(wd.shape)],
        out_specs=row(d),
        out_shape=jax.ShapeDtypeStruct((n, d), F32),
        compiler_params=_params(("parallel",)),
        name="ffn_dense",
    )(x, osb, ofx, wo, g, wg, wu, wd)


def _proj_c_kernel(x_ref, g_ref, w_ref, q_ref, k_ref, v_ref, k16_ref, v16_ref):
    hn = _rms(x_ref[...], g_ref[...]).astype(BF16)
    q_ref[...] = (_dot(hn, w_ref[:, :SWA_Q_W]) * ATTN_SCALE).astype(BF16)
    k = _dot(hn, w_ref[:, SWA_Q_W:SWA_Q_W + SWA_KV_W])
    v = _dot(hn, w_ref[:, SWA_Q_W + SWA_KV_W:])
    k_ref[...] = k
    v_ref[...] = v
    k16_ref[...] = k.astype(BF16)
    v16_ref[...] = v.astype(BF16)


def _proj_c(x, g, w):
    n, d = x.shape
    tm = min(ROW_TILE, n)
    row = lambda c: pl.BlockSpec((tm, c), lambda i: (i, 0))
    kv32 = jax.ShapeDtypeStruct((n, SWA_KV_W), F32)
    kv16 = jax.ShapeDtypeStruct((n, SWA_KV_W), BF16)
    return pl.pallas_call(
        _proj_c_kernel,
        grid=(n // tm,),
        in_specs=[row(d), _resident((1, d)), _resident(w.shape)],
        out_specs=[row(SWA_Q_W)] + [row(SWA_KV_W)] * 4,
        out_shape=[jax.ShapeDtypeStruct((n, SWA_Q_W), BF16), kv32, kv32, kv16, kv16],
        compiler_params=_params(("parallel",)),
        name="proj_c",
    )(x, g, w)


def _swap_halves(a):
    return jnp.concatenate([a[:, HEAD_DIM:], a[:, :HEAD_DIM]], axis=1)


def _swa_kernel(sink_ref, q_ref, kprev_ref, vprev_ref, kcur_ref, vcur_ref, o_ref, *, pos0):
    tq = q_ref.shape[1]
    band = (WINDOW_CHUNKS + 1) * CHUNK
    stack = SWA_GROUP // 2
    rows = stack * CHUNK
    first = pos0 + pl.program_id(1) * tq
    kw = jnp.concatenate([kprev_ref[0].astype(BF16), kcur_ref[0].astype(BF16)], axis=0)
    vw = jnp.concatenate([vprev_ref[0].astype(BF16), vcur_ref[0].astype(BF16)], axis=0)
    k_by_half = (kw, _swap_halves(kw))
    v_by_half = (vw, _swap_halves(vw))
    q = q_ref[0]
    in_low = lax.broadcasted_iota(jnp.int32, q.shape, 1) % LANES < HEAD_DIM
    q_by_half = (jnp.where(in_low, q, jnp.zeros_like(q)), jnp.where(in_low, jnp.zeros_like(q), q))
    r_id = lax.broadcasted_iota(jnp.int32, (rows, band), 0)
    c_id = lax.broadcasted_iota(jnp.int32, (rows, band), 1)
    dist = jnp.abs(WINDOW_CHUNKS * CHUNK + r_id % CHUNK - c_id).astype(F32)
    head_in_stack = lax.broadcasted_iota(jnp.int32, (rows, 1), 0) // CHUNK
    col = lax.broadcasted_iota(jnp.int32, (1, band), 1)
    lane = lax.broadcasted_iota(jnp.int32, (CHUNK, LANES), 1)

    def per_head(values):
        out = jnp.full((rows, 1), values[-1], F32)
        for i in range(stack - 2, -1, -1):
            out = jnp.where(head_in_stack == i, values[i], out)
        return out

    heads = {(g, half): [g * SWA_GROUP + 2 * i + half for i in range(stack)]
             for g in range(N_SWA_KV_HEADS) for half in range(2)}
    bias = {key: per_head([2.0 ** (-8.0 * (h + 1) / N_SWA_HEADS) for h in hs]) * dist for key, hs in heads.items()}
    sink = {key: per_head([sink_ref[h] for h in hs]) for key, hs in heads.items()}

    for c in range(tq // CHUNK):
        in_range = first - WINDOW_CHUNKS * CHUNK + c * CHUNK + col >= 0
        win = slice(c * CHUNK, c * CHUNK + band)
        outs = {}
        for (g, half), hs in heads.items():
            qs = jnp.concatenate([q_by_half[half][c * CHUNK:(c + 1) * CHUNK, (h // 2) * LANES:(h // 2 + 1) * LANES]
                                  for h in hs], axis=0)
            s = _dot_nt(qs, k_by_half[g ^ half][win]) - bias[g, half]
            s = jnp.where(in_range, s, NEG_INF)
            m = jnp.maximum(jnp.max(s, axis=1, keepdims=True), sink[g, half])
            p = jnp.exp(s - m)
            l = jnp.sum(p, axis=1, keepdims=True) + jnp.exp(sink[g, half] - m)
            outs[g, half] = _dot(p.astype(BF16), v_by_half[g ^ half][win]) / l
        for g in range(N_SWA_KV_HEADS):
            for i in range(stack):
                pair = (g * SWA_GROUP + 2 * i) // 2
                o_ref[0, c * CHUNK:(c + 1) * CHUNK, pair * LANES:(pair + 1) * LANES] = jnp.where(
                    lane < HEAD_DIM, outs[g, 0][i * CHUNK:(i + 1) * CHUNK],
                    outs[g, 1][i * CHUNK:(i + 1) * CHUNK]).astype(o_ref.dtype)


def _swa(sinks, q, kprev, vprev, kcur, vcur, *, tq, pos0, prev_is_cache):
    b, t, _ = q.shape
    per = tq // SWA_CACHE
    prev_map = ((lambda bi, i, s: (bi, 0, 0)) if prev_is_cache
                else (lambda bi, i, s: (bi, jnp.maximum(i * per - 1, 0), 0)))
    cur = lambda w: pl.BlockSpec((1, tq, w), lambda bi, i, s: (bi, i, 0))
    prev = pl.BlockSpec((1, SWA_CACHE, SWA_KV_W), prev_map)
    return pl.pallas_call(
        functools.partial(_swa_kernel, pos0=pos0),
        grid_spec=pltpu.PrefetchScalarGridSpec(
            num_scalar_prefetch=1, grid=(b, t // tq),
            in_specs=[cur(SWA_Q_W), prev, prev, cur(SWA_KV_W), cur(SWA_KV_W)],
            out_specs=cur(SWA_Q_W)),
        out_shape=jax.ShapeDtypeStruct(q.shape, BF16),
        compiler_params=_params(("parallel", "arbitrary")),
        name="swa",
    )(sinks, q, kprev, vprev, kcur, vcur)


def _route_kernel(x_ref, o_ref, wo_ref, g_ref, wr_ref, x2_ref, xn_ref, route_ref):
    x2 = x_ref[...] + _dot(o_ref[...], wo_ref[...])
    xn = _rms(x2, g_ref[...])
    x2_ref[...] = x2
    _store_chunked(xn_ref, xn)
    hi = xn.astype(BF16)
    mid = (xn - hi.astype(F32)).astype(BF16)
    d = xn.shape[1]
    logits = _dot(hi, wr_ref[:d, :]) + _dot(hi, wr_ref[d:, :]) + _dot(mid, wr_ref[:d, :])
    lane = lax.broadcasted_iota(jnp.int32, logits.shape, 1).astype(F32)
    lg = jnp.where(lane < N_EXPERTS, logits, -jnp.inf)
    m1 = jnp.max(lg, axis=1, keepdims=True)
    i1 = jnp.min(jnp.where(lg == m1, lane, float(LANES)), axis=1, keepdims=True)
    lg2 = jnp.where(lane == i1, -jnp.inf, lg)
    m2 = jnp.max(lg2, axis=1, keepdims=True)
    i2 = jnp.min(jnp.where(lg2 == m2, lane, float(LANES)), axis=1, keepdims=True)
    e = jnp.exp(m2 - m1)
    g1 = 1.0 / (1.0 + e)
    g2 = e / (1.0 + e)
    route = jnp.where(lane == 0, i1, jnp.where(lane == 1, i2, jnp.where(lane == 2, g1, jnp.where(lane == 3, g2, 0.0))))
    route_ref[...] = route


def _route(x, o, wo, g, wr):
    n, d = x.shape
    tm = min(ROW_TILE, n)
    row = lambda c: pl.BlockSpec((tm, c), lambda i: (i, 0))
    xs = jax.ShapeDtypeStruct((n, d), F32)
    return pl.pallas_call(
        _route_kernel,
        grid=(n // tm,),
        in_specs=[row(d), row(SWA_Q_W), _resident(wo.shape), _resident((1, d)), _resident(wr.shape)],
        out_specs=[row(d), _chunked_spec(tm, d), row(LANES)],
        out_shape=[xs, _chunked_shape(n, d), jax.ShapeDtypeStruct((n, LANES), F32)],
        compiler_params=_params(("parallel",)),
        name="route",
    )(x, o, wo, g, wr)


def _chunked_shape(rows, d):
    return jax.ShapeDtypeStruct((d // GATHER_CHUNK, rows, GATHER_CHUNK), F32)


def _chunked_spec(tm, d, index_map=lambda i: (0, i, 0)):
    return pl.BlockSpec((d // GATHER_CHUNK, tm, GATHER_CHUNK), index_map)


def _store_chunked(ref, x):
    for j in range(x.shape[1] // GATHER_CHUNK):
        ref[j] = x[:, j * GATHER_CHUNK:(j + 1) * GATHER_CHUNK]


def _load_chunked(ref):
    return jnp.concatenate([ref[j] for j in range(ref.shape[0])], axis=1)


def _sc_gather_flat(flat, idx):
    m, w = idx.shape[1], flat.shape[1]
    mesh = plsc.VectorSubcoreMesh(core_axis_name="core", subcore_axis_name="subcore")

    @pl.kernel(out_type=jax.ShapeDtypeStruct((m, w), flat.dtype), mesh=mesh)
    def gather(x_hbm, i_hbm, o_hbm):
        def body(i_vmem, o_vmem):
            pltpu.sync_copy(x_hbm.at[i_vmem.at[0]], o_vmem)

        pltpu.emit_pipeline(
            body,
            grid=(m // SC_WINDOW,),
            in_specs=[pl.BlockSpec((1, SC_WINDOW), index_map=lambda i: (0, i))],
            out_specs=[pl.BlockSpec((SC_WINDOW, w), index_map=lambda i: (i, 0))],
            core_axis_name=("core", "subcore"),
            dimension_semantics=(pltpu.PARALLEL,),
        )(i_hbm, o_hbm)

    return gather(flat, idx)


def _gather_indices(rows, n_chunks, src_rows):
    return jnp.concatenate([rows + j * src_rows for j in range(n_chunks)]).reshape(1, -1)


def _sc_gather(src, idx):
    n_chunks, _, w = src.shape
    out = _sc_gather_flat(src.reshape(-1, w), idx)
    return out.reshape(n_chunks, idx.shape[1] // n_chunks, w)


def _expert_kernel(be_ref, nv_ref, x_ref, wg_ref, wu_ref, wd_ref, o_ref, xb_ref, acc_ref):
    blk, f = pl.program_id(0), pl.program_id(1)
    last = pl.num_programs(1) - 1
    valid = blk < nv_ref[0]

    @pl.when(valid & (f == 0))
    def _():
        for j in range(x_ref.shape[0]):
            xb_ref[:, j * GATHER_CHUNK:(j + 1) * GATHER_CHUNK] = x_ref[j].astype(BF16)
        acc_ref[...] = jnp.zeros_like(acc_ref)

    @pl.when(valid)
    def _():
        xb = xb_ref[...]
        h = (_silu(_dot(xb, wg_ref[0].astype(BF16))) * _dot(xb, wu_ref[0].astype(BF16))).astype(BF16)
        acc_ref[...] += _dot(h, wd_ref[0].astype(BF16))

    @pl.when(valid & (f == last))
    def _():
        _store_chunked(o_ref, acc_ref[...])

    @pl.when(jnp.logical_not(valid) & (f == last))
    def _():
        o_ref[...] = jnp.zeros_like(o_ref)


def _experts(block_expert, n_valid, x_slots, wg, wu, wd, *, tm, tf):
    d = wg.shape[1]
    cap = x_slots.shape[1]
    nf = wg.shape[2] // tf
    fsel = lambda b, f, nv: jnp.where(b < nv[0], f, nf - 1)
    slots = _chunked_spec(tm, d, lambda b, f, be, nv: (0, b, 0))
    return pl.pallas_call(
        _expert_kernel,
        grid_spec=pltpu.PrefetchScalarGridSpec(
            num_scalar_prefetch=2, grid=(cap // tm, nf),
            in_specs=[slots,
                      pl.BlockSpec((1, d, tf), lambda b, f, be, nv: (be[b], 0, fsel(b, f, nv))),
                      pl.BlockSpec((1, d, tf), lambda b, f, be, nv: (be[b], 0, fsel(b, f, nv))),
                      pl.BlockSpec((1, tf, d), lambda b, f, be, nv: (be[b], fsel(b, f, nv), 0))],
            out_specs=slots,
            scratch_shapes=[pltpu.VMEM((tm, d), BF16), pltpu.VMEM((tm, d), F32)]),
        out_shape=_chunked_shape(cap, d),
        compiler_params=_params(("arbitrary", "arbitrary")),
        name="moe_experts",
    )(block_expert, n_valid, x_slots, wg, wu, wd)


def _combine_kernel(y1_ref, y2_ref, x_ref, route_ref, g_ref, o_ref):
    route = route_ref[...]
    y = x_ref[...] + (_load_chunked(y1_ref) * route[:, 2:3] + _load_chunked(y2_ref) * route[:, 3:4])
    o_ref[...] = _rms(y, g_ref[...])


def _combine(y_pairs, x2, route, g):
    n, d = x2.shape
    tm = min(ROW_TILE, n)
    row = lambda c: pl.BlockSpec((tm, c), lambda i: (i, 0))
    second = n // tm
    return pl.pallas_call(
        _combine_kernel,
        grid=(n // tm,),
        in_specs=[_chunked_spec(tm, d), _chunked_spec(tm, d, lambda i: (0, i + second, 0)),
                  row(d), row(LANES), _resident((1, d))],
        out_specs=row(d),
        out_shape=jax.ShapeDtypeStruct((n, d), F32),
        compiler_params=_params(("parallel",)),
        name="moe_combine",
    )(y_pairs, y_pairs, x2, route, g)


def _dispatch(route, tm):
    n = route.shape[0]
    expert = route[:, :2].astype(jnp.int32).reshape(-1)
    onehot = (expert[:, None] == jnp.arange(N_EXPERTS, dtype=jnp.int32)[None, :]).astype(jnp.int32)
    csum = jnp.cumsum(onehot, axis=0)
    rank = jnp.sum(onehot * csum, axis=1) - 1
    counts = csum[-1]
    padded = (counts + tm - 1) // tm * tm
    pend = jnp.cumsum(padded)
    pstart = pend - padded
    dest = (jnp.sum(onehot * pstart[None, :], axis=1) + rank).astype(jnp.int32)
    step = max(tm, SLOT_ALIGN)
    cap = -(-(2 * n + N_EXPERTS * (tm - 1)) // step) * step
    n_blocks = cap // tm
    token = jnp.arange(2 * n, dtype=jnp.int32) // 2
    slot_token = jnp.zeros((n_blocks * tm,), jnp.int32).at[dest].set(token, unique_indices=True)
    block_expert = jnp.minimum(
        jnp.searchsorted(pend, jnp.arange(n_blocks, dtype=jnp.int32) * tm, side="right"), N_EXPERTS - 1
    ).astype(jnp.int32)
    n_valid = (pend[-1:] // tm).astype(jnp.int32)
    return slot_token, dest, block_expert, n_valid


def _run(x, caches, wts, *, tq_ab, tq_c, expert_rows):
    b, t, d = x.shape
    n = b * t
    xf = x.reshape(n, d)
    (qsb, ksb, vsb, qfx, kfx, vfx, ksb16, vsb16, kfx16, vfx16, logf) = _proj_ab(
        xf, wts["g_ab"], wts["w_in_ab"], wts["w_f"], wts["b_f"])
    r3 = lambda a: a.reshape(b, t, -1)
    logf3 = r3(logf)
    if caches is None:
        c = _cumsum_time(logf3.transpose(0, 2, 1)).transpose(0, 2, 1)
        c_q, c_p = c, c
        kp_sb, vp_sb, kp_fx, vp_fx = r3(ksb16), r3(vsb16), r3(kfx16), r3(vfx16)
        td = tk = min(ATTN_TILE, t)
        base, per_q = 0, 1
    else:
        p_sb_k, p_sb_v, p_fx_k, p_fx_v, p_logf = caches[:5]
        past = p_logf.shape[1]
        pad = (-(past + t)) % LANES
        allf = jnp.concatenate([p_logf.astype(F32), logf3, jnp.zeros((b, pad, N_FOX_HEADS), F32)], axis=1)
        c = _cumsum_time(allf.transpose(0, 2, 1)).transpose(0, 2, 1)
        c_q, c_p = c[:, past:past + t], c[:, :past]
        kp_sb, vp_sb, kp_fx, vp_fx = (a.reshape(b, past, -1) for a in (p_sb_k, p_sb_v, p_fx_k, p_fx_v))
        td, tk = t, min(SAMPLE_ATTN_TILE, past)
        base, per_q = past // tk, 0
    kw = dict(tq=tq_ab, td=td, tk=tk, base=base, per_q=per_q)
    o_sb = _attn_ab(_sb_kernel, r3(qsb), r3(ksb16), r3(vsb16), kp_sb, vp_sb, [], [], name="attn_sb", **kw)
    extra, extra_specs = _fox_bias_operands(c_q, c_p, tq_ab, td, tk)
    o_fx = _attn_ab(_fox_kernel, r3(qfx), r3(kfx16), r3(vfx16), kp_fx, vp_fx, extra, extra_specs,
                    name="attn_fox", **kw)
    x1 = _ffn_dense(xf, o_sb.reshape(n, SB_W), o_fx.reshape(n, FOX_W), wts["w_out_ab"], wts["g_ffn"],
                    wts["w_gate"], wts["w_up"], wts["w_down"])

    qc, kc, vc, kc16, vc16 = _proj_c(x1, wts["g_c"], wts["w_in_c"])
    if caches is None:
        kprev, vprev, pos0 = r3(kc16), r3(vc16), 0
        new_k, new_v = r3(kc)[:, t - SWA_CACHE:], r3(vc)[:, t - SWA_CACHE:]
    else:
        p_k, p_v = (a.reshape(b, SWA_CACHE, SWA_KV_W) for a in caches[5:])
        kprev, vprev, pos0 = p_k, p_v, caches[0].shape[1]
        new_k = jnp.concatenate([p_k, r3(kc)], axis=1)[:, t:]
        new_v = jnp.concatenate([p_v, r3(vc)], axis=1)[:, t:]
    o_c = _swa(wts["sinks"], r3(qc), kprev, vprev, r3(kc16), r3(vc16), tq=tq_c, pos0=pos0,
               prev_is_cache=caches is not None)

    x2, xn, route = _route(x1, o_c.reshape(n, SWA_Q_W), wts["w_out_c"], wts["g_moe"], wts["w_router"])
    slot_token, dest, block_expert, n_valid = _dispatch(route, expert_rows)
    heads = lambda a, h: a.reshape(1, b, -1, h, HEAD_DIM)
    states = (heads(ksb, N_SB_HEADS), heads(vsb, N_SB_HEADS), heads(kfx, N_FOX_HEADS), heads(vfx, N_FOX_HEADS),
              logf3[None], heads(new_k, N_SWA_KV_HEADS), heads(new_v, N_SWA_KV_HEADS))
    n_chunks = d // GATHER_CHUNK
    pair_rows = dest.reshape(n, 2).T.reshape(-1)
    return dict(xn=xn, slot_idx=_gather_indices(slot_token, n_chunks, n),
                pair_idx=_gather_indices(pair_rows, n_chunks, slot_token.shape[0]),
                block_expert=block_expert, n_valid=n_valid, x2=x2, route=route, states=states, shape=(b, t, d),
                expert_rows=expert_rows)


def _moe_experts(ctx, wts):
    x_slots = _sc_gather(ctx["xn"], ctx["slot_idx"])
    y_slots = _experts(ctx["block_expert"], ctx["n_valid"], x_slots, wts["w_gate_moe"], wts["w_up_moe"],
                       wts["w_down_moe"], tm=ctx["expert_rows"],
                       tf=EXPERT_COLS if ctx["expert_rows"] == EXPERT_ROWS else SAMPLE_EXPERT_COLS)
    return dict(ctx, y_slots=y_slots)


def _moe_combine(ctx, wts):
    y_pairs = _sc_gather(ctx["y_slots"], ctx["pair_idx"])
    y = _combine(y_pairs, ctx["x2"], ctx["route"], wts["g_final"])
    return y.reshape(ctx["shape"]), ctx["states"]


def _issue_after(first, later):
    return lax.optimization_barrier((first, later))


def kernel(x_prompt, x_sample, cache_sb_k, cache_sb_v, cache_fox_k, cache_fox_v, cache_fox_logf, cache_swa_k,
           cache_swa_v, norm_mix_ab, w_in_ab, b_forget, w_out_ab, norm_ffn_dense, w_gate_dense, w_up_dense,
           w_down_dense, norm_mix_c, w_in_c, sinks, w_out_c, norm_ffn_moe, w_router, w_gate_moe, w_up_moe,
           w_down_moe, norm_final):
    d = x_prompt.shape[-1]
    main_w = 3 * SB_W + 3 * FOX_W
    w_f = jnp.zeros((d, LANES), F32).at[:, :N_FOX_HEADS].set(w_in_ab[0][:, main_w:])
    wr = jnp.zeros((d, LANES), F32).at[:, :N_EXPERTS].set(w_router[0])
    wr_hi = wr.astype(BF16)
    wr_mid = (wr - wr_hi.astype(F32)).astype(BF16)
    wts = dict(
        g_ab=norm_mix_ab[0][None], w_in_ab=w_in_ab[0][:, :main_w].astype(BF16), w_f=w_f.astype(BF16),
        b_f=b_forget[0][None], w_out_ab=w_out_ab[0].astype(BF16), g_ffn=norm_ffn_dense[0][None],
        w_gate=w_gate_dense[0].astype(BF16), w_up=w_up_dense[0].astype(BF16), w_down=w_down_dense[0].astype(BF16),
        g_c=norm_mix_c[0][None], w_in_c=w_in_c[0].astype(BF16), sinks=sinks[0], w_out_c=w_out_c[0].astype(BF16),
        g_moe=norm_ffn_moe[0][None], w_router=jnp.concatenate([wr_hi, wr_mid], axis=0),
        w_gate_moe=w_gate_moe[0], w_up_moe=w_up_moe[0], w_down_moe=w_down_moe[0], g_final=norm_final[None])
    t_p, t_s = x_prompt.shape[1], x_sample.shape[1]
    flat = lambda a: a[0].reshape(a.shape[1], a.shape[2], -1)
    caches = (flat(cache_sb_k), flat(cache_sb_v), flat(cache_fox_k), flat(cache_fox_v), cache_fox_logf[0],
              flat(cache_swa_k), flat(cache_swa_v))
    caches, x_prompt = _issue_after(caches, x_prompt)
    prompt = _run(x_prompt, None, wts, tq_ab=min(ATTN_ROWS, t_p), tq_c=min(SWA_TILE, t_p), expert_rows=EXPERT_ROWS)
    prompt["slot_idx"], x_sample = _issue_after(prompt["slot_idx"], x_sample)
    sample = _run(x_sample, caches, wts, tq_ab=t_s, tq_c=t_s, expert_rows=SAMPLE_EXPERT_ROWS)
    prompt = _moe_experts(prompt, wts)
    prompt["y_slots"], sample["xn"] = _issue_after(prompt["y_slots"], sample["xn"])
    sample = _moe_experts(sample, wts)
    y_p, st_p = _moe_combine(prompt, wts)
    y_s, st_s = _moe_combine(sample, wts)
    return (y_p, y_s) + st_p + st_s
```

```python
import functools

import jax
import jax.numpy as jnp
from jax import lax
from jax.experimental import pallas as pl
from jax.experimental.pallas import tpu as pltpu
from jax.experimental.pallas import tpu_sc as plsc

F32 = jnp.float32
BF16 = jnp.bfloat16

HEAD_DIM = 64
LANES = 128
N_SB_HEADS = 8
N_FOX_HEADS = 8
N_SWA_HEADS = 16
N_SWA_KV_HEADS = 2
SWA_GROUP = N_SWA_HEADS // N_SWA_KV_HEADS
CHUNK = 64
WINDOW_CHUNKS = 2
SWA_CACHE = 128
N_EXPERTS = 8
RMS_EPS = 1e-6
NEG_INF = -1e30
ATTN_SCALE = HEAD_DIM ** -0.5
SB_W = N_SB_HEADS * HEAD_DIM
FOX_W = N_FOX_HEADS * HEAD_DIM
SWA_Q_W = N_SWA_HEADS * HEAD_DIM
SWA_KV_W = N_SWA_KV_HEADS * HEAD_DIM

VMEM_LIMIT_BYTES = 56 * 1024 * 1024
ROW_TILE = 512
ATTN_TILE = 512
ATTN_ROWS = 512
SAMPLE_ATTN_TILE = 2048
CUMSUM_SUB = 256
SWA_TILE = 256
SAMPLE_EXPERT_ROWS = 256
LOG2_E = 1.4426950408889634
EXPERT_ROWS = 1024
EXPERT_COLS = 512
SAMPLE_EXPERT_COLS = 896
GATHER_CHUNK = 256
SC_WINDOW = 128
SLOT_ALIGN = 1024
NT_DIMS = (((1,), (1,)), ((), ()))


def _params(sem):
    return pltpu.CompilerParams(dimension_semantics=sem, vmem_limit_bytes=VMEM_LIMIT_BYTES)


def _resident(shape):
    n = len(shape)
    return pl.BlockSpec(shape, lambda *_: (0,) * n, pipeline_mode=pl.Buffered(1))


def _rms(x, g):
    return x * lax.rsqrt(jnp.mean(x * x, axis=-1, keepdims=True) + RMS_EPS) * g


def _log_sigmoid(z):
    return jnp.minimum(z, 0.0) - jnp.log(1.0 + jnp.exp(-jnp.abs(z)))


def _silu(a):
    return a * (1.0 / (1.0 + jnp.exp(-a)))


def _dot(a, b):
    return jnp.dot(a, b, preferred_element_type=F32)


def _dot_nt(a, b):
    return lax.dot_general(a, b, NT_DIMS, preferred_element_type=F32)


def _split3(x):
    hi = x.astype(BF16)
    r = x - hi.astype(F32)
    mid = r.astype(BF16)
    lo = (r - mid.astype(F32)).astype(BF16)
    return hi, mid, lo


def _proj_ab_kernel(x_ref, g_ref, w_ref, wf_ref, bf_ref,
                    qsb_ref, ksb_ref, vsb_ref, qfx_ref, kfx_ref, vfx_ref,
                    ksb16_ref, vsb16_ref, kfx16_ref, vfx16_ref, logf_ref):
    hn = _rms(x_ref[...], g_ref[...]).astype(BF16)
    f32_outs = (None, ksb_ref, vsb_ref, None, kfx_ref, vfx_ref)
    b16_outs = (qsb_ref, ksb16_ref, vsb16_ref, qfx_ref, kfx16_ref, vfx16_ref)
    for c in range(6):
        p = _dot(hn, w_ref[:, c * SB_W:(c + 1) * SB_W])
        if f32_outs[c] is None:
            b16_outs[c][...] = (p * ATTN_SCALE).astype(BF16)
        else:
            f32_outs[c][...] = p.reshape(p.shape[0], SB_W // HEAD_DIM, HEAD_DIM)
            b16_outs[c][...] = p.astype(BF16)
    f = _dot(hn, wf_ref[...])[:, :N_FOX_HEADS] + bf_ref[...]
    logf_ref[...] = _log_sigmoid(f)


def _proj_ab(x, g, w, wf, bf):
    n, d = x.shape
    tm = min(ROW_TILE, n)
    row = lambda c: pl.BlockSpec((tm, c), lambda i: (i, 0))
    f32o = jax.ShapeDtypeStruct((n, SB_W // HEAD_DIM, HEAD_DIM), F32)
    b16o = jax.ShapeDtypeStruct((n, SB_W), BF16)
    st = pl.BlockSpec((tm, SB_W // HEAD_DIM, HEAD_DIM), lambda i: (i, 0, 0))
    w16 = row(SB_W)
    return pl.pallas_call(
        _proj_ab_kernel,
        grid=(n // tm,),
        in_specs=[row(d), _resident((1, d)), _resident(w.shape), _resident(wf.shape), _resident((1, N_FOX_HEADS))],
        out_specs=[w16, st, st, w16, st, st, w16, w16, w16, w16, row(N_FOX_HEADS)],
        out_shape=[b16o, f32o, f32o, b16o, f32o, f32o, b16o, b16o, b16o, b16o,
                   jax.ShapeDtypeStruct((n, N_FOX_HEADS), F32)],
        compiler_params=_params(("parallel",)),
        name="proj_ab",
    )(x, g, w, wf, bf)


def _cumsum_kernel(x_ref, o_ref):
    t = x_ref.shape[-1]
    r = lax.broadcasted_iota(jnp.int32, (LANES, LANES), 0)
    c = lax.broadcasted_iota(jnp.int32, (LANES, LANES), 1)
    tri = (r <= c).astype(BF16)
    carry = jnp.zeros((x_ref.shape[1], 1), F32)
    for i in range(t // LANES):
        hi, mid, lo = _split3(x_ref[0, :, i * LANES:(i + 1) * LANES])
        cs = _dot(hi, tri) + _dot(mid, tri) + _dot(lo, tri) + carry
        o_ref[0, :, i * LANES:(i + 1) * LANES] = cs
        carry = cs[:, LANES - 1:LANES]


def _cumsum_time(x):
    b, h, t = x.shape
    spec = pl.BlockSpec((1, h, t), lambda i: (i, 0, 0))
    return pl.pallas_call(
        _cumsum_kernel, grid=(b,), in_specs=[spec], out_specs=spec,
        out_shape=jax.ShapeDtypeStruct(x.shape, F32),
        compiler_params=_params(("parallel",)), name="cumsum_time",
    )(x)


def _stacked_heads(q):
    lane = lax.broadcasted_iota(jnp.int32, q.shape, 1)
    zero = jnp.zeros_like(q)
    return jnp.concatenate([jnp.where(lane < HEAD_DIM, q, zero), jnp.where(lane >= HEAD_DIM, q, zero)], axis=0)


def _unstack_heads(acc):
    tq = acc.shape[0] // 2
    lane = lax.broadcasted_iota(jnp.int32, (tq, LANES), 1)
    return jnp.where(lane < HEAD_DIM, acc[:tq], acc[tq:])


def _causal_mask(tq, td, first_row, strict):
    row = first_row + lax.broadcasted_iota(jnp.int32, (2 * tq, td), 0) % tq
    col = lax.broadcasted_iota(jnp.int32, (2 * tq, td), 1)
    return col < row if strict else col <= row


def _strict_upper(n):
    j = lax.broadcasted_iota(jnp.int32, (n, n), 0)
    s = lax.broadcasted_iota(jnp.int32, (n, n), 1)
    return (j > s).astype(BF16)


def _sb_block(qk, v, total, acc_ref, u, vis):
    sub = u.shape[0]
    z = qk * LOG2_E
    neg_abs = lax.bitcast_convert_type(lax.bitcast_convert_type(z, jnp.uint32) | jnp.uint32(0x80000000), F32)
    ls = jnp.minimum(z, 0.0) - jnp.log2(1.0 + jnp.exp2(neg_abs))
    lk = ls - z
    if vis is not None:
        lk = jnp.where(vis, lk, 0.0)
    lk = lk.astype(BF16)
    for s in reversed(range(qk.shape[1] // sub)):
        cols = slice(s * sub, (s + 1) * sub)
        after = _dot(lk[:, cols], u)
        w = jnp.exp2(ls[:, cols] + after + total)
        if vis is not None:
            w = jnp.where(vis[:, cols], w, 0.0)
        acc_ref[...] += _dot(w.astype(BF16), v[cols, :])
        total = total + after[:, 0:1] + lk[:, s * sub:s * sub + 1].astype(F32)
    return total


def _sb_kernel(q_ref, kd_ref, vd_ref, kp_ref, vp_ref, o_ref, acc_ref, *, tk, base, per_q):
    tq, td = q_ref.shape[1], kd_ref.shape[1]
    qs = _stacked_heads(q_ref[0])
    first_row = (pl.program_id(2) % (td // tq)) * tq
    vis = _causal_mask(tq, td, first_row, strict=True)
    u_diag = _strict_upper(min(CUMSUM_SUB, td))
    u_past = _strict_upper(CUMSUM_SUB)
    acc_ref[...] = jnp.zeros_like(acc_ref)
    total = _sb_block(_dot_nt(qs, kd_ref[0]), vd_ref[0], jnp.zeros((2 * tq, 1), F32), acc_ref, u_diag, vis)

    n_past = base + pl.program_id(2) // (td // tq) * per_q

    def body(jj, total):
        start = pl.multiple_of((n_past - 1 - jj) * tk, tk)
        k = kp_ref[0, pl.ds(start, tk), :].astype(BF16)
        v = vp_ref[0, pl.ds(start, tk), :].astype(BF16)
        return _sb_block(_dot_nt(qs, k), v, total, acc_ref, u_past, None)

    lax.fori_loop(0, n_past, body, total)
    o_ref[0] = _unstack_heads(acc_ref[...]).astype(o_ref.dtype)


def _fox_block(qk, v, ck, m, l, acc_ref, vis):
    tq = qk.shape[0] // 2
    s = jnp.concatenate([qk[:tq] - ck[0:1, :], qk[tq:] - ck[1:2, :]], axis=0)
    if vis is not None:
        s = jnp.where(vis, s, NEG_INF)
    m_new = jnp.maximum(m, jnp.max(s, axis=1, keepdims=True))
    alpha = jnp.exp(m - m_new)
    p = jnp.exp(s - m_new)
    acc_ref[...] = alpha * acc_ref[...] + _dot(p.astype(BF16), v)
    return m_new, alpha * l + jnp.sum(p, axis=1, keepdims=True)


def _fox_kernel(q_ref, kd_ref, vd_ref, kp_ref, vp_ref, ckd_ref, ckp_ref, o_ref, acc_ref, *, tk, base, per_q):
    tq, td = q_ref.shape[1], kd_ref.shape[1]
    qs = _stacked_heads(q_ref[0])
    first_row = (pl.program_id(2) % (td // tq)) * tq
    vis = _causal_mask(tq, td, first_row, strict=False)
    acc_ref[...] = jnp.zeros_like(acc_ref)
    m0 = jnp.full((2 * tq, 1), NEG_INF, F32)
    l0 = jnp.zeros((2 * tq, 1), F32)
    stats = _fox_block(_dot_nt(qs, kd_ref[0]), vd_ref[0], ckd_ref[0, 0], m0, l0, acc_ref, vis)

    n_past = base + pl.program_id(2) // (td // tq) * per_q

    def body(j, stats):
        start = pl.multiple_of(j * tk, tk)
        k = kp_ref[0, pl.ds(start, tk), :].astype(BF16)
        v = vp_ref[0, pl.ds(start, tk), :].astype(BF16)
        return _fox_block(_dot_nt(qs, k), v, ckp_ref[0, 0, j], stats[0], stats[1], acc_ref, None)

    _, l = lax.fori_loop(0, n_past, body, stats)
    o_ref[0] = _unstack_heads(acc_ref[...] / l).astype(o_ref.dtype)


def _attn_ab(kernel_fn, q, kd, vd, kp, vp, extra, extra_specs, *, tq, td, tk, base, per_q, name):
    b, t, w = q.shape
    tp = kp.shape[1]
    grid = (b, w // LANES, t // tq)
    qspec = pl.BlockSpec((1, tq, LANES), lambda bi, hp, i: (bi, i, hp))
    dspec = pl.BlockSpec((1, td, LANES), lambda bi, hp, i: (bi, i // (td // tq), hp))
    pspec = pl.BlockSpec((1, tp, LANES), lambda bi, hp, i: (bi, 0, hp))
    return pl.pallas_call(
        functools.partial(kernel_fn, tk=tk, base=base, per_q=per_q),
        grid=grid,
        in_specs=[qspec, dspec, dspec, pspec, pspec] + extra_specs,
        out_specs=qspec,
        out_shape=jax.ShapeDtypeStruct(q.shape, BF16),
        scratch_shapes=[pltpu.VMEM((2 * tq, LANES), F32)],
        compiler_params=_params(("parallel", "parallel", "arbitrary")),
        name=name,
    )(q, kd, vd, kp, vp, *extra)


def _fox_bias_operands(c_q, c_p, tq, td, tk):
    b, t, h = c_q.shape
    tp = c_p.shape[1]
    hp = h // 2
    ckd = c_q.reshape(b, t, hp, 2).transpose(0, 2, 3, 1)
    ckp = c_p.reshape(b, tp // tk, tk, hp, 2).transpose(0, 3, 1, 4, 2)
    specs = [pl.BlockSpec((1, 1, 2, td), lambda bi, p, i: (bi, p, 0, i // (td // tq))),
             pl.BlockSpec((1, 1, tp // tk, 2, tk), lambda bi, p, i: (bi, p, 0, 0, 0))]
    return [ckd, ckp], specs


def _ffn_dense_kernel(x_ref, osb_ref, ofx_ref, wo_ref, g_ref, wg_ref, wu_ref, wd_ref, o_ref, *, tf):
    o_ref[...] = x_ref[...] + _dot(osb_ref[...], wo_ref[:SB_W, :]) + _dot(ofx_ref[...], wo_ref[SB_W:, :])
    hn = _rms(o_ref[...], g_ref[...]).astype(BF16)
    for c in range(wg_ref.shape[1] // tf):
        a = _dot(hn, wg_ref[:, c * tf:(c + 1) * tf])
        u = _dot(hn, wu_ref[:, c * tf:(c + 1) * tf])
        o_ref[...] += _dot((_silu(a) * u).astype(BF16), wd_ref[c * tf:(c + 1) * tf, :])


def _ffn_dense(x, osb, ofx, wo, g, wg, wu, wd):
    n, d = x.shape
    tm = min(ROW_TILE, n)
    row = lambda c: pl.BlockSpec((tm, c), lambda i: (i, 0))
    return pl.pallas_call(
        functools.partial(_ffn_dense_kernel, tf=256),
        grid=(n // tm,),
        in_specs=[row(d), row(SB_W), row(FOX_W), _resident(wo.shape), _resident((1, d)),
                  _resident(wg.shape), _resident(wu.shape), _resident(wd.shape)],
        out_specs=row(d),
        out_shape=jax.ShapeDtypeStruct((n, d), F32),
        compiler_params=_params(("parallel",)),
        name="ffn_dense",
    )(x, osb, ofx, wo, g, wg, wu, wd)


def _proj_c_kernel(x_ref, g_ref, w_ref, q_ref, k_ref, v_ref, k16_ref, v16_ref):
    hn = _rms(x_ref[...], g_ref[...]).astype(BF16)
    q_ref[...] = (_dot(hn, w_ref[:, :SWA_Q_W]) * ATTN_SCALE).astype(BF16)
    k = _dot(hn, w_ref[:, SWA_Q_W:SWA_Q_W + SWA_KV_W])
    v = _dot(hn, w_ref[:, SWA_Q_W + SWA_KV_W:])
    k_ref[...] = k
    v_ref[...] = v
    k16_ref[...] = k.astype(BF16)
    v16_ref[...] = v.astype(BF16)


def _proj_c(x, g, w):
    n, d = x.shape
    tm = min(ROW_TILE, n)
    row = lambda c: pl.BlockSpec((tm, c), lambda i: (i, 0))
    kv32 = jax.ShapeDtypeStruct((n, SWA_KV_W), F32)
    kv16 = jax.ShapeDtypeStruct((n, SWA_KV_W), BF16)
    return pl.pallas_call(
        _proj_c_kernel,
        grid=(n // tm,),
        in_specs=[row(d), _resident((1, d)), _resident(w.shape)],
        out_specs=[row(SWA_Q_W)] + [row(SWA_KV_W)] * 4,
        out_shape=[jax.ShapeDtypeStruct((n, SWA_Q_W), BF16), kv32, kv32, kv16, kv16],
        compiler_params=_params(("parallel",)),
        name="proj_c",
    )(x, g, w)


def _swap_halves(a):
    return jnp.concatenate([a[:, HEAD_DIM:], a[:, :HEAD_DIM]], axis=1)


def _swa_kernel(sink_ref, q_ref, kprev_ref, vprev_ref, kcur_ref, vcur_ref, o_ref, *, pos0):
    tq = q_ref.shape[1]
    band = (WINDOW_CHUNKS + 1) * CHUNK
    stack = SWA_GROUP // 2
    rows = stack * CHUNK
    first = pos0 + pl.program_id(1) * tq
    kw = jnp.concatenate([kprev_ref[0].astype(BF16), kcur_ref[0].astype(BF16)], axis=0)
    vw = jnp.concatenate([vprev_ref[0].astype(BF16), vcur_ref[0].astype(BF16)], axis=0)
    k_by_half = (kw, _swap_halves(kw))
    v_by_half = (vw, _swap_halves(vw))
    q = q_ref[0]
    in_low = lax.broadcasted_iota(jnp.int32, q.shape, 1) % LANES < HEAD_DIM
    q_by_half = (jnp.where(in_low, q, jnp.zeros_like(q)), jnp.where(in_low, jnp.zeros_like(q), q))
    r_id = lax.broadcasted_iota(jnp.int32, (rows, band), 0)
    c_id = lax.broadcasted_iota(jnp.int32, (rows, band), 1)
    dist = jnp.abs(WINDOW_CHUNKS * CHUNK + r_id % CHUNK - c_id).astype(F32)
    head_in_stack = lax.broadcasted_iota(jnp.int32, (rows, 1), 0) // CHUNK
    col = lax.broadcasted_iota(jnp.int32, (1, band), 1)
    lane = lax.broadcasted_iota(jnp.int32, (CHUNK, LANES), 1)

    def per_head(values):
        out = jnp.full((rows, 1), values[-1], F32)
        for i in range(stack - 2, -1, -1):
            out = jnp.where(head_in_stack == i, values[i], out)
        return out

    heads = {(g, half): [g * SWA_GROUP + 2 * i + half for i in range(stack)]
             for g in range(N_SWA_KV_HEADS) for half in range(2)}
    bias = {key: per_head([2.0 ** (-8.0 * (h + 1) / N_SWA_HEADS) for h in hs]) * dist for key, hs in heads.items()}
    sink = {key: per_head([sink_ref[h] for h in hs]) for key, hs in heads.items()}

    for c in range(tq // CHUNK):
        in_range = first - WINDOW_CHUNKS * CHUNK + c * CHUNK + col >= 0
        win = slice(c * CHUNK, c * CHUNK + band)
        outs = {}
        for (g, half), hs in heads.items():
            qs = jnp.concatenate([q_by_half[half][c * CHUNK:(c + 1) * CHUNK, (h // 2) * LANES:(h // 2 + 1) * LANES]
                                  for h in hs], axis=0)
            s = _dot_nt(qs, k_by_half[g ^ half][win]) - bias[g, half]
            s = jnp.where(in_range, s, NEG_INF)
            m = jnp.maximum(jnp.max(s, axis=1, keepdims=True), sink[g, half])
            p = jnp.exp(s - m)
            l = jnp.sum(p, axis=1, keepdims=True) + jnp.exp(sink[g, half] - m)
            outs[g, half] = _dot(p.astype(BF16), v_by_half[g ^ half][win]) / l
        for g in range(N_SWA_KV_HEADS):
            for i in range(stack):
                pair = (g * SWA_GROUP + 2 * i) // 2
                o_ref[0, c * CHUNK:(c + 1) * CHUNK, pair * LANES:(pair + 1) * LANES] = jnp.where(
                    lane < HEAD_DIM, outs[g, 0][i * CHUNK:(i + 1) * CHUNK],
                    outs[g, 1][i * CHUNK:(i + 1) * CHUNK]).astype(o_ref.dtype)


def _swa(sinks, q, kprev, vprev, kcur, vcur, *, tq, pos0, prev_is_cache):
    b, t, _ = q.shape
    per = tq // SWA_CACHE
    prev_map = ((lambda bi, i, s: (bi, 0, 0)) if prev_is_cache
                else (lambda bi, i, s: (bi, jnp.maximum(i * per - 1, 0), 0)))
    cur = lambda w: pl.BlockSpec((1, tq, w), lambda bi, i, s: (bi, i, 0))
    prev = pl.BlockSpec((1, SWA_CACHE, SWA_KV_W), prev_map)
    return pl.pallas_call(
        functools.partial(_swa_kernel, pos0=pos0),
        grid_spec=pltpu.PrefetchScalarGridSpec(
            num_scalar_prefetch=1, grid=(b, t // tq),
            in_specs=[cur(SWA_Q_W), prev, prev, cur(SWA_KV_W), cur(SWA_KV_W)],
            out_specs=cur(SWA_Q_W)),
        out_shape=jax.ShapeDtypeStruct(q.shape, BF16),
        compiler_params=_params(("parallel", "arbitrary")),
        name="swa",
    )(sinks, q, kprev, vprev, kcur, vcur)


def _route_kernel(x_ref, o_ref, wo_ref, g_ref, wr_ref, x2_ref, xn_ref, route_ref):
    x2 = x_ref[...] + _dot(o_ref[...], wo_ref[...])
    xn = _rms(x2, g_ref[...])
    x2_ref[...] = x2
    _store_chunked(xn_ref, xn)
    hi = xn.astype(BF16)
    mid = (xn - hi.astype(F32)).astype(BF16)
    d = xn.shape[1]
    logits = _dot(hi, wr_ref[:d, :]) + _dot(hi, wr_ref[d:, :]) + _dot(mid, wr_ref[:d, :])
    lane = lax.broadcasted_iota(jnp.int32, logits.shape, 1).astype(F32)
    lg = jnp.where(lane < N_EXPERTS, logits, -jnp.inf)
    m1 = jnp.max(lg, axis=1, keepdims=True)
    i1 = jnp.min(jnp.where(lg == m1, lane, float(LANES)), axis=1, keepdims=True)
    lg2 = jnp.where(lane == i1, -jnp.inf, lg)
    m2 = jnp.max(lg2, axis=1, keepdims=True)
    i2 = jnp.min(jnp.where(lg2 == m2, lane, float(LANES)), axis=1, keepdims=True)
    e = jnp.exp(m2 - m1)
    g1 = 1.0 / (1.0 + e)
    g2 = e / (1.0 + e)
    route = jnp.where(lane == 0, i1, jnp.where(lane == 1, i2, jnp.where(lane == 2, g1, jnp.where(lane == 3, g2, 0.0))))
    route_ref[...] = route


def _route(x, o, wo, g, wr):
    n, d = x.shape
    tm = min(ROW_TILE, n)
    row = lambda c: pl.BlockSpec((tm, c), lambda i: (i, 0))
    xs = jax.ShapeDtypeStruct((n, d), F32)
    return pl.pallas_call(
        _route_kernel,
        grid=(n // tm,),
        in_specs=[row(d), row(SWA_Q_W), _resident(wo.shape), _resident((1, d)), _resident(wr.shape)],
        out_specs=[row(d), _chunked_spec(tm, d), row(LANES)],
        out_shape=[xs, _chunked_shape(n, d), jax.ShapeDtypeStruct((n, LANES), F32)],
        compiler_params=_params(("parallel",)),
        name="route",
    )(x, o, wo, g, wr)


def _chunked_shape(rows, d):
    return jax.ShapeDtypeStruct((d // GATHER_CHUNK, rows, GATHER_CHUNK), F32)


def _chunked_spec(tm, d, index_map=lambda i: (0, i, 0)):
    return pl.BlockSpec((d // GATHER_CHUNK, tm, GATHER_CHUNK), index_map)


def _store_chunked(ref, x):
    for j in range(x.shape[1] // GATHER_CHUNK):
        ref[j] = x[:, j * GATHER_CHUNK:(j + 1) * GATHER_CHUNK]


def _load_chunked(ref):
    return jnp.concatenate([ref[j] for j in range(ref.shape[0])], axis=1)


def _sc_gather_flat(flat, idx):
    m, w = idx.shape[1], flat.shape[1]
    mesh = plsc.VectorSubcoreMesh(core_axis_name="core", subcore_axis_name="subcore")

    @pl.kernel(out_type=jax.ShapeDtypeStruct((m, w), flat.dtype), mesh=mesh)
    def gather(x_hbm, i_hbm, o_hbm):
        def body(i_vmem, o_vmem):
            pltpu.sync_copy(x_hbm.at[i_vmem.at[0]], o_vmem)

        pltpu.emit_pipeline(
            body,
            grid=(m // SC_WINDOW,),
            in_specs=[pl.BlockSpec((1, SC_WINDOW), index_map=lambda i: (0, i))],
            out_specs=[pl.BlockSpec((SC_WINDOW, w), index_map=lambda i: (i, 0))],
            core_axis_name=("core", "subcore"),
            dimension_semantics=(pltpu.PARALLEL,),
        )(i_hbm, o_hbm)

    return gather(flat, idx)


def _gather_indices(rows, n_chunks, src_rows):
    return jnp.concatenate([rows + j * src_rows for j in range(n_chunks)]).reshape(1, -1)


def _sc_gather(src, idx):
    n_chunks, _, w = src.shape
    out = _sc_gather_flat(src.reshape(-1, w), idx)
    return out.reshape(n_chunks, idx.shape[1] // n_chunks, w)


def _expert_kernel(be_ref, nv_ref, x_ref, wg_ref, wu_ref, wd_ref, o_ref, xb_ref, acc_ref):
    blk, f = pl.program_id(0), pl.program_id(1)
    last = pl.num_programs(1) - 1
    valid = blk < nv_ref[0]

    @pl.when(valid & (f == 0))
    def _():
        for j in range(x_ref.shape[0]):
            xb_ref[:, j * GATHER_CHUNK:(j + 1) * GATHER_CHUNK] = x_ref[j].astype(BF16)
        acc_ref[...] = jnp.zeros_like(acc_ref)

    @pl.when(valid)
    def _():
        xb = xb_ref[...]
        h = (_silu(_dot(xb, wg_ref[0].astype(BF16))) * _dot(xb, wu_ref[0].astype(BF16))).astype(BF16)
        acc_ref[...] += _dot(h, wd_ref[0].astype(BF16))

    @pl.when(valid & (f == last))
    def _():
        _store_chunked(o_ref, acc_ref[...])

    @pl.when(jnp.logical_not(valid) & (f == last))
    def _():
        o_ref[...] = jnp.zeros_like(o_ref)


def _experts(block_expert, n_valid, x_slots, wg, wu, wd, *, tm, tf):
    d = wg.shape[1]
    cap = x_slots.shape[1]
    nf = wg.shape[2] // tf
    fsel = lambda b, f, nv: jnp.where(b < nv[0], f, nf - 1)
    slots = _chunked_spec(tm, d, lambda b, f, be, nv: (0, b, 0))
    return pl.pallas_call(
        _expert_kernel,
        grid_spec=pltpu.PrefetchScalarGridSpec(
            num_scalar_prefetch=2, grid=(cap // tm, nf),
            in_specs=[slots,
                      pl.BlockSpec((1, d, tf), lambda b, f, be, nv: (be[b], 0, fsel(b, f, nv))),
                      pl.BlockSpec((1, d, tf), lambda b, f, be, nv: (be[b], 0, fsel(b, f, nv))),
                      pl.BlockSpec((1, tf, d), lambda b, f, be, nv: (be[b], fsel(b, f, nv), 0))],
            out_specs=slots,
            scratch_shapes=[pltpu.VMEM((tm, d), BF16), pltpu.VMEM((tm, d), F32)]),
        out_shape=_chunked_shape(cap, d),
        compiler_params=_params(("arbitrary", "arbitrary")),
        name="moe_experts",
    )(block_expert, n_valid, x_slots, wg, wu, wd)


def _combine_kernel(y1_ref, y2_ref, x_ref, route_ref, g_ref, o_ref):
    route = route_ref[...]
    y = x_ref[...] + (_load_chunked(y1_ref) * route[:, 2:3] + _load_chunked(y2_ref) * route[:, 3:4])
    o_ref[...] = _rms(y, g_ref[...])


def _combine(y_pairs, x2, route, g):
    n, d = x2.shape
    tm = min(ROW_TILE, n)
    row = lambda c: pl.BlockSpec((tm, c), lambda i: (i, 0))
    second = n // tm
    return pl.pallas_call(
        _combine_kernel,
        grid=(n // tm,),
        in_specs=[_chunked_spec(tm, d), _chunked_spec(tm, d, lambda i: (0, i + second, 0)),
                  row(d), row(LANES), _resident((1, d))],
        out_specs=row(d),
        out_shape=jax.ShapeDtypeStruct((n, d), F32),
        compiler_params=_params(("parallel",)),
        name="moe_combine",
    )(y_pairs, y_pairs, x2, route, g)


def _dispatch(route, tm):
    n = route.shape[0]
    expert = route[:, :2].astype(jnp.int32).reshape(-1)
    onehot = (expert[:, None] == jnp.arange(N_EXPERTS, dtype=jnp.int32)[None, :]).astype(jnp.int32)
    csum = jnp.cumsum(onehot, axis=0)
    rank = jnp.sum(onehot * csum, axis=1) - 1
    counts = csum[-1]
    padded = (counts + tm - 1) // tm * tm
    pend = jnp.cumsum(padded)
    pstart = pend - padded
    dest = (jnp.sum(onehot * pstart[None, :], axis=1) + rank).astype(jnp.int32)
    step = max(tm, SLOT_ALIGN)
    cap = -(-(2 * n + N_EXPERTS * (tm - 1)) // step) * step
    n_blocks = cap // tm
    token = jnp.arange(2 * n, dtype=jnp.int32) // 2
    slot_token = jnp.zeros((n_blocks * tm,), jnp.int32).at[dest].set(token, unique_indices=True)
    block_expert = jnp.minimum(
        jnp.searchsorted(pend, jnp.arange(n_blocks, dtype=jnp.int32) * tm, side="right"), N_EXPERTS - 1
    ).astype(jnp.int32)
    n_valid = (pend[-1:] // tm).astype(jnp.int32)
    return slot_token, dest, block_expert, n_valid


def _run(x, caches, wts, *, tq_ab, tq_c, expert_rows):
    b, t, d = x.shape
    n = b * t
    xf = x.reshape(n, d)
    (qsb, ksb, vsb, qfx, kfx, vfx, ksb16, vsb16, kfx16, vfx16, logf) = _proj_ab(
        xf, wts["g_ab"], wts["w_in_ab"], wts["w_f"], wts["b_f"])
    r3 = lambda a: a.reshape(b, t, -1)
    logf3 = r3(logf)
    if caches is None:
        c = _cumsum_time(logf3.transpose(0, 2, 1)).transpose(0, 2, 1)
        c_q, c_p = c, c
        kp_sb, vp_sb, kp_fx, vp_fx = r3(ksb16), r3(vsb16), r3(kfx16), r3(vfx16)
        td = tk = min(ATTN_TILE, t)
        base, per_q = 0, 1
    else:
        p_sb_k, p_sb_v, p_fx_k, p_fx_v, p_logf = caches[:5]
        past = p_logf.shape[1]
        pad = (-(past + t)) % LANES
        allf = jnp.concatenate([p_logf.astype(F32), logf3, jnp.zeros((b, pad, N_FOX_HEADS), F32)], axis=1)
        c = _cumsum_time(allf.transpose(0, 2, 1)).transpose(0, 2, 1)
        c_q, c_p = c[:, past:past + t], c[:, :past]
        kp_sb, vp_sb, kp_fx, vp_fx = (a.reshape(b, past, -1) for a in (p_sb_k, p_sb_v, p_fx_k, p_fx_v))
        td, tk = t, min(SAMPLE_ATTN_TILE, past)
        base, per_q = past // tk, 0
    kw = dict(tq=tq_ab, td=td, tk=tk, base=base, per_q=per_q)
    o_sb = _attn_ab(_sb_kernel, r3(qsb), r3(ksb16), r3(vsb16), kp_sb, vp_sb, [], [], name="attn_sb", **kw)
    extra, extra_specs = _fox_bias_operands(c_q, c_p, tq_ab, td, tk)
    o_fx = _attn_ab(_fox_kernel, r3(qfx), r3(kfx16), r3(vfx16), kp_fx, vp_fx, extra, extra_specs,
                    name="attn_fox", **kw)
    x1 = _ffn_dense(xf, o_sb.reshape(n, SB_W), o_fx.reshape(n, FOX_W), wts["w_out_ab"], wts["g_ffn"],
                    wts["w_gate"], wts["w_up"], wts["w_down"])

    qc, kc, vc, kc16, vc16 = _proj_c(x1, wts["g_c"], wts["w_in_c"])
    if caches is None:
        kprev, vprev, pos0 = r3(kc16), r3(vc16), 0
        new_k, new_v = r3(kc)[:, t - SWA_CACHE:], r3(vc)[:, t - SWA_CACHE:]
    else:
        p_k, p_v = (a.reshape(b, SWA_CACHE, SWA_KV_W) for a in caches[5:])
        kprev, vprev, pos0 = p_k, p_v, caches[0].shape[1]
        new_k = jnp.concatenate([p_k, r3(kc)], axis=1)[:, t:]
        new_v = jnp.concatenate([p_v, r3(vc)], axis=1)[:, t:]
    o_c = _swa(wts["sinks"], r3(qc), kprev, vprev, r3(kc16), r3(vc16), tq=tq_c, pos0=pos0,
               prev_is_cache=caches is not None)

    x2, xn, route = _route(x1, o_c.reshape(n, SWA_Q_W), wts["w_out_c"], wts["g_moe"], wts["w_router"])
    slot_token, dest, block_expert, n_valid = _dispatch(route, expert_rows)
    heads = lambda a, h: a.reshape(1, b, -1, h, HEAD_DIM)
    states = (heads(ksb, N_SB_HEADS), heads(vsb, N_SB_HEADS), heads(kfx, N_FOX_HEADS), heads(vfx, N_FOX_HEADS),
              logf3[None], heads(new_k, N_SWA_KV_HEADS), heads(new_v, N_SWA_KV_HEADS))
    n_chunks = d // GATHER_CHUNK
    pair_rows = dest.reshape(n, 2).T.reshape(-1)
    return dict(xn=xn, slot_idx=_gather_indices(slot_token, n_chunks, n),
                pair_idx=_gather_indices(pair_rows, n_chunks, slot_token.shape[0]),
                block_expert=block_expert, n_valid=n_valid, x2=x2, route=route, states=states, shape=(b, t, d),
                expert_rows=expert_rows)


def _moe_experts(ctx, wts):
    x_slots = _sc_gather(ctx["xn"], ctx["slot_idx"])
    y_slots = _experts(ctx["block_expert"], ctx["n_valid"], x_slots, wts["w_gate_moe"], wts["w_up_moe"],
                       wts["w_down_moe"], tm=ctx["expert_rows"],
                       tf=EXPERT_COLS if ctx["expert_rows"] == EXPERT_ROWS else SAMPLE_EXPERT_COLS)
    return dict(ctx, y_slots=y_slots)


def _moe_combine(ctx, wts):
    y_pairs = _sc_gather(ctx["y_slots"], ctx["pair_idx"])
    y = _combine(y_pairs, ctx["x2"], ctx["route"], wts["g_final"])
    return y.reshape(ctx["shape"]), ctx["states"]


def _flatten_heads_kernel(x_ref, o_ref):
    rows = x_ref.shape[1]
    o_ref[0] = x_ref[0].reshape(rows, o_ref.shape[2]).astype(o_ref.dtype)


def _flatten_heads(cache):
    b, p, h, e = cache.shape
    tp = min(ATTN_TILE, p)
    return pl.pallas_call(
        _flatten_heads_kernel,
        grid=(b, p // tp),
        in_specs=[pl.BlockSpec((1, tp, h, e), lambda bi, i: (bi, i, 0, 0))],
        out_specs=pl.BlockSpec((1, tp, h * e), lambda bi, i: (bi, i, 0)),
        out_shape=jax.ShapeDtypeStruct((b, p, h * e), BF16),
        compiler_params=_params(("parallel", "parallel")),
        name="flatten_heads",
    )(cache)


def _issue_after(first, later):
    return lax.optimization_barrier((first, later))


def kernel(x_prompt, x_sample, cache_sb_k, cache_sb_v, cache_fox_k, cache_fox_v, cache_fox_logf, cache_swa_k,
           cache_swa_v, norm_mix_ab, w_in_ab, b_forget, w_out_ab, norm_ffn_dense, w_gate_dense, w_up_dense,
           w_down_dense, norm_mix_c, w_in_c, sinks, w_out_c, norm_ffn_moe, w_router, w_gate_moe, w_up_moe,
           w_down_moe, norm_final):
    d = x_prompt.shape[-1]
    main_w = 3 * SB_W + 3 * FOX_W
    w_f = jnp.zeros((d, LANES), F32).at[:, :N_FOX_HEADS].set(w_in_ab[0][:, main_w:])
    wr = jnp.zeros((d, LANES), F32).at[:, :N_EXPERTS].set(w_router[0])
    wr_hi = wr.astype(BF16)
    wr_mid = (wr - wr_hi.astype(F32)).astype(BF16)
    wts = dict(
        g_ab=norm_mix_ab[0][None], w_in_ab=w_in_ab[0][:, :main_w].astype(BF16), w_f=w_f.astype(BF16),
        b_f=b_forget[0][None], w_out_ab=w_out_ab[0].astype(BF16), g_ffn=norm_ffn_dense[0][None],
        w_gate=w_gate_dense[0].astype(BF16), w_up=w_up_dense[0].astype(BF16), w_down=w_down_dense[0].astype(BF16),
        g_c=norm_mix_c[0][None], w_in_c=w_in_c[0].astype(BF16), sinks=sinks[0], w_out_c=w_out_c[0].astype(BF16),
        g_moe=norm_ffn_moe[0][None], w_router=jnp.concatenate([wr_hi, wr_mid], axis=0),
        w_gate_moe=w_gate_moe[0], w_up_moe=w_up_moe[0], w_down_moe=w_down_moe[0], g_final=norm_final[None])
    t_p, t_s = x_prompt.shape[1], x_sample.shape[1]
    flat = lambda a: a[0].reshape(a.shape[1], a.shape[2], -1)
    caches = (_flatten_heads(cache_sb_k[0]), _flatten_heads(cache_sb_v[0]), _flatten_heads(cache_fox_k[0]),
              _flatten_heads(cache_fox_v[0]), cache_fox_logf[0], flat(cache_swa_k), flat(cache_swa_v))
    caches, x_prompt = _issue_after(caches, x_prompt)
    prompt = _run(x_prompt, None, wts, tq_ab=min(ATTN_ROWS, t_p), tq_c=min(SWA_TILE, t_p), expert_rows=EXPERT_ROWS)
    prompt["slot_idx"], x_sample = _issue_after(prompt["slot_idx"], x_sample)
    sample = _run(x_sample, caches, wts, tq_ab=t_s, tq_c=t_s, expert_rows=SAMPLE_EXPERT_ROWS)
    prompt = _moe_experts(prompt, wts)
    prompt["y_slots"], sample["xn"] = _issue_after(prompt["y_slots"], sample["xn"])
    sample = _moe_experts(sample, wts)
    y_p, st_p = _moe_combine(prompt, wts)
    y_s, st_s = _moe_combine(sample, wts)
    return (y_p, y_s) + st_p + st_s
```

```python
import functools

import jax
import jax.numpy as jnp
from jax import lax
from jax.experimental import pallas as pl
from jax.experimental.pallas import tpu as pltpu
from jax.experimental.pallas import tpu_sc as plsc

F32 = jnp.float32
BF16 = jnp.bfloat16

HEAD_DIM = 64
LANES = 128
N_SB_HEADS = 8
N_FOX_HEADS = 8
N_SWA_HEADS = 16
N_SWA_KV_HEADS = 2
SWA_GROUP = N_SWA_HEADS // N_SWA_KV_HEADS
CHUNK = 64
WINDOW_CHUNKS = 2
SWA_CACHE = 128
N_EXPERTS = 8
RMS_EPS = 1e-6
NEG_INF = -1e30
ATTN_SCALE = HEAD_DIM ** -0.5
SB_W = N_SB_HEADS * HEAD_DIM
FOX_W = N_FOX_HEADS * HEAD_DIM
SWA_Q_W = N_SWA_HEADS * HEAD_DIM
SWA_KV_W = N_SWA_KV_HEADS * HEAD_DIM

VMEM_LIMIT_BYTES = 56 * 1024 * 1024
ROW_TILE = 512
COMBINE_TILE = 1024
ATTN_TILE = 512
ATTN_ROWS = 512
SAMPLE_ATTN_TILE = 2048
CUMSUM_SUB = 256
SWA_TILE = 256
SAMPLE_EXPERT_ROWS = 256
LOG2_E = 1.4426950408889634
EXPERT_ROWS = 1024
EXPERT_COLS = 512
SAMPLE_EXPERT_COLS = 896
GATHER_CHUNK = 256
SC_WINDOW = 128
SLOT_ALIGN = 1024
NT_DIMS = (((1,), (1,)), ((), ()))


def _params(sem):
    return pltpu.CompilerParams(dimension_semantics=sem, vmem_limit_bytes=VMEM_LIMIT_BYTES)


def _resident(shape):
    n = len(shape)
    return pl.BlockSpec(shape, lambda *_: (0,) * n, pipeline_mode=pl.Buffered(1))


def _rms(x, g):
    return x * lax.rsqrt(jnp.mean(x * x, axis=-1, keepdims=True) + RMS_EPS) * g


def _log_sigmoid(z):
    return jnp.minimum(z, 0.0) - jnp.log(1.0 + jnp.exp(-jnp.abs(z)))


def _silu(a):
    return a * (1.0 / (1.0 + jnp.exp(-a)))


def _dot(a, b):
    return jnp.dot(a, b, preferred_element_type=F32)


def _dot_nt(a, b):
    return lax.dot_general(a, b, NT_DIMS, preferred_element_type=F32)


def _split3(x):
    hi = x.astype(BF16)
    r = x - hi.astype(F32)
    mid = r.astype(BF16)
    lo = (r - mid.astype(F32)).astype(BF16)
    return hi, mid, lo


def _proj_ab_kernel(x_ref, g_ref, w_ref, wf_ref, bf_ref,
                    qsb_ref, ksb_ref, vsb_ref, qfx_ref, kfx_ref, vfx_ref,
                    ksb16_ref, vsb16_ref, kfx16_ref, vfx16_ref, logf_ref):
    hn = _rms(x_ref[...], g_ref[...]).astype(BF16)
    f32_outs = (None, ksb_ref, vsb_ref, None, kfx_ref, vfx_ref)
    b16_outs = (qsb_ref, ksb16_ref, vsb16_ref, qfx_ref, kfx16_ref, vfx16_ref)
    for c in range(6):
        p = _dot(hn, w_ref[:, c * SB_W:(c + 1) * SB_W])
        if f32_outs[c] is None:
            b16_outs[c][...] = (p * (ATTN_SCALE * LOG2_E if c == 0 else ATTN_SCALE)).astype(BF16)
        else:
            f32_outs[c][...] = p.reshape(p.shape[0], SB_W // HEAD_DIM, HEAD_DIM)
            b16_outs[c][...] = p.astype(BF16)
    f = _dot(hn, wf_ref[...])[:, :N_FOX_HEADS] + bf_ref[...]
    logf_ref[...] = _log_sigmoid(f)


def _proj_ab(x, g, w, wf, bf):
    n, d = x.shape
    tm = min(ROW_TILE, n)
    row = lambda c: pl.BlockSpec((tm, c), lambda i: (i, 0))
    f32o = jax.ShapeDtypeStruct((n, SB_W // HEAD_DIM, HEAD_DIM), F32)
    b16o = jax.ShapeDtypeStruct((n, SB_W), BF16)
    st = pl.BlockSpec((tm, SB_W // HEAD_DIM, HEAD_DIM), lambda i: (i, 0, 0))
    w16 = row(SB_W)
    return pl.pallas_call(
        _proj_ab_kernel,
        grid=(n // tm,),
        in_specs=[row(d), _resident((1, d)), _resident(w.shape), _resident(wf.shape), _resident((1, N_FOX_HEADS))],
        out_specs=[w16, st, st, w16, st, st, w16, w16, w16, w16, row(N_FOX_HEADS)],
        out_shape=[b16o, f32o, f32o, b16o, f32o, f32o, b16o, b16o, b16o, b16o,
                   jax.ShapeDtypeStruct((n, N_FOX_HEADS), F32)],
        compiler_params=_params(("parallel",)),
        name="proj_ab",
    )(x, g, w, wf, bf)


def _cumsum_kernel(x_ref, o_ref):
    t = x_ref.shape[-1]
    r = lax.broadcasted_iota(jnp.int32, (LANES, LANES), 0)
    c = lax.broadcasted_iota(jnp.int32, (LANES, LANES), 1)
    tri = (r <= c).astype(BF16)
    carry = jnp.zeros((x_ref.shape[1], 1), F32)
    for i in range(t // LANES):
        hi, mid, lo = _split3(x_ref[0, :, i * LANES:(i + 1) * LANES])
        cs = _dot(hi, tri) + _dot(mid, tri) + _dot(lo, tri) + carry
        o_ref[0, :, i * LANES:(i + 1) * LANES] = cs
        carry = cs[:, LANES - 1:LANES]


def _cumsum_time(x):
    b, h, t = x.shape
    spec = pl.BlockSpec((1, h, t), lambda i: (i, 0, 0))
    return pl.pallas_call(
        _cumsum_kernel, grid=(b,), in_specs=[spec], out_specs=spec,
        out_shape=jax.ShapeDtypeStruct(x.shape, F32),
        compiler_params=_params(("parallel",)), name="cumsum_time",
    )(x)


def _stacked_heads(q):
    lane = lax.broadcasted_iota(jnp.int32, q.shape, 1)
    zero = jnp.zeros_like(q)
    return jnp.concatenate([jnp.where(lane < HEAD_DIM, q, zero), jnp.where(lane >= HEAD_DIM, q, zero)], axis=0)


def _unstack_heads(acc):
    tq = acc.shape[0] // 2
    lane = lax.broadcasted_iota(jnp.int32, (tq, LANES), 1)
    return jnp.where(lane < HEAD_DIM, acc[:tq], acc[tq:])


def _causal_mask(tq, td, first_row, strict):
    row = first_row + lax.broadcasted_iota(jnp.int32, (2 * tq, td), 0) % tq
    col = lax.broadcasted_iota(jnp.int32, (2 * tq, td), 1)
    return col < row if strict else col <= row


def _strict_upper(n):
    j = lax.broadcasted_iota(jnp.int32, (n, n), 0)
    s = lax.broadcasted_iota(jnp.int32, (n, n), 1)
    return (j > s).astype(BF16)


def _sb_block(z, v, total, acc_ref, u, vis):
    sub = u.shape[0]
    neg_abs = lax.bitcast_convert_type(lax.bitcast_convert_type(z, jnp.uint32) | jnp.uint32(0x80000000), F32)
    ls = jnp.minimum(z, 0.0) - jnp.log2(1.0 + jnp.exp2(neg_abs))
    lk = ls - z
    if vis is not None:
        lk = jnp.where(vis, lk, 0.0)
    lk = lk.astype(BF16)
    for s in reversed(range(z.shape[1] // sub)):
        cols = slice(s * sub, (s + 1) * sub)
        after = _dot(lk[:, cols], u)
        w = jnp.exp2(ls[:, cols] + after + total)
        if vis is not None:
            w = jnp.where(vis[:, cols], w, 0.0)
        acc_ref[...] += _dot(w.astype(BF16), v[cols, :])
        total = total + after[:, 0:1] + lk[:, s * sub:s * sub + 1].astype(F32)
    return total


def _sb_kernel(q_ref, kd_ref, vd_ref, kp_ref, vp_ref, o_ref, acc_ref, *, tk, base, per_q):
    tq, td = q_ref.shape[1], kd_ref.shape[1]
    qs = _stacked_heads(q_ref[0])
    first_row = (pl.program_id(2) % (td // tq)) * tq
    vis = _causal_mask(tq, td, first_row, strict=True)
    u_diag = _strict_upper(min(CUMSUM_SUB, td))
    u_past = _strict_upper(CUMSUM_SUB)
    acc_ref[...] = jnp.zeros_like(acc_ref)
    total = _sb_block(_dot_nt(qs, kd_ref[0]), vd_ref[0], jnp.zeros((2 * tq, 1), F32), acc_ref, u_diag, vis)

    n_past = base + pl.program_id(2) // (td // tq) * per_q

    def body(jj, total):
        start = pl.multiple_of((n_past - 1 - jj) * tk, tk)
        k = kp_ref[0, pl.ds(start, tk), :].astype(BF16)
        v = vp_ref[0, pl.ds(start, tk), :].astype(BF16)
        return _sb_block(_dot_nt(qs, k), v, total, acc_ref, u_past, None)

    lax.fori_loop(0, n_past, body, total)
    o_ref[0] = _unstack_heads(acc_ref[...]).astype(o_ref.dtype)


def _fox_block(qk, v, ck, m, l, acc_ref, vis):
    tq = qk.shape[0] // 2
    s = jnp.concatenate([qk[:tq] - ck[0:1, :], qk[tq:] - ck[1:2, :]], axis=0)
    if vis is not None:
        s = jnp.where(vis, s, NEG_INF)
    m_new = jnp.maximum(m, jnp.max(s, axis=1, keepdims=True))
    alpha = jnp.exp(m - m_new)
    p = jnp.exp(s - m_new)
    acc_ref[...] = alpha * acc_ref[...] + _dot(p.astype(BF16), v)
    return m_new, alpha * l + jnp.sum(p, axis=1, keepdims=True)


def _fox_kernel(q_ref, kd_ref, vd_ref, kp_ref, vp_ref, ckd_ref, ckp_ref, o_ref, acc_ref, *, tk, base, per_q):
    tq, td = q_ref.shape[1], kd_ref.shape[1]
    qs = _stacked_heads(q_ref[0])
    first_row = (pl.program_id(2) % (td // tq)) * tq
    vis = _causal_mask(tq, td, first_row, strict=False)
    acc_ref[...] = jnp.zeros_like(acc_ref)
    m0 = jnp.full((2 * tq, 1), NEG_INF, F32)
    l0 = jnp.zeros((2 * tq, 1), F32)
    stats = _fox_block(_dot_nt(qs, kd_ref[0]), vd_ref[0], ckd_ref[0, 0], m0, l0, acc_ref, vis)

    n_past = base + pl.program_id(2) // (td // tq) * per_q

    def body(j, stats):
        start = pl.multiple_of(j * tk, tk)
        k = kp_ref[0, pl.ds(start, tk), :].astype(BF16)
        v = vp_ref[0, pl.ds(start, tk), :].astype(BF16)
        return _fox_block(_dot_nt(qs, k), v, ckp_ref[0, 0, j], stats[0], stats[1], acc_ref, None)

    _, l = lax.fori_loop(0, n_past, body, stats)
    o_ref[0] = _unstack_heads(acc_ref[...] / l).astype(o_ref.dtype)


def _attn_ab(kernel_fn, q, kd, vd, kp, vp, extra, extra_specs, *, tq, td, tk, base, per_q, name):
    b, t, w = q.shape
    tp = kp.shape[1]
    grid = (b, w // LANES, t // tq)
    qspec = pl.BlockSpec((1, tq, LANES), lambda bi, hp, i: (bi, i, hp))
    dspec = pl.BlockSpec((1, td, LANES), lambda bi, hp, i: (bi, i // (td // tq), hp))
    pspec = pl.BlockSpec((1, tp, LANES), lambda bi, hp, i: (bi, 0, hp))
    return pl.pallas_call(
        functools.partial(kernel_fn, tk=tk, base=base, per_q=per_q),
        grid=grid,
        in_specs=[qspec, dspec, dspec, pspec, pspec] + extra_specs,
        out_specs=qspec,
        out_shape=jax.ShapeDtypeStruct(q.shape, BF16),
        scratch_shapes=[pltpu.VMEM((2 * tq, LANES), F32)],
        compiler_params=_params(("parallel", "parallel", "arbitrary")),
        name=name,
    )(q, kd, vd, kp, vp, *extra)


def _fox_bias_operands(c_q, c_p, tq, td, tk):
    b, t, h = c_q.shape
    tp = c_p.shape[1]
    hp = h // 2
    ckd = c_q.reshape(b, t, hp, 2).transpose(0, 2, 3, 1)
    ckp = c_p.reshape(b, tp // tk, tk, hp, 2).transpose(0, 3, 1, 4, 2)
    specs = [pl.BlockSpec((1, 1, 2, td), lambda bi, p, i: (bi, p, 0, i // (td // tq))),
             pl.BlockSpec((1, 1, tp // tk, 2, tk), lambda bi, p, i: (bi, p, 0, 0, 0))]
    return [ckd, ckp], specs


def _ffn_dense_kernel(x_ref, osb_ref, ofx_ref, wo_ref, g_ref, wg_ref, wu_ref, wd_ref, o_ref, *, tf):
    o_ref[...] = x_ref[...] + _dot(osb_ref[...], wo_ref[:SB_W, :]) + _dot(ofx_ref[...], wo_ref[SB_W:, :])
    hn = _rms(o_ref[...], g_ref[...]).astype(BF16)
    for c in range(wg_ref.shape[1] // tf):
        a = _dot(hn, wg_ref[:, c * tf:(c + 1) * tf])
        u = _dot(hn, wu_ref[:, c * tf:(c + 1) * tf])
        o_ref[...] += _dot((_silu(a) * u).astype(BF16), wd_ref[c * tf:(c + 1) * tf, :])


def _ffn_dense(x, osb, ofx, wo, g, wg, wu, wd):
    n, d = x.shape
    tm = min(ROW_TILE, n)
    row = lambda c: pl.BlockSpec((tm, c), lambda i: (i, 0))
    return pl.pallas_call(
        functools.partial(_ffn_dense_kernel, tf=256),
        grid=(n // tm,),
        in_specs=[row(d), row(SB_W), row(FOX_W), _resident(wo.shape), _resident((1, d)),
                  _resident(wg.shape), _resident(wu.shape), _resident(wd.shape)],
        out_specs=row(d),
        out_shape=jax.ShapeDtypeStruct((n, d), F32),
        compiler_params=_params(("parallel",)),
        name="ffn_dense",
    )(x, osb, ofx, wo, g, wg, wu, wd)


def _proj_c_kernel(x_ref, g_ref, w_ref, q_ref, k_ref, v_ref, k16_ref, v16_ref):
    hn = _rms(x_ref[...], g_ref[...]).astype(BF16)
    q_ref[...] = (_dot(hn, w_ref[:, :SWA_Q_W]) * ATTN_SCALE).astype(BF16)
    k = _dot(hn, w_ref[:, SWA_Q_W:SWA_Q_W + SWA_KV_W])
    v = _dot(hn, w_ref[:, SWA_Q_W + SWA_KV_W:])
    k_ref[...] = k
    v_ref[...] = v
    k16_ref[...] = k.astype(BF16)
    v16_ref[...] = v.astype(BF16)


def _proj_c(x, g, w):
    n, d = x.shape
    tm = min(ROW_TILE, n)
    row = lambda c: pl.BlockSpec((tm, c), lambda i: (i, 0))
    kv32 = jax.ShapeDtypeStruct((n, SWA_KV_W), F32)
    kv16 = jax.ShapeDtypeStruct((n, SWA_KV_W), BF16)
    return pl.pallas_call(
        _proj_c_kernel,
        grid=(n // tm,),
        in_specs=[row(d), _resident((1, d)), _resident(w.shape)],
        out_specs=[row(SWA_Q_W)] + [row(SWA_KV_W)] * 4,
        out_shape=[jax.ShapeDtypeStruct((n, SWA_Q_W), BF16), kv32, kv32, kv16, kv16],
        compiler_params=_params(("parallel",)),
        name="proj_c",
    )(x, g, w)


def _swap_halves(a):
    return jnp.concatenate([a[:, HEAD_DIM:], a[:, :HEAD_DIM]], axis=1)


def _swa_kernel(sink_ref, q_ref, kprev_ref, vprev_ref, kcur_ref, vcur_ref, o_ref, *, pos0):
    tq = q_ref.shape[1]
    band = (WINDOW_CHUNKS + 1) * CHUNK
    stack = SWA_GROUP // 2
    rows = stack * CHUNK
    first = pos0 + pl.program_id(1) * tq
    kw = jnp.concatenate([kprev_ref[0].astype(BF16), kcur_ref[0].astype(BF16)], axis=0)
    vw = jnp.concatenate([vprev_ref[0].astype(BF16), vcur_ref[0].astype(BF16)], axis=0)
    k_by_half = (kw, _swap_halves(kw))
    v_by_half = (vw, _swap_halves(vw))
    q = q_ref[0]
    in_low = lax.broadcasted_iota(jnp.int32, q.shape, 1) % LANES < HEAD_DIM
    q_by_half = (jnp.where(in_low, q, jnp.zeros_like(q)), jnp.where(in_low, jnp.zeros_like(q), q))
    r_id = lax.broadcasted_iota(jnp.int32, (rows, band), 0)
    c_id = lax.broadcasted_iota(jnp.int32, (rows, band), 1)
    dist = jnp.abs(WINDOW_CHUNKS * CHUNK + r_id % CHUNK - c_id).astype(F32)
    head_in_stack = lax.broadcasted_iota(jnp.int32, (rows, 1), 0) // CHUNK
    col = lax.broadcasted_iota(jnp.int32, (1, band), 1)
    lane = lax.broadcasted_iota(jnp.int32, (CHUNK, LANES), 1)

    def per_head(values):
        out = jnp.full((rows, 1), values[-1], F32)
        for i in range(stack - 2, -1, -1):
            out = jnp.where(head_in_stack == i, values[i], out)
        return out

    heads = {(g, half): [g * SWA_GROUP + 2 * i + half for i in range(stack)]
             for g in range(N_SWA_KV_HEADS) for half in range(2)}
    bias = {key: per_head([2.0 ** (-8.0 * (h + 1) / N_SWA_HEADS) for h in hs]) * dist for key, hs in heads.items()}
    sink = {key: per_head([sink_ref[h] for h in hs]) for key, hs in heads.items()}

    for c in range(tq // CHUNK):
        in_range = first - WINDOW_CHUNKS * CHUNK + c * CHUNK + col >= 0
        win = slice(c * CHUNK, c * CHUNK + band)
        outs = {}
        for (g, half), hs in heads.items():
            qs = jnp.concatenate([q_by_half[half][c * CHUNK:(c + 1) * CHUNK, (h // 2) * LANES:(h // 2 + 1) * LANES]
                                  for h in hs], axis=0)
            s = _dot_nt(qs, k_by_half[g ^ half][win]) - bias[g, half]
            s = jnp.where(in_range, s, NEG_INF)
            m = jnp.maximum(jnp.max(s, axis=1, keepdims=True), sink[g, half])
            p = jnp.exp(s - m)
            l = jnp.sum(p, axis=1, keepdims=True) + jnp.exp(sink[g, half] - m)
            outs[g, half] = _dot(p.astype(BF16), v_by_half[g ^ half][win]) / l
        for g in range(N_SWA_KV_HEADS):
            for i in range(stack):
                pair = (g * SWA_GROUP + 2 * i) // 2
                o_ref[0, c * CHUNK:(c + 1) * CHUNK, pair * LANES:(pair + 1) * LANES] = jnp.where(
                    lane < HEAD_DIM, outs[g, 0][i * CHUNK:(i + 1) * CHUNK],
                    outs[g, 1][i * CHUNK:(i + 1) * CHUNK]).astype(o_ref.dtype)


def _swa(sinks, q, kprev, vprev, kcur, vcur, *, tq, pos0, prev_is_cache):
    b, t, _ = q.shape
    per = tq // SWA_CACHE
    prev_map = ((lambda bi, i, s: (bi, 0, 0)) if prev_is_cache
                else (lambda bi, i, s: (bi, jnp.maximum(i * per - 1, 0), 0)))
    cur = lambda w: pl.BlockSpec((1, tq, w), lambda bi, i, s: (bi, i, 0))
    prev = pl.BlockSpec((1, SWA_CACHE, SWA_KV_W), prev_map)
    return pl.pallas_call(
        functools.partial(_swa_kernel, pos0=pos0),
        grid_spec=pltpu.PrefetchScalarGridSpec(
            num_scalar_prefetch=1, grid=(b, t // tq),
            in_specs=[cur(SWA_Q_W), prev, prev, cur(SWA_KV_W), cur(SWA_KV_W)],
            out_specs=cur(SWA_Q_W)),
        out_shape=jax.ShapeDtypeStruct(q.shape, BF16),
        compiler_params=_params(("parallel", "arbitrary")),
        name="swa",
    )(sinks, q, kprev, vprev, kcur, vcur)


def _route_kernel(x_ref, o_ref, wo_ref, g_ref, wr_ref, x2_ref, xn_ref, route_ref):
    x2 = x_ref[...] + _dot(o_ref[...], wo_ref[...])
    xn = _rms(x2, g_ref[...])
    x2_ref[...] = x2
    _store_chunked(xn_ref, xn)
    hi = xn.astype(BF16)
    mid = (xn - hi.astype(F32)).astype(BF16)
    d = xn.shape[1]
    logits = _dot(hi, wr_ref[:d, :]) + _dot(hi, wr_ref[d:, :]) + _dot(mid, wr_ref[:d, :])
    lane = lax.broadcasted_iota(jnp.int32, logits.shape, 1).astype(F32)
    lg = jnp.where(lane < N_EXPERTS, logits, -jnp.inf)
    m1 = jnp.max(lg, axis=1, keepdims=True)
    i1 = jnp.min(jnp.where(lg == m1, lane, float(LANES)), axis=1, keepdims=True)
    lg2 = jnp.where(lane == i1, -jnp.inf, lg)
    m2 = jnp.max(lg2, axis=1, keepdims=True)
    i2 = jnp.min(jnp.where(lg2 == m2, lane, float(LANES)), axis=1, keepdims=True)
    e = jnp.exp(m2 - m1)
    g1 = 1.0 / (1.0 + e)
    g2 = e / (1.0 + e)
    route = jnp.where(lane == 0, i1, jnp.where(lane == 1, i2, jnp.where(lane == 2, g1, jnp.where(lane == 3, g2, 0.0))))
    route_ref[...] = route


def _route(x, o, wo, g, wr):
    n, d = x.shape
    tm = min(ROW_TILE, n)
    row = lambda c: pl.BlockSpec((tm, c), lambda i: (i, 0))
    xs = jax.ShapeDtypeStruct((n, d), F32)
    return pl.pallas_call(
        _route_kernel,
        grid=(n // tm,),
        in_specs=[row(d), row(SWA_Q_W), _resident(wo.shape), _resident((1, d)), _resident(wr.shape)],
        out_specs=[row(d), _chunked_spec(tm, d), row(LANES)],
        out_shape=[xs, _chunked_shape(n, d), jax.ShapeDtypeStruct((n, LANES), F32)],
        compiler_params=_params(("parallel",)),
        name="route",
    )(x, o, wo, g, wr)


def _chunked_shape(rows, d):
    return jax.ShapeDtypeStruct((d // GATHER_CHUNK, rows, GATHER_CHUNK), F32)


def _chunked_spec(tm, d, index_map=lambda i: (0, i, 0)):
    return pl.BlockSpec((d // GATHER_CHUNK, tm, GATHER_CHUNK), index_map)


def _store_chunked(ref, x):
    for j in range(x.shape[1] // GATHER_CHUNK):
        ref[j] = x[:, j * GATHER_CHUNK:(j + 1) * GATHER_CHUNK]


def _load_chunked(ref):
    return jnp.concatenate([ref[j] for j in range(ref.shape[0])], axis=1)


def _sc_gather_flat(flat, idx):
    m, w = idx.shape[1], flat.shape[1]
    mesh = plsc.VectorSubcoreMesh(core_axis_name="core", subcore_axis_name="subcore")

    @pl.kernel(out_type=jax.ShapeDtypeStruct((m, w), flat.dtype), mesh=mesh)
    def gather(x_hbm, i_hbm, o_hbm):
        def body(i_vmem, o_vmem):
            pltpu.sync_copy(x_hbm.at[i_vmem.at[0]], o_vmem)

        pltpu.emit_pipeline(
            body,
            grid=(m // SC_WINDOW,),
            in_specs=[pl.BlockSpec((1, SC_WINDOW), index_map=lambda i: (0, i))],
            out_specs=[pl.BlockSpec((SC_WINDOW, w), index_map=lambda i: (i, 0))],
            core_axis_name=("core", "subcore"),
            dimension_semantics=(pltpu.PARALLEL,),
        )(i_hbm, o_hbm)

    return gather(flat, idx)


def _gather_indices(rows, n_chunks, src_rows):
    return jnp.concatenate([rows + j * src_rows for j in range(n_chunks)]).reshape(1, -1)


def _sc_gather(src, idx):
    n_chunks, _, w = src.shape
    out = _sc_gather_flat(src.reshape(-1, w), idx)
    return out.reshape(n_chunks, idx.shape[1] // n_chunks, w)


def _expert_kernel(be_ref, nv_ref, x_ref, wg_ref, wu_ref, wd_ref, o_ref, xb_ref, acc_ref):
    blk, f = pl.program_id(0), pl.program_id(1)
    last = pl.num_programs(1) - 1
    valid = blk < nv_ref[0]

    @pl.when(valid & (f == 0))
    def _():
        for j in range(x_ref.shape[0]):
            xb_ref[:, j * GATHER_CHUNK:(j + 1) * GATHER_CHUNK] = x_ref[j].astype(BF16)
        acc_ref[...] = jnp.zeros_like(acc_ref)

    @pl.when(valid)
    def _():
        xb = xb_ref[...]
        h = (_silu(_dot(xb, wg_ref[0].astype(BF16))) * _dot(xb, wu_ref[0].astype(BF16))).astype(BF16)
        acc_ref[...] += _dot(h, wd_ref[0].astype(BF16))

    @pl.when(valid & (f == last))
    def _():
        _store_chunked(o_ref, acc_ref[...])

    @pl.when(jnp.logical_not(valid) & (f == last))
    def _():
        o_ref[...] = jnp.zeros_like(o_ref)


def _experts(block_expert, n_valid, x_slots, wg, wu, wd, *, tm, tf):
    d = wg.shape[1]
    cap = x_slots.shape[1]
    nf = wg.shape[2] // tf
    fsel = lambda b, f, nv: jnp.where(b < nv[0], f, nf - 1)
    slots = _chunked_spec(tm, d, lambda b, f, be, nv: (0, b, 0))
    return pl.pallas_call(
        _expert_kernel,
        grid_spec=pltpu.PrefetchScalarGridSpec(
            num_scalar_prefetch=2, grid=(cap // tm, nf),
            in_specs=[slots,
                      pl.BlockSpec((1, d, tf), lambda b, f, be, nv: (be[b], 0, fsel(b, f, nv))),
                      pl.BlockSpec((1, d, tf), lambda b, f, be, nv: (be[b], 0, fsel(b, f, nv))),
                      pl.BlockSpec((1, tf, d), lambda b, f, be, nv: (be[b], fsel(b, f, nv), 0))],
            out_specs=slots,
            scratch_shapes=[pltpu.VMEM((tm, d), BF16), pltpu.VMEM((tm, d), F32)]),
        out_shape=_chunked_shape(cap, d),
        compiler_params=_params(("arbitrary", "arbitrary")),
        name="moe_experts",
    )(block_expert, n_valid, x_slots, wg, wu, wd)


def _combine_kernel(y1_ref, y2_ref, x_ref, route_ref, g_ref, o_ref):
    route = route_ref[...]
    y = x_ref[...] + (_load_chunked(y1_ref) * route[:, 2:3] + _load_chunked(y2_ref) * route[:, 3:4])
    o_ref[...] = _rms(y, g_ref[...])


def _combine(y_pairs, x2, route, g):
    n, d = x2.shape
    tm = min(COMBINE_TILE, n)
    row = lambda c: pl.BlockSpec((tm, c), lambda i: (i, 0))
    second = n // tm
    return pl.pallas_call(
        _combine_kernel,
        grid=(n // tm,),
        in_specs=[_chunked_spec(tm, d), _chunked_spec(tm, d, lambda i: (0, i + second, 0)),
                  row(d), row(LANES), _resident((1, d))],
        out_specs=row(d),
        out_shape=jax.ShapeDtypeStruct((n, d), F32),
        compiler_params=_params(("parallel",)),
        name="moe_combine",
    )(y_pairs, y_pairs, x2, route, g)


def _dispatch(route, tm):
    n = route.shape[0]
    expert = route[:, :2].astype(jnp.int32).reshape(-1)
    onehot = (expert[:, None] == jnp.arange(N_EXPERTS, dtype=jnp.int32)[None, :]).astype(jnp.int32)
    csum = jnp.cumsum(onehot, axis=0)
    rank = jnp.sum(onehot * csum, axis=1) - 1
    counts = csum[-1]
    padded = (counts + tm - 1) // tm * tm
    pend = jnp.cumsum(padded)
    pstart = pend - padded
    dest = (jnp.sum(onehot * pstart[None, :], axis=1) + rank).astype(jnp.int32)
    step = max(tm, SLOT_ALIGN)
    cap = -(-(2 * n + N_EXPERTS * (tm - 1)) // step) * step
    n_blocks = cap // tm
    token = jnp.arange(2 * n, dtype=jnp.int32) // 2
    slot_token = jnp.zeros((n_blocks * tm,), jnp.int32).at[dest].set(token, unique_indices=True)
    block_expert = jnp.minimum(
        jnp.searchsorted(pend, jnp.arange(n_blocks, dtype=jnp.int32) * tm, side="right"), N_EXPERTS - 1
    ).astype(jnp.int32)
    n_valid = (pend[-1:] // tm).astype(jnp.int32)
    return slot_token, dest, block_expert, n_valid


def _run(x, caches, wts, *, tq_ab, tq_c, expert_rows):
    b, t, d = x.shape
    n = b * t
    xf = x.reshape(n, d)
    (qsb, ksb, vsb, qfx, kfx, vfx, ksb16, vsb16, kfx16, vfx16, logf) = _proj_ab(
        xf, wts["g_ab"], wts["w_in_ab"], wts["w_f"], wts["b_f"])
    r3 = lambda a: a.reshape(b, t, -1)
    logf3 = r3(logf)
    if caches is None:
        c = _cumsum_time(logf3.transpose(0, 2, 1)).transpose(0, 2, 1)
        c_q, c_p = c, c
        kp_sb, vp_sb, kp_fx, vp_fx = r3(ksb16), r3(vsb16), r3(kfx16), r3(vfx16)
        td = tk = min(ATTN_TILE, t)
        base, per_q = 0, 1
    else:
        p_sb_k, p_sb_v, p_fx_k, p_fx_v, p_logf = caches[:5]
        past = p_logf.shape[1]
        pad = (-(past + t)) % LANES
        allf = jnp.concatenate([p_logf.astype(F32), logf3, jnp.zeros((b, pad, N_FOX_HEADS), F32)], axis=1)
        c = _cumsum_time(allf.transpose(0, 2, 1)).transpose(0, 2, 1)
        c_q, c_p = c[:, past:past + t], c[:, :past]
        kp_sb, vp_sb, kp_fx, vp_fx = (a.reshape(b, past, -1) for a in (p_sb_k, p_sb_v, p_fx_k, p_fx_v))
        td, tk = t, min(SAMPLE_ATTN_TILE, past)
        base, per_q = past // tk, 0
    kw = dict(tq=tq_ab, td=td, tk=tk, base=base, per_q=per_q)
    o_sb = _attn_ab(_sb_kernel, r3(qsb), r3(ksb16), r3(vsb16), kp_sb, vp_sb, [], [], name="attn_sb", **kw)
    extra, extra_specs = _fox_bias_operands(c_q, c_p, tq_ab, td, tk)
    o_fx = _attn_ab(_fox_kernel, r3(qfx), r3(kfx16), r3(vfx16), kp_fx, vp_fx, extra, extra_specs,
                    name="attn_fox", **kw)
    x1 = _ffn_dense(xf, o_sb.reshape(n, SB_W), o_fx.reshape(n, FOX_W), wts["w_out_ab"], wts["g_ffn"],
                    wts["w_gate"], wts["w_up"], wts["w_down"])

    qc, kc, vc, kc16, vc16 = _proj_c(x1, wts["g_c"], wts["w_in_c"])
    if caches is None:
        kprev, vprev, pos0 = r3(kc16), r3(vc16), 0
        new_k, new_v = r3(kc)[:, t - SWA_CACHE:], r3(vc)[:, t - SWA_CACHE:]
    else:
        p_k, p_v = (a.reshape(b, SWA_CACHE, SWA_KV_W) for a in caches[5:])
        kprev, vprev, pos0 = p_k, p_v, caches[0].shape[1]
        new_k = jnp.concatenate([p_k, r3(kc)], axis=1)[:, t:]
        new_v = jnp.concatenate([p_v, r3(vc)], axis=1)[:, t:]
    o_c = _swa(wts["sinks"], r3(qc), kprev, vprev, r3(kc16), r3(vc16), tq=tq_c, pos0=pos0,
               prev_is_cache=caches is not None)

    x2, xn, route = _route(x1, o_c.reshape(n, SWA_Q_W), wts["w_out_c"], wts["g_moe"], wts["w_router"])
    slot_token, dest, block_expert, n_valid = _dispatch(route, expert_rows)
    heads = lambda a, h: a.reshape(1, b, -1, h, HEAD_DIM)
    states = (heads(ksb, N_SB_HEADS), heads(vsb, N_SB_HEADS), heads(kfx, N_FOX_HEADS), heads(vfx, N_FOX_HEADS),
              logf3[None], heads(new_k, N_SWA_KV_HEADS), heads(new_v, N_SWA_KV_HEADS))
    n_chunks = d // GATHER_CHUNK
    pair_rows = dest.reshape(n, 2).T.reshape(-1)
    return dict(xn=xn, slot_idx=_gather_indices(slot_token, n_chunks, n),
                pair_idx=_gather_indices(pair_rows, n_chunks, slot_token.shape[0]),
                block_expert=block_expert, n_valid=n_valid, x2=x2, route=route, states=states, shape=(b, t, d),
                expert_rows=expert_rows)


def _moe_experts(ctx, wts):
    x_slots = _sc_gather(ctx["xn"], ctx["slot_idx"])
    y_slots = _experts(ctx["block_expert"], ctx["n_valid"], x_slots, wts["w_gate_moe"], wts["w_up_moe"],
                       wts["w_down_moe"], tm=ctx["expert_rows"],
                       tf=EXPERT_COLS if ctx["expert_rows"] == EXPERT_ROWS else SAMPLE_EXPERT_COLS)
    return dict(ctx, y_slots=y_slots)


def _moe_combine(ctx, wts):
    y_pairs = _sc_gather(ctx["y_slots"], ctx["pair_idx"])
    y = _combine(y_pairs, ctx["x2"], ctx["route"], wts["g_final"])
    return y.reshape(ctx["shape"]), ctx["states"]


def _issue_after(first, later):
    return lax.optimization_barrier((first, later))


def kernel(x_prompt, x_sample, cache_sb_k, cache_sb_v, cache_fox_k, cache_fox_v, cache_fox_logf, cache_swa_k,
           cache_swa_v, norm_mix_ab, w_in_ab, b_forget, w_out_ab, norm_ffn_dense, w_gate_dense, w_up_dense,
           w_down_dense, norm_mix_c, w_in_c, sinks, w_out_c, norm_ffn_moe, w_router, w_gate_moe, w_up_moe,
           w_down_moe, norm_final):
    d = x_prompt.shape[-1]
    main_w = 3 * SB_W + 3 * FOX_W
    w_f = jnp.zeros((d, LANES), F32).at[:, :N_FOX_HEADS].set(w_in_ab[0][:, main_w:])
    wr = jnp.zeros((d, LANES), F32).at[:, :N_EXPERTS].set(w_router[0])
    wr_hi = wr.astype(BF16)
    wr_mid = (wr - wr_hi.astype(F32)).astype(BF16)
    wts = dict(
        g_ab=norm_mix_ab[0][None], w_in_ab=w_in_ab[0][:, :main_w].astype(BF16), w_f=w_f.astype(BF16),
        b_f=b_forget[0][None], w_out_ab=w_out_ab[0].astype(BF16), g_ffn=norm_ffn_dense[0][None],
        w_gate=w_gate_dense[0].astype(BF16), w_up=w_up_dense[0].astype(BF16), w_down=w_down_dense[0].astype(BF16),
        g_c=norm_mix_c[0][None], w_in_c=w_in_c[0].astype(BF16), sinks=sinks[0], w_out_c=w_out_c[0].astype(BF16),
        g_moe=norm_ffn_moe[0][None], w_router=jnp.concatenate([wr_hi, wr_mid], axis=0),
        w_gate_moe=w_gate_moe[0], w_up_moe=w_up_moe[0], w_down_moe=w_down_moe[0], g_final=norm_final[None])
    t_p, t_s = x_prompt.shape[1], x_sample.shape[1]
    caches = (cache_sb_k[0], cache_sb_v[0], cache_fox_k[0], cache_fox_v[0], cache_fox_logf[0],
              cache_swa_k[0], cache_swa_v[0])
    prompt = _run(x_prompt, None, wts, tq_ab=min(ATTN_ROWS, t_p), tq_c=min(SWA_TILE, t_p), expert_rows=EXPERT_ROWS)
    prompt["slot_idx"], (x_sample, caches) = _issue_after(prompt["slot_idx"], (x_sample, caches))
    sample = _run(x_sample, caches, wts, tq_ab=t_s, tq_c=t_s, expert_rows=SAMPLE_EXPERT_ROWS)
    prompt = _moe_experts(prompt, wts)
    prompt["y_slots"], sample["xn"] = _issue_after(prompt["y_slots"], sample["xn"])
    sample = _moe_experts(sample, wts)
    y_p, st_p = _moe_combine(prompt, wts)
    y_s, st_s = _moe_combine(sample, wts)
    return (y_p, y_s) + st_p + st_s
```

```python
import functools

import jax
import jax.numpy as jnp
from jax import lax
from jax.experimental import pallas as pl
from jax.experimental.pallas import tpu as pltpu
from jax.experimental.pallas import tpu_sc as plsc

F32 = jnp.float32
BF16 = jnp.bfloat16

HEAD_DIM = 64
LANES = 128
N_SB_HEADS = 8
N_FOX_HEADS = 8
N_SWA_HEADS = 16
N_SWA_KV_HEADS = 2
SWA_GROUP = N_SWA_HEADS // N_SWA_KV_HEADS
CHUNK = 64
WINDOW_CHUNKS = 2
SWA_CACHE = 128
N_EXPERTS = 8
RMS_EPS = 1e-6
NEG_INF = -1e30
ATTN_SCALE = HEAD_DIM ** -0.5
SB_W = N_SB_HEADS * HEAD_DIM
FOX_W = N_FOX_HEADS * HEAD_DIM
SWA_Q_W = N_SWA_HEADS * HEAD_DIM
SWA_KV_W = N_SWA_KV_HEADS * HEAD_DIM

VMEM_LIMIT_BYTES = 56 * 1024 * 1024
ROW_TILE = 512
COMBINE_TILE = 1024
ATTN_TILE = 512
ATTN_ROWS = 512
SAMPLE_ATTN_TILE = 2048
CUMSUM_SUB = 256
SWA_TILE = 256
SAMPLE_EXPERT_ROWS = 256
LOG2_E = 1.4426950408889634
EXPERT_ROWS = 1024
EXPERT_COLS = 512
SAMPLE_EXPERT_COLS = 896
GATHER_CHUNK = 256
SC_WINDOW = 128
SLOT_ALIGN = 1024
NT_DIMS = (((1,), (1,)), ((), ()))


def _params(sem):
    return pltpu.CompilerParams(dimension_semantics=sem, vmem_limit_bytes=VMEM_LIMIT_BYTES)


def _resident(shape):
    n = len(shape)
    return pl.BlockSpec(shape, lambda *_: (0,) * n, pipeline_mode=pl.Buffered(1))


def _rms(x, g):
    return x * lax.rsqrt(jnp.mean(x * x, axis=-1, keepdims=True) + RMS_EPS) * g


def _log_sigmoid(z):
    return jnp.minimum(z, 0.0) - jnp.log(1.0 + jnp.exp(-jnp.abs(z)))


def _silu(a):
    return a * (1.0 / (1.0 + jnp.exp(-a)))


def _dot(a, b):
    return jnp.dot(a, b, preferred_element_type=F32)


def _dot_nt(a, b):
    return lax.dot_general(a, b, NT_DIMS, preferred_element_type=F32)


def _split3(x):
    hi = x.astype(BF16)
    r = x - hi.astype(F32)
    mid = r.astype(BF16)
    lo = (r - mid.astype(F32)).astype(BF16)
    return hi, mid, lo


def _proj_ab_kernel(x_ref, g_ref, w_ref, wf_ref, bf_ref,
                    qsb_ref, ksb_ref, vsb_ref, qfx_ref, kfx_ref, vfx_ref,
                    ksb16_ref, vsb16_ref, kfx16_ref, vfx16_ref, logf_ref):
    hn = _rms(x_ref[...], g_ref[...]).astype(BF16)
    f32_outs = (None, ksb_ref, vsb_ref, None, kfx_ref, vfx_ref)
    b16_outs = (qsb_ref, ksb16_ref, vsb16_ref, qfx_ref, kfx16_ref, vfx16_ref)
    for c in range(6):
        p = _dot(hn, w_ref[:, c * SB_W:(c + 1) * SB_W])
        if f32_outs[c] is None:
            b16_outs[c][...] = (p * (ATTN_SCALE * LOG2_E)).astype(BF16)
        else:
            f32_outs[c][...] = p.reshape(p.shape[0], SB_W // HEAD_DIM, HEAD_DIM)
            b16_outs[c][...] = p.astype(BF16)
    f = _dot(hn, wf_ref[...])[:, :N_FOX_HEADS] + bf_ref[...]
    logf_ref[...] = _log_sigmoid(f)


def _proj_ab(x, g, w, wf, bf):
    n, d = x.shape
    tm = min(ROW_TILE, n)
    row = lambda c: pl.BlockSpec((tm, c), lambda i: (i, 0))
    f32o = jax.ShapeDtypeStruct((n, SB_W // HEAD_DIM, HEAD_DIM), F32)
    b16o = jax.ShapeDtypeStruct((n, SB_W), BF16)
    st = pl.BlockSpec((tm, SB_W // HEAD_DIM, HEAD_DIM), lambda i: (i, 0, 0))
    w16 = row(SB_W)
    return pl.pallas_call(
        _proj_ab_kernel,
        grid=(n // tm,),
        in_specs=[row(d), _resident((1, d)), _resident(w.shape), _resident(wf.shape), _resident((1, N_FOX_HEADS))],
        out_specs=[w16, st, st, w16, st, st, w16, w16, w16, w16, row(N_FOX_HEADS)],
        out_shape=[b16o, f32o, f32o, b16o, f32o, f32o, b16o, b16o, b16o, b16o,
                   jax.ShapeDtypeStruct((n, N_FOX_HEADS), F32)],
        compiler_params=_params(("parallel",)),
        name="proj_ab",
    )(x, g, w, wf, bf)


def _cumsum_kernel(x_ref, o_ref):
    t = x_ref.shape[-1]
    r = lax.broadcasted_iota(jnp.int32, (LANES, LANES), 0)
    c = lax.broadcasted_iota(jnp.int32, (LANES, LANES), 1)
    tri = (r <= c).astype(BF16)
    carry = jnp.zeros((x_ref.shape[1], 1), F32)
    for i in range(t // LANES):
        hi, mid, lo = _split3(x_ref[0, :, i * LANES:(i + 1) * LANES])
        cs = _dot(hi, tri) + _dot(mid, tri) + _dot(lo, tri) + carry
        o_ref[0, :, i * LANES:(i + 1) * LANES] = cs * LOG2_E
        carry = cs[:, LANES - 1:LANES]


def _cumsum_time(x):
    b, h, t = x.shape
    spec = pl.BlockSpec((1, h, t), lambda i: (i, 0, 0))
    return pl.pallas_call(
        _cumsum_kernel, grid=(b,), in_specs=[spec], out_specs=spec,
        out_shape=jax.ShapeDtypeStruct(x.shape, F32),
        compiler_params=_params(("parallel",)), name="cumsum_time",
    )(x)


def _stacked_heads(q):
    lane = lax.broadcasted_iota(jnp.int32, q.shape, 1)
    zero = jnp.zeros_like(q)
    return jnp.concatenate([jnp.where(lane < HEAD_DIM, q, zero), jnp.where(lane >= HEAD_DIM, q, zero)], axis=0)


def _unstack_heads(acc):
    tq = acc.shape[0] // 2
    lane = lax.broadcasted_iota(jnp.int32, (tq, LANES), 1)
    return jnp.where(lane < HEAD_DIM, acc[:tq], acc[tq:])


def _causal_mask(tq, td, first_row, strict):
    row = first_row + lax.broadcasted_iota(jnp.int32, (2 * tq, td), 0) % tq
    col = lax.broadcasted_iota(jnp.int32, (2 * tq, td), 1)
    return col < row if strict else col <= row


def _strict_upper(n):
    j = lax.broadcasted_iota(jnp.int32, (n, n), 0)
    s = lax.broadcasted_iota(jnp.int32, (n, n), 1)
    return (j > s).astype(BF16)


def _sb_block(z, v, total, acc_ref, u, vis):
    sub = u.shape[0]
    neg_abs = lax.bitcast_convert_type(lax.bitcast_convert_type(z, jnp.uint32) | jnp.uint32(0x80000000), F32)
    ls = jnp.minimum(z, 0.0) - jnp.log2(1.0 + jnp.exp2(neg_abs))
    lk = ls - z
    if vis is not None:
        lk = jnp.where(vis, lk, 0.0)
    lk = lk.astype(BF16)
    for s in reversed(range(z.shape[1] // sub)):
        cols = slice(s * sub, (s + 1) * sub)
        after = _dot(lk[:, cols], u)
        w = jnp.exp2(ls[:, cols] + after + total)
        if vis is not None:
            w = jnp.where(vis[:, cols], w, 0.0)
        acc_ref[...] += _dot(w.astype(BF16), v[cols, :])
        total = total + after[:, 0:1] + lk[:, s * sub:s * sub + 1].astype(F32)
    return total


def _sb_kernel(q_ref, kd_ref, vd_ref, kp_ref, vp_ref, o_ref, acc_ref, *, tk, base, per_q):
    tq, td = q_ref.shape[1], kd_ref.shape[1]
    qs = _stacked_heads(q_ref[0])
    first_row = (pl.program_id(2) % (td // tq)) * tq
    vis = _causal_mask(tq, td, first_row, strict=True)
    u_diag = _strict_upper(min(CUMSUM_SUB, td))
    u_past = _strict_upper(CUMSUM_SUB)
    acc_ref[...] = jnp.zeros_like(acc_ref)
    total = _sb_block(_dot_nt(qs, kd_ref[0]), vd_ref[0], jnp.zeros((2 * tq, 1), F32), acc_ref, u_diag, vis)

    n_past = base + pl.program_id(2) // (td // tq) * per_q

    def body(jj, total):
        start = pl.multiple_of((n_past - 1 - jj) * tk, tk)
        k = kp_ref[0, pl.ds(start, tk), :].astype(BF16)
        v = vp_ref[0, pl.ds(start, tk), :].astype(BF16)
        return _sb_block(_dot_nt(qs, k), v, total, acc_ref, u_past, None)

    lax.fori_loop(0, n_past, body, total)
    o_ref[0] = _unstack_heads(acc_ref[...]).astype(o_ref.dtype)


def _fox_block(qk, v, ck, m, l, acc_ref, vis):
    tq = qk.shape[0] // 2
    s = jnp.concatenate([qk[:tq] - ck[0:1, :], qk[tq:] - ck[1:2, :]], axis=0)
    if vis is not None:
        s = jnp.where(vis, s, NEG_INF)
    m_new = jnp.maximum(m, jnp.max(s, axis=1, keepdims=True))
    alpha = jnp.exp2(m - m_new)
    p = jnp.exp2(s - m_new)
    acc_ref[...] = alpha * acc_ref[...] + _dot(p.astype(BF16), v)
    return m_new, alpha * l + jnp.sum(p, axis=1, keepdims=True)


def _fox_kernel(q_ref, kd_ref, vd_ref, kp_ref, vp_ref, ckd_ref, ckp_ref, o_ref, acc_ref, *, tk, base, per_q):
    tq, td = q_ref.shape[1], kd_ref.shape[1]
    qs = _stacked_heads(q_ref[0])
    first_row = (pl.program_id(2) % (td // tq)) * tq
    vis = _causal_mask(tq, td, first_row, strict=False)
    acc_ref[...] = jnp.zeros_like(acc_ref)
    m0 = jnp.full((2 * tq, 1), NEG_INF, F32)
    l0 = jnp.zeros((2 * tq, 1), F32)
    stats = _fox_block(_dot_nt(qs, kd_ref[0]), vd_ref[0], ckd_ref[0, 0], m0, l0, acc_ref, vis)

    n_past = base + pl.program_id(2) // (td // tq) * per_q

    def body(j, stats):
        start = pl.multiple_of(j * tk, tk)
        k = kp_ref[0, pl.ds(start, tk), :].astype(BF16)
        v = vp_ref[0, pl.ds(start, tk), :].astype(BF16)
        return _fox_block(_dot_nt(qs, k), v, ckp_ref[0, 0, j], stats[0], stats[1], acc_ref, None)

    _, l = lax.fori_loop(0, n_past, body, stats)
    o_ref[0] = _unstack_heads(acc_ref[...] / l).astype(o_ref.dtype)


def _attn_ab(kernel_fn, q, kd, vd, kp, vp, extra, extra_specs, *, tq, td, tk, base, per_q, name):
    b, t, w = q.shape
    tp = kp.shape[1]
    grid = (b, w // LANES, t // tq)
    qspec = pl.BlockSpec((1, tq, LANES), lambda bi, hp, i: (bi, i, hp))
    dspec = pl.BlockSpec((1, td, LANES), lambda bi, hp, i: (bi, i // (td // tq), hp))
    pspec = pl.BlockSpec((1, tp, LANES), lambda bi, hp, i: (bi, 0, hp))
    return pl.pallas_call(
        functools.partial(kernel_fn, tk=tk, base=base, per_q=per_q),
        grid=grid,
        in_specs=[qspec, dspec, dspec, pspec, pspec] + extra_specs,
        out_specs=qspec,
        out_shape=jax.ShapeDtypeStruct(q.shape, BF16),
        scratch_shapes=[pltpu.VMEM((2 * tq, LANES), F32)],
        compiler_params=_params(("parallel", "parallel", "arbitrary")),
        name=name,
    )(q, kd, vd, kp, vp, *extra)


def _fox_bias_operands(c_q, c_p, tq, td, tk):
    b, t, h = c_q.shape
    tp = c_p.shape[1]
    hp = h // 2
    ckd = c_q.reshape(b, t, hp, 2).transpose(0, 2, 3, 1)
    ckp = c_p.reshape(b, tp // tk, tk, hp, 2).transpose(0, 3, 1, 4, 2)
    specs = [pl.BlockSpec((1, 1, 2, td), lambda bi, p, i: (bi, p, 0, i // (td // tq))),
             pl.BlockSpec((1, 1, tp // tk, 2, tk), lambda bi, p, i: (bi, p, 0, 0, 0))]
    return [ckd, ckp], specs


def _ffn_dense_kernel(x_ref, osb_ref, ofx_ref, wo_ref, g_ref, wg_ref, wu_ref, wd_ref, o_ref, *, tf):
    o_ref[...] = x_ref[...] + _dot(osb_ref[...], wo_ref[:SB_W, :]) + _dot(ofx_ref[...], wo_ref[SB_W:, :])
    hn = _rms(o_ref[...], g_ref[...]).astype(BF16)
    for c in range(wg_ref.shape[1] // tf):
        a = _dot(hn, wg_ref[:, c * tf:(c + 1) * tf])
        u = _dot(hn, wu_ref[:, c * tf:(c + 1) * tf])
        o_ref[...] += _dot((_silu(a) * u).astype(BF16), wd_ref[c * tf:(c + 1) * tf, :])


def _ffn_dense(x, osb, ofx, wo, g, wg, wu, wd):
    n, d = x.shape
    tm = min(ROW_TILE, n)
    row = lambda c: pl.BlockSpec((tm, c), lambda i: (i, 0))
    return pl.pallas_call(
        functools.partial(_ffn_dense_kernel, tf=256),
        grid=(n // tm,),
        in_specs=[row(d), row(SB_W), row(FOX_W), _resident(wo.shape), _resident((1, d)),
                  _resident(wg.shape), _resident(wu.shape), _resident(wd.shape)],
        out_specs=row(d),
        out_shape=jax.ShapeDtypeStruct((n, d), F32),
        compiler_params=_params(("parallel",)),
        name="ffn_dense",
    )(x, osb, ofx, wo, g, wg, wu, wd)


def _proj_c_kernel(x_ref, g_ref, w_ref, q_ref, k_ref, v_ref, k16_ref, v16_ref):
    hn = _rms(x_ref[...], g_ref[...]).astype(BF16)
    q_ref[...] = (_dot(hn, w_ref[:, :SWA_Q_W]) * ATTN_SCALE).astype(BF16)
    k = _dot(hn, w_ref[:, SWA_Q_W:SWA_Q_W + SWA_KV_W])
    v = _dot(hn, w_ref[:, SWA_Q_W + SWA_KV_W:])
    k_ref[...] = k
    v_ref[...] = v
    k16_ref[...] = k.astype(BF16)
    v16_ref[...] = v.astype(BF16)


def _proj_c(x, g, w):
    n, d = x.shape
    tm = min(ROW_TILE, n)
    row = lambda c: pl.BlockSpec((tm, c), lambda i: (i, 0))
    kv32 = jax.ShapeDtypeStruct((n, SWA_KV_W), F32)
    kv16 = jax.ShapeDtypeStruct((n, SWA_KV_W), BF16)
    return pl.pallas_call(
        _proj_c_kernel,
        grid=(n // tm,),
        in_specs=[row(d), _resident((1, d)), _resident(w.shape)],
        out_specs=[row(SWA_Q_W)] + [row(SWA_KV_W)] * 4,
        out_shape=[jax.ShapeDtypeStruct((n, SWA_Q_W), BF16), kv32, kv32, kv16, kv16],
        compiler_params=_params(("parallel",)),
        name="proj_c",
    )(x, g, w)


def _swap_halves(a):
    return jnp.concatenate([a[:, HEAD_DIM:], a[:, :HEAD_DIM]], axis=1)


def _swa_kernel(sink_ref, q_ref, kprev_ref, vprev_ref, kcur_ref, vcur_ref, o_ref, *, pos0):
    tq = q_ref.shape[1]
    band = (WINDOW_CHUNKS + 1) * CHUNK
    stack = SWA_GROUP // 2
    rows = stack * CHUNK
    first = pos0 + pl.program_id(1) * tq
    kw = jnp.concatenate([kprev_ref[0].astype(BF16), kcur_ref[0].astype(BF16)], axis=0)
    vw = jnp.concatenate([vprev_ref[0].astype(BF16), vcur_ref[0].astype(BF16)], axis=0)
    k_by_half = (kw, _swap_halves(kw))
    v_by_half = (vw, _swap_halves(vw))
    q = q_ref[0]
    in_low = lax.broadcasted_iota(jnp.int32, q.shape, 1) % LANES < HEAD_DIM
    q_by_half = (jnp.where(in_low, q, jnp.zeros_like(q)), jnp.where(in_low, jnp.zeros_like(q), q))
    r_id = lax.broadcasted_iota(jnp.int32, (rows, band), 0)
    c_id = lax.broadcasted_iota(jnp.int32, (rows, band), 1)
    dist = jnp.abs(WINDOW_CHUNKS * CHUNK + r_id % CHUNK - c_id).astype(F32)
    head_in_stack = lax.broadcasted_iota(jnp.int32, (rows, 1), 0) // CHUNK
    col = lax.broadcasted_iota(jnp.int32, (1, band), 1)
    lane = lax.broadcasted_iota(jnp.int32, (CHUNK, LANES), 1)

    def per_head(values):
        out = jnp.full((rows, 1), values[-1], F32)
        for i in range(stack - 2, -1, -1):
            out = jnp.where(head_in_stack == i, values[i], out)
        return out

    heads = {(g, half): [g * SWA_GROUP + 2 * i + half for i in range(stack)]
             for g in range(N_SWA_KV_HEADS) for half in range(2)}
    bias = {key: per_head([2.0 ** (-8.0 * (h + 1) / N_SWA_HEADS) for h in hs]) * dist for key, hs in heads.items()}
    sink = {key: per_head([sink_ref[h] for h in hs]) for key, hs in heads.items()}

    for c in range(tq // CHUNK):
        in_range = first - WINDOW_CHUNKS * CHUNK + c * CHUNK + col >= 0
        win = slice(c * CHUNK, c * CHUNK + band)
        outs = {}
        for (g, half), hs in heads.items():
            qs = jnp.concatenate([q_by_half[half][c * CHUNK:(c + 1) * CHUNK, (h // 2) * LANES:(h // 2 + 1) * LANES]
                                  for h in hs], axis=0)
            s = _dot_nt(qs, k_by_half[g ^ half][win]) - bias[g, half]
            s = jnp.where(in_range, s, NEG_INF)
            m = jnp.maximum(jnp.max(s, axis=1, keepdims=True), sink[g, half])
            p = jnp.exp(s - m)
            l = jnp.sum(p, axis=1, keepdims=True) + jnp.exp(sink[g, half] - m)
            outs[g, half] = _dot(p.astype(BF16), v_by_half[g ^ half][win]) / l
        for g in range(N_SWA_KV_HEADS):
            for i in range(stack):
                pair = (g * SWA_GROUP + 2 * i) // 2
                o_ref[0, c * CHUNK:(c + 1) * CHUNK, pair * LANES:(pair + 1) * LANES] = jnp.where(
                    lane < HEAD_DIM, outs[g, 0][i * CHUNK:(i + 1) * CHUNK],
                    outs[g, 1][i * CHUNK:(i + 1) * CHUNK]).astype(o_ref.dtype)


def _swa(sinks, q, kprev, vprev, kcur, vcur, *, tq, pos0, prev_is_cache):
    b, t, _ = q.shape
    per = tq // SWA_CACHE
    prev_map = ((lambda bi, i, s: (bi, 0, 0)) if prev_is_cache
                else (lambda bi, i, s: (bi, jnp.maximum(i * per - 1, 0), 0)))
    cur = lambda w: pl.BlockSpec((1, tq, w), lambda bi, i, s: (bi, i, 0))
    prev = pl.BlockSpec((1, SWA_CACHE, SWA_KV_W), prev_map)
    return pl.pallas_call(
        functools.partial(_swa_kernel, pos0=pos0),
        grid_spec=pltpu.PrefetchScalarGridSpec(
            num_scalar_prefetch=1, grid=(b, t // tq),
            in_specs=[cur(SWA_Q_W), prev, prev, cur(SWA_KV_W), cur(SWA_KV_W)],
            out_specs=cur(SWA_Q_W)),
        out_shape=jax.ShapeDtypeStruct(q.shape, BF16),
        compiler_params=_params(("parallel", "arbitrary")),
        name="swa",
    )(sinks, q, kprev, vprev, kcur, vcur)


def _route_kernel(x_ref, o_ref, wo_ref, g_ref, wr_ref, x2_ref, xn_ref, route_ref):
    x2 = x_ref[...] + _dot(o_ref[...], wo_ref[...])
    xn = _rms(x2, g_ref[...])
    x2_ref[...] = x2
    _store_chunked(xn_ref, xn)
    hi = xn.astype(BF16)
    mid = (xn - hi.astype(F32)).astype(BF16)
    d = xn.shape[1]
    logits = _dot(hi, wr_ref[:d, :]) + _dot(hi, wr_ref[d:, :]) + _dot(mid, wr_ref[:d, :])
    lane = lax.broadcasted_iota(jnp.int32, logits.shape, 1).astype(F32)
    lg = jnp.where(lane < N_EXPERTS, logits, -jnp.inf)
    m1 = jnp.max(lg, axis=1, keepdims=True)
    i1 = jnp.min(jnp.where(lg == m1, lane, float(LANES)), axis=1, keepdims=True)
    lg2 = jnp.where(lane == i1, -jnp.inf, lg)
    m2 = jnp.max(lg2, axis=1, keepdims=True)
    i2 = jnp.min(jnp.where(lg2 == m2, lane, float(LANES)), axis=1, keepdims=True)
    e = jnp.exp(m2 - m1)
    g1 = 1.0 / (1.0 + e)
    g2 = e / (1.0 + e)
    route = jnp.where(lane == 0, i1, jnp.where(lane == 1, i2, jnp.where(lane == 2, g1, jnp.where(lane == 3, g2, 0.0))))
    route_ref[...] = route


def _route(x, o, wo, g, wr):
    n, d = x.shape
    tm = min(ROW_TILE, n)
    row = lambda c: pl.BlockSpec((tm, c), lambda i: (i, 0))
    xs = jax.ShapeDtypeStruct((n, d), F32)
    return pl.pallas_call(
        _route_kernel,
        grid=(n // tm,),
        in_specs=[row(d), row(SWA_Q_W), _resident(wo.shape), _resident((1, d)), _resident(wr.shape)],
        out_specs=[row(d), _chunked_spec(tm, d), row(LANES)],
        out_shape=[xs, _chunked_shape(n, d), jax.ShapeDtypeStruct((n, LANES), F32)],
        compiler_params=_params(("parallel",)),
        name="route",
    )(x, o, wo, g, wr)


def _chunked_shape(rows, d):
    return jax.ShapeDtypeStruct((d // GATHER_CHUNK, rows, GATHER_CHUNK), F32)


def _chunked_spec(tm, d, index_map=lambda i: (0, i, 0)):
    return pl.BlockSpec((d // GATHER_CHUNK, tm, GATHER_CHUNK), index_map)


def _store_chunked(ref, x):
    for j in range(x.shape[1] // GATHER_CHUNK):
        ref[j] = x[:, j * GATHER_CHUNK:(j + 1) * GATHER_CHUNK]


def _load_chunked(ref):
    return jnp.concatenate([ref[j] for j in range(ref.shape[0])], axis=1)


def _sc_gather_flat(flat, idx):
    m, w = idx.shape[1], flat.shape[1]
    mesh = plsc.VectorSubcoreMesh(core_axis_name="core", subcore_axis_name="subcore")

    @pl.kernel(out_type=jax.ShapeDtypeStruct((m, w), flat.dtype), mesh=mesh)
    def gather(x_hbm, i_hbm, o_hbm):
        def body(i_vmem, o_vmem):
            pltpu.sync_copy(x_hbm.at[i_vmem.at[0]], o_vmem)

        pltpu.emit_pipeline(
            body,
            grid=(m // SC_WINDOW,),
            in_specs=[pl.BlockSpec((1, SC_WINDOW), index_map=lambda i: (0, i))],
            out_specs=[pl.BlockSpec((SC_WINDOW, w), index_map=lambda i: (i, 0))],
            core_axis_name=("core", "subcore"),
            dimension_semantics=(pltpu.PARALLEL,),
        )(i_hbm, o_hbm)

    return gather(flat, idx)


def _gather_indices(rows, n_chunks, src_rows):
    return jnp.concatenate([rows + j * src_rows for j in range(n_chunks)]).reshape(1, -1)


def _sc_gather(src, idx):
    n_chunks, _, w = src.shape
    out = _sc_gather_flat(src.reshape(-1, w), idx)
    return out.reshape(n_chunks, idx.shape[1] // n_chunks, w)


def _expert_kernel(be_ref, nv_ref, x_ref, wg_ref, wu_ref, wd_ref, o_ref, xb_ref, acc_ref):
    blk, f = pl.program_id(0), pl.program_id(1)
    last = pl.num_programs(1) - 1
    valid = blk < nv_ref[0]

    @pl.when(valid & (f == 0))
    def _():
        for j in range(x_ref.shape[0]):
            xb_ref[:, j * GATHER_CHUNK:(j + 1) * GATHER_CHUNK] = x_ref[j].astype(BF16)
        acc_ref[...] = jnp.zeros_like(acc_ref)

    @pl.when(valid)
    def _():
        xb = xb_ref[...]
        h = (_silu(_dot(xb, wg_ref[0].astype(BF16))) * _dot(xb, wu_ref[0].astype(BF16))).astype(BF16)
        acc_ref[...] += _dot(h, wd_ref[0].astype(BF16))

    @pl.when(valid & (f == last))
    def _():
        _store_chunked(o_ref, acc_ref[...])

    @pl.when(jnp.logical_not(valid) & (f == last))
    def _():
        o_ref[...] = jnp.zeros_like(o_ref)


def _experts(block_expert, n_valid, x_slots, wg, wu, wd, *, tm, tf):
    d = wg.shape[1]
    cap = x_slots.shape[1]
    nf = wg.shape[2] // tf
    fsel = lambda b, f, nv: jnp.where(b < nv[0], f, nf - 1)
    slots = _chunked_spec(tm, d, lambda b, f, be, nv: (0, b, 0))
    return pl.pallas_call(
        _expert_kernel,
        grid_spec=pltpu.PrefetchScalarGridSpec(
            num_scalar_prefetch=2, grid=(cap // tm, nf),
            in_specs=[slots,
                      pl.BlockSpec((1, d, tf), lambda b, f, be, nv: (be[b], 0, fsel(b, f, nv))),
                      pl.BlockSpec((1, d, tf), lambda b, f, be, nv: (be[b], 0, fsel(b, f, nv))),
                      pl.BlockSpec((1, tf, d), lambda b, f, be, nv: (be[b], fsel(b, f, nv), 0))],
            out_specs=slots,
            scratch_shapes=[pltpu.VMEM((tm, d), BF16), pltpu.VMEM((tm, d), F32)]),
        out_shape=_chunked_shape(cap, d),
        compiler_params=_params(("arbitrary", "arbitrary")),
        name="moe_experts",
    )(block_expert, n_valid, x_slots, wg, wu, wd)


def _combine_kernel(y1_ref, y2_ref, x_ref, route_ref, g_ref, o_ref):
    route = route_ref[...]
    y = x_ref[...] + (_load_chunked(y1_ref) * route[:, 2:3] + _load_chunked(y2_ref) * route[:, 3:4])
    o_ref[...] = _rms(y, g_ref[...])


def _combine(y_pairs, x2, route, g):
    n, d = x2.shape
    tm = min(COMBINE_TILE, n)
    row = lambda c: pl.BlockSpec((tm, c), lambda i: (i, 0))
    second = n // tm
    return pl.pallas_call(
        _combine_kernel,
        grid=(n // tm,),
        in_specs=[_chunked_spec(tm, d), _chunked_spec(tm, d, lambda i: (0, i + second, 0)),
                  row(d), row(LANES), _resident((1, d))],
        out_specs=row(d),
        out_shape=jax.ShapeDtypeStruct((n, d), F32),
        compiler_params=_params(("parallel",)),
        name="moe_combine",
    )(y_pairs, y_pairs, x2, route, g)


def _dispatch(route, tm):
    n = route.shape[0]
    expert = route[:, :2].astype(jnp.int32).reshape(-1)
    onehot = (expert[:, None] == jnp.arange(N_EXPERTS, dtype=jnp.int32)[None, :]).astype(jnp.int32)
    csum = jnp.cumsum(onehot, axis=0)
    rank = jnp.sum(onehot * csum, axis=1) - 1
    counts = csum[-1]
    padded = (counts + tm - 1) // tm * tm
    pend = jnp.cumsum(padded)
    pstart = pend - padded
    dest = (jnp.sum(onehot * pstart[None, :], axis=1) + rank).astype(jnp.int32)
    step = max(tm, SLOT_ALIGN)
    cap = -(-(2 * n + N_EXPERTS * (tm - 1)) // step) * step
    n_blocks = cap // tm
    token = jnp.arange(2 * n, dtype=jnp.int32) // 2
    slot_token = jnp.zeros((n_blocks * tm,), jnp.int32).at[dest].set(token, unique_indices=True)
    block_expert = jnp.minimum(
        jnp.searchsorted(pend, jnp.arange(n_blocks, dtype=jnp.int32) * tm, side="right"), N_EXPERTS - 1
    ).astype(jnp.int32)
    n_valid = (pend[-1:] // tm).astype(jnp.int32)
    return slot_token, dest, block_expert, n_valid


def _run(x, caches, wts, *, tq_ab, tq_c, expert_rows):
    b, t, d = x.shape
    n = b * t
    xf = x.reshape(n, d)
    (qsb, ksb, vsb, qfx, kfx, vfx, ksb16, vsb16, kfx16, vfx16, logf) = _proj_ab(
        xf, wts["g_ab"], wts["w_in_ab"], wts["w_f"], wts["b_f"])
    r3 = lambda a: a.reshape(b, t, -1)
    logf3 = r3(logf)
    if caches is None:
        c = _cumsum_time(logf3.transpose(0, 2, 1)).transpose(0, 2, 1)
        c_q, c_p = c, c
        kp_sb, vp_sb, kp_fx, vp_fx = r3(ksb16), r3(vsb16), r3(kfx16), r3(vfx16)
        td = tk = min(ATTN_TILE, t)
        base, per_q = 0, 1
    else:
        p_sb_k, p_sb_v, p_fx_k, p_fx_v, p_logf = caches[:5]
        past = p_logf.shape[1]
        pad = (-(past + t)) % LANES
        allf = jnp.concatenate([p_logf.astype(F32), logf3, jnp.zeros((b, pad, N_FOX_HEADS), F32)], axis=1)
        c = _cumsum_time(allf.transpose(0, 2, 1)).transpose(0, 2, 1)
        c_q, c_p = c[:, past:past + t], c[:, :past]
        kp_sb, vp_sb, kp_fx, vp_fx = (a.reshape(b, past, -1) for a in (p_sb_k, p_sb_v, p_fx_k, p_fx_v))
        td, tk = t, min(SAMPLE_ATTN_TILE, past)
        base, per_q = past // tk, 0
    kw = dict(tq=tq_ab, td=td, tk=tk, base=base, per_q=per_q)
    o_sb = _attn_ab(_sb_kernel, r3(qsb), r3(ksb16), r3(vsb16), kp_sb, vp_sb, [], [], name="attn_sb", **kw)
    extra, extra_specs = _fox_bias_operands(c_q, c_p, tq_ab, td, tk)
    o_fx = _attn_ab(_fox_kernel, r3(qfx), r3(kfx16), r3(vfx16), kp_fx, vp_fx, extra, extra_specs,
                    name="attn_fox", **kw)
    x1 = _ffn_dense(xf, o_sb.reshape(n, SB_W), o_fx.reshape(n, FOX_W), wts["w_out_ab"], wts["g_ffn"],
                    wts["w_gate"], wts["w_up"], wts["w_down"])

    qc, kc, vc, kc16, vc16 = _proj_c(x1, wts["g_c"], wts["w_in_c"])
    if caches is None:
        kprev, vprev, pos0 = r3(kc16), r3(vc16), 0
        new_k, new_v = r3(kc)[:, t - SWA_CACHE:], r3(vc)[:, t - SWA_CACHE:]
    else:
        p_k, p_v = (a.reshape(b, SWA_CACHE, SWA_KV_W) for a in caches[5:])
        kprev, vprev, pos0 = p_k, p_v, caches[0].shape[1]
        new_k = jnp.concatenate([p_k, r3(kc)], axis=1)[:, t:]
        new_v = jnp.concatenate([p_v, r3(vc)], axis=1)[:, t:]
    o_c = _swa(wts["sinks"], r3(qc), kprev, vprev, r3(kc16), r3(vc16), tq=tq_c, pos0=pos0,
               prev_is_cache=caches is not None)

    x2, xn, route = _route(x1, o_c.reshape(n, SWA_Q_W), wts["w_out_c"], wts["g_moe"], wts["w_router"])
    slot_token, dest, block_expert, n_valid = _dispatch(route, expert_rows)
    heads = lambda a, h: a.reshape(1, b, -1, h, HEAD_DIM)
    states = (heads(ksb, N_SB_HEADS), heads(vsb, N_SB_HEADS), heads(kfx, N_FOX_HEADS), heads(vfx, N_FOX_HEADS),
              logf3[None], heads(new_k, N_SWA_KV_HEADS), heads(new_v, N_SWA_KV_HEADS))
    n_chunks = d // GATHER_CHUNK
    pair_rows = dest.reshape(n, 2).T.reshape(-1)
    return dict(xn=xn, slot_idx=_gather_indices(slot_token, n_chunks, n),
                pair_idx=_gather_indices(pair_rows, n_chunks, slot_token.shape[0]),
                block_expert=block_expert, n_valid=n_valid, x2=x2, route=route, states=states, shape=(b, t, d),
                expert_rows=expert_rows)


def _moe_experts(ctx, wts):
    x_slots = _sc_gather(ctx["xn"], ctx["slot_idx"])
    y_slots = _experts(ctx["block_expert"], ctx["n_valid"], x_slots, wts["w_gate_moe"], wts["w_up_moe"],
                       wts["w_down_moe"], tm=ctx["expert_rows"],
                       tf=EXPERT_COLS if ctx["expert_rows"] == EXPERT_ROWS else SAMPLE_EXPERT_COLS)
    return dict(ctx, y_slots=y_slots)


def _moe_combine(ctx, wts):
    y_pairs = _sc_gather(ctx["y_slots"], ctx["pair_idx"])
    y = _combine(y_pairs, ctx["x2"], ctx["route"], wts["g_final"])
    return y.reshape(ctx["shape"]), ctx["states"]


def _issue_after(first, later):
    return lax.optimization_barrier((first, later))


def kernel(x_prompt, x_sample, cache_sb_k, cache_sb_v, cache_fox_k, cache_fox_v, cache_fox_logf, cache_swa_k,
           cache_swa_v, norm_mix_ab, w_in_ab, b_forget, w_out_ab, norm_ffn_dense, w_gate_dense, w_up_dense,
           w_down_dense, norm_mix_c, w_in_c, sinks, w_out_c, norm_ffn_moe, w_router, w_gate_moe, w_up_moe,
           w_down_moe, norm_final):
    d = x_prompt.shape[-1]
    main_w = 3 * SB_W + 3 * FOX_W
    w_f = jnp.zeros((d, LANES), F32).at[:, :N_FOX_HEADS].set(w_in_ab[0][:, main_w:])
    wr = jnp.zeros((d, LANES), F32).at[:, :N_EXPERTS].set(w_router[0])
    wr_hi = wr.astype(BF16)
    wr_mid = (wr - wr_hi.astype(F32)).astype(BF16)
    wts = dict(
        g_ab=norm_mix_ab[0][None], w_in_ab=w_in_ab[0][:, :main_w].astype(BF16), w_f=w_f.astype(BF16),
        b_f=b_forget[0][None], w_out_ab=w_out_ab[0].astype(BF16), g_ffn=norm_ffn_dense[0][None],
        w_gate=w_gate_dense[0].astype(BF16), w_up=w_up_dense[0].astype(BF16), w_down=w_down_dense[0].astype(BF16),
        g_c=norm_mix_c[0][None], w_in_c=w_in_c[0].astype(BF16), sinks=sinks[0], w_out_c=w_out_c[0].astype(BF16),
        g_moe=norm_ffn_moe[0][None], w_router=jnp.concatenate([wr_hi, wr_mid], axis=0),
        w_gate_moe=w_gate_moe[0], w_up_moe=w_up_moe[0], w_down_moe=w_down_moe[0], g_final=norm_final[None])
    t_p, t_s = x_prompt.shape[1], x_sample.shape[1]
    caches = (cache_sb_k[0], cache_sb_v[0], cache_fox_k[0], cache_fox_v[0], cache_fox_logf[0],
              cache_swa_k[0], cache_swa_v[0])
    prompt = _run(x_prompt, None, wts, tq_ab=min(ATTN_ROWS, t_p), tq_c=min(SWA_TILE, t_p), expert_rows=EXPERT_ROWS)
    prompt["slot_idx"], (x_sample, caches) = _issue_after(prompt["slot_idx"], (x_sample, caches))
    sample = _run(x_sample, caches, wts, tq_ab=t_s, tq_c=t_s, expert_rows=SAMPLE_EXPERT_ROWS)
    prompt = _moe_experts(prompt, wts)
    prompt["y_slots"], sample["xn"] = _issue_after(prompt["y_slots"], sample["xn"])
    sample = _moe_experts(sample, wts)
    y_p, st_p = _moe_combine(prompt, wts)
    y_s, st_s = _moe_combine(sample, wts)
    return (y_p, y_s) + st_p + st_s
```

```python
import functools

import jax
import jax.numpy as jnp
from jax import lax
from jax.experimental import pallas as pl
from jax.experimental.pallas import tpu as pltpu
from jax.experimental.pallas import tpu_sc as plsc

F32 = jnp.float32
BF16 = jnp.bfloat16

HEAD_DIM = 64
LANES = 128
N_SB_HEADS = 8
N_FOX_HEADS = 8
N_SWA_HEADS = 16
N_SWA_KV_HEADS = 2
SWA_GROUP = N_SWA_HEADS // N_SWA_KV_HEADS
CHUNK = 64
WINDOW_CHUNKS = 2
SWA_CACHE = 128
N_EXPERTS = 8
RMS_EPS = 1e-6
NEG_INF = -1e30
ATTN_SCALE = HEAD_DIM ** -0.5
SB_W = N_SB_HEADS * HEAD_DIM
FOX_W = N_FOX_HEADS * HEAD_DIM
SWA_Q_W = N_SWA_HEADS * HEAD_DIM
SWA_KV_W = N_SWA_KV_HEADS * HEAD_DIM

VMEM_LIMIT_BYTES = 56 * 1024 * 1024
ROW_TILE = 512
COMBINE_TILE = 1024
ATTN_TILE = 512
ATTN_ROWS = 512
SAMPLE_ATTN_TILE = 2048
CUMSUM_SUB = 256
SWA_TILE = 256
SAMPLE_EXPERT_ROWS = 512
LOG2_E = 1.4426950408889634
EXPERT_ROWS = 1024
EXPERT_COLS = 512
SAMPLE_EXPERT_COLS = 896
GATHER_CHUNK = 256
SC_WINDOW = 128
SLOT_ALIGN = 1024
NT_DIMS = (((1,), (1,)), ((), ()))


def _params(sem):
    return pltpu.CompilerParams(dimension_semantics=sem, vmem_limit_bytes=VMEM_LIMIT_BYTES)


def _resident(shape):
    n = len(shape)
    return pl.BlockSpec(shape, lambda *_: (0,) * n, pipeline_mode=pl.Buffered(1))


def _rms(x, g):
    return x * lax.rsqrt(jnp.mean(x * x, axis=-1, keepdims=True) + RMS_EPS) * g


def _log_sigmoid(z):
    return jnp.minimum(z, 0.0) - jnp.log(1.0 + jnp.exp(-jnp.abs(z)))


def _silu(a):
    return a * (1.0 / (1.0 + jnp.exp(-a)))


def _dot(a, b):
    return jnp.dot(a, b, preferred_element_type=F32)


def _dot_nt(a, b):
    return lax.dot_general(a, b, NT_DIMS, preferred_element_type=F32)


def _split3(x):
    hi = x.astype(BF16)
    r = x - hi.astype(F32)
    mid = r.astype(BF16)
    lo = (r - mid.astype(F32)).astype(BF16)
    return hi, mid, lo


def _proj_ab_kernel(x_ref, g_ref, w_ref, wf_ref, bf_ref,
                    qsb_ref, ksb_ref, vsb_ref, qfx_ref, kfx_ref, vfx_ref,
                    ksb16_ref, vsb16_ref, kfx16_ref, vfx16_ref, logf_ref):
    hn = _rms(x_ref[...], g_ref[...]).astype(BF16)
    f32_outs = (None, ksb_ref, vsb_ref, None, kfx_ref, vfx_ref)
    b16_outs = (qsb_ref, ksb16_ref, vsb16_ref, qfx_ref, kfx16_ref, vfx16_ref)
    for c in range(6):
        p = _dot(hn, w_ref[:, c * SB_W:(c + 1) * SB_W])
        if f32_outs[c] is None:
            b16_outs[c][...] = (p * (ATTN_SCALE * LOG2_E)).astype(BF16)
        else:
            f32_outs[c][...] = p.reshape(p.shape[0], SB_W // HEAD_DIM, HEAD_DIM)
            b16_outs[c][...] = p.astype(BF16)
    f = _dot(hn, wf_ref[...])[:, :N_FOX_HEADS] + bf_ref[...]
    logf_ref[...] = _log_sigmoid(f)


def _proj_ab(x, g, w, wf, bf):
    n, d = x.shape
    tm = min(ROW_TILE, n)
    row = lambda c: pl.BlockSpec((tm, c), lambda i: (i, 0))
    f32o = jax.ShapeDtypeStruct((n, SB_W // HEAD_DIM, HEAD_DIM), F32)
    b16o = jax.ShapeDtypeStruct((n, SB_W), BF16)
    st = pl.BlockSpec((tm, SB_W // HEAD_DIM, HEAD_DIM), lambda i: (i, 0, 0))
    w16 = row(SB_W)
    return pl.pallas_call(
        _proj_ab_kernel,
        grid=(n // tm,),
        in_specs=[row(d), _resident((1, d)), _resident(w.shape), _resident(wf.shape), _resident((1, N_FOX_HEADS))],
        out_specs=[w16, st, st, w16, st, st, w16, w16, w16, w16, row(N_FOX_HEADS)],
        out_shape=[b16o, f32o, f32o, b16o, f32o, f32o, b16o, b16o, b16o, b16o,
                   jax.ShapeDtypeStruct((n, N_FOX_HEADS), F32)],
        compiler_params=_params(("parallel",)),
        name="proj_ab",
    )(x, g, w, wf, bf)


def _cumsum_kernel(x_ref, o_ref):
    t = x_ref.shape[-1]
    r = lax.broadcasted_iota(jnp.int32, (LANES, LANES), 0)
    c = lax.broadcasted_iota(jnp.int32, (LANES, LANES), 1)
    tri = (r <= c).astype(BF16)
    carry = jnp.zeros((x_ref.shape[1], 1), F32)
    for i in range(t // LANES):
        hi, mid, lo = _split3(x_ref[0, :, i * LANES:(i + 1) * LANES])
        cs = _dot(hi, tri) + _dot(mid, tri) + _dot(lo, tri) + carry
        o_ref[0, :, i * LANES:(i + 1) * LANES] = cs * LOG2_E
        carry = cs[:, LANES - 1:LANES]


def _cumsum_time(x):
    b, h, t = x.shape
    spec = pl.BlockSpec((1, h, t), lambda i: (i, 0, 0))
    return pl.pallas_call(
        _cumsum_kernel, grid=(b,), in_specs=[spec], out_specs=spec,
        out_shape=jax.ShapeDtypeStruct(x.shape, F32),
        compiler_params=_params(("parallel",)), name="cumsum_time",
    )(x)


def _stacked_heads(q):
    lane = lax.broadcasted_iota(jnp.int32, q.shape, 1)
    zero = jnp.zeros_like(q)
    return jnp.concatenate([jnp.where(lane < HEAD_DIM, q, zero), jnp.where(lane >= HEAD_DIM, q, zero)], axis=0)


def _unstack_heads(acc):
    tq = acc.shape[0] // 2
    lane = lax.broadcasted_iota(jnp.int32, (tq, LANES), 1)
    return jnp.where(lane < HEAD_DIM, acc[:tq], acc[tq:])


def _causal_mask(tq, td, first_row, strict):
    row = first_row + lax.broadcasted_iota(jnp.int32, (2 * tq, td), 0) % tq
    col = lax.broadcasted_iota(jnp.int32, (2 * tq, td), 1)
    return col < row if strict else col <= row


def _strict_upper(n):
    j = lax.broadcasted_iota(jnp.int32, (n, n), 0)
    s = lax.broadcasted_iota(jnp.int32, (n, n), 1)
    return (j > s).astype(BF16)


def _sb_block(z, v, total, acc_ref, u, vis):
    sub = u.shape[0]
    neg_abs = lax.bitcast_convert_type(lax.bitcast_convert_type(z, jnp.uint32) | jnp.uint32(0x80000000), F32)
    ls = jnp.minimum(z, 0.0) - jnp.log2(1.0 + jnp.exp2(neg_abs))
    lk = ls - z
    if vis is not None:
        lk = jnp.where(vis, lk, 0.0)
    lk = lk.astype(BF16)
    for s in reversed(range(z.shape[1] // sub)):
        cols = slice(s * sub, (s + 1) * sub)
        after = _dot(lk[:, cols], u)
        w = jnp.exp2(ls[:, cols] + after + total)
        if vis is not None:
            w = jnp.where(vis[:, cols], w, 0.0)
        acc_ref[...] += _dot(w.astype(BF16), v[cols, :])
        total = total + after[:, 0:1] + lk[:, s * sub:s * sub + 1].astype(F32)
    return total


def _sb_kernel(q_ref, kd_ref, vd_ref, kp_ref, vp_ref, o_ref, acc_ref, *, tk, base, per_q):
    tq, td = q_ref.shape[1], kd_ref.shape[1]
    qs = _stacked_heads(q_ref[0])
    first_row = (pl.program_id(2) % (td // tq)) * tq
    vis = _causal_mask(tq, td, first_row, strict=True)
    u_diag = _strict_upper(min(CUMSUM_SUB, td))
    u_past = _strict_upper(CUMSUM_SUB)
    acc_ref[...] = jnp.zeros_like(acc_ref)
    total = _sb_block(_dot_nt(qs, kd_ref[0]), vd_ref[0], jnp.zeros((2 * tq, 1), F32), acc_ref, u_diag, vis)

    n_past = base + pl.program_id(2) // (td // tq) * per_q

    def body(jj, total):
        start = pl.multiple_of((n_past - 1 - jj) * tk, tk)
        k = kp_ref[0, pl.ds(start, tk), :].astype(BF16)
        v = vp_ref[0, pl.ds(start, tk), :].astype(BF16)
        return _sb_block(_dot_nt(qs, k), v, total, acc_ref, u_past, None)

    lax.fori_loop(0, n_past, body, total)
    o_ref[0] = _unstack_heads(acc_ref[...]).astype(o_ref.dtype)


def _fox_block(qk, v, ck, m, l, acc_ref, vis):
    tq = qk.shape[0] // 2
    s = jnp.concatenate([qk[:tq] - ck[0:1, :], qk[tq:] - ck[1:2, :]], axis=0)
    if vis is not None:
        s = jnp.where(vis, s, NEG_INF)
    m_new = jnp.maximum(m, jnp.max(s, axis=1, keepdims=True))
    alpha = jnp.exp2(m - m_new)
    p = jnp.exp2(s - m_new)
    acc_ref[...] = alpha * acc_ref[...] + _dot(p.astype(BF16), v)
    return m_new, alpha * l + jnp.sum(p, axis=1, keepdims=True)


def _fox_kernel(q_ref, kd_ref, vd_ref, kp_ref, vp_ref, ckd_ref, ckp_ref, o_ref, acc_ref, *, tk, base, per_q):
    tq, td = q_ref.shape[1], kd_ref.shape[1]
    qs = _stacked_heads(q_ref[0])
    first_row = (pl.program_id(2) % (td // tq)) * tq
    vis = _causal_mask(tq, td, first_row, strict=False)
    acc_ref[...] = jnp.zeros_like(acc_ref)
    m0 = jnp.full((2 * tq, 1), NEG_INF, F32)
    l0 = jnp.zeros((2 * tq, 1), F32)
    stats = _fox_block(_dot_nt(qs, kd_ref[0]), vd_ref[0], ckd_ref[0, 0], m0, l0, acc_ref, vis)

    n_past = base + pl.program_id(2) // (td // tq) * per_q

    def body(j, stats):
        start = pl.multiple_of(j * tk, tk)
        k = kp_ref[0, pl.ds(start, tk), :].astype(BF16)
        v = vp_ref[0, pl.ds(start, tk), :].astype(BF16)
        return _fox_block(_dot_nt(qs, k), v, ckp_ref[0, 0, j], stats[0], stats[1], acc_ref, None)

    _, l = lax.fori_loop(0, n_past, body, stats)
    o_ref[0] = _unstack_heads(acc_ref[...] / l).astype(o_ref.dtype)


def _attn_ab(kernel_fn, q, kd, vd, kp, vp, extra, extra_specs, *, tq, td, tk, base, per_q, name):
    b, t, w = q.shape
    tp = kp.shape[1]
    grid = (b, w // LANES, t // tq)
    qspec = pl.BlockSpec((1, tq, LANES), lambda bi, hp, i: (bi, i, hp))
    dspec = pl.BlockSpec((1, td, LANES), lambda bi, hp, i: (bi, i // (td // tq), hp))
    pspec = pl.BlockSpec((1, tp, LANES), lambda bi, hp, i: (bi, 0, hp))
    return pl.pallas_call(
        functools.partial(kernel_fn, tk=tk, base=base, per_q=per_q),
        grid=grid,
        in_specs=[qspec, dspec, dspec, pspec, pspec] + extra_specs,
        out_specs=qspec,
        out_shape=jax.ShapeDtypeStruct(q.shape, BF16),
        scratch_shapes=[pltpu.VMEM((2 * tq, LANES), F32)],
        compiler_params=_params(("parallel", "parallel", "arbitrary")),
        name=name,
    )(q, kd, vd, kp, vp, *extra)


def _fox_bias_operands(c_q, c_p, tq, td, tk):
    b, t, h = c_q.shape
    tp = c_p.shape[1]
    hp = h // 2
    ckd = c_q.reshape(b, t, hp, 2).transpose(0, 2, 3, 1)
    ckp = c_p.reshape(b, tp // tk, tk, hp, 2).transpose(0, 3, 1, 4, 2)
    specs = [pl.BlockSpec((1, 1, 2, td), lambda bi, p, i: (bi, p, 0, i // (td // tq))),
             pl.BlockSpec((1, 1, tp // tk, 2, tk), lambda bi, p, i: (bi, p, 0, 0, 0))]
    return [ckd, ckp], specs


def _ffn_dense_kernel(x_ref, osb_ref, ofx_ref, wo_ref, g_ref, wg_ref, wu_ref, wd_ref, o_ref, *, tf):
    o_ref[...] = x_ref[...] + _dot(osb_ref[...], wo_ref[:SB_W, :]) + _dot(ofx_ref[...], wo_ref[SB_W:, :])
    hn = _rms(o_ref[...], g_ref[...]).astype(BF16)
    for c in range(wg_ref.shape[1] // tf):
        a = _dot(hn, wg_ref[:, c * tf:(c + 1) * tf])
        u = _dot(hn, wu_ref[:, c * tf:(c + 1) * tf])
        o_ref[...] += _dot((_silu(a) * u).astype(BF16), wd_ref[c * tf:(c + 1) * tf, :])


def _ffn_dense(x, osb, ofx, wo, g, wg, wu, wd):
    n, d = x.shape
    tm = min(ROW_TILE, n)
    row = lambda c: pl.BlockSpec((tm, c), lambda i: (i, 0))
    return pl.pallas_call(
        functools.partial(_ffn_dense_kernel, tf=256),
        grid=(n // tm,),
        in_specs=[row(d), row(SB_W), row(FOX_W), _resident(wo.shape), _resident((1, d)),
                  _resident(wg.shape), _resident(wu.shape), _resident(wd.shape)],
        out_specs=row(d),
        out_shape=jax.ShapeDtypeStruct((n, d), F32),
        compiler_params=_params(("parallel",)),
        name="ffn_dense",
    )(x, osb, ofx, wo, g, wg, wu, wd)


def _proj_c_kernel(x_ref, g_ref, w_ref, q_ref, k_ref, v_ref, k16_ref, v16_ref):
    hn = _rms(x_ref[...], g_ref[...]).astype(BF16)
    q_ref[...] = (_dot(hn, w_ref[:, :SWA_Q_W]) * ATTN_SCALE).astype(BF16)
    k = _dot(hn, w_ref[:, SWA_Q_W:SWA_Q_W + SWA_KV_W])
    v = _dot(hn, w_ref[:, SWA_Q_W + SWA_KV_W:])
    k_ref[...] = k
    v_ref[...] = v
    k16_ref[...] = k.astype(BF16)
    v16_ref[...] = v.astype(BF16)


def _proj_c(x, g, w):
    n, d = x.shape
    tm = min(ROW_TILE, n)
    row = lambda c: pl.BlockSpec((tm, c), lambda i: (i, 0))
    kv32 = jax.ShapeDtypeStruct((n, SWA_KV_W), F32)
    kv16 = jax.ShapeDtypeStruct((n, SWA_KV_W), BF16)
    return pl.pallas_call(
        _proj_c_kernel,
        grid=(n // tm,),
        in_specs=[row(d), _resident((1, d)), _resident(w.shape)],
        out_specs=[row(SWA_Q_W)] + [row(SWA_KV_W)] * 4,
        out_shape=[jax.ShapeDtypeStruct((n, SWA_Q_W), BF16), kv32, kv32, kv16, kv16],
        compiler_params=_params(("parallel",)),
        name="proj_c",
    )(x, g, w)


def _swap_halves(a):
    return jnp.concatenate([a[:, HEAD_DIM:], a[:, :HEAD_DIM]], axis=1)


def _swa_kernel(sink_ref, q_ref, kprev_ref, vprev_ref, kcur_ref, vcur_ref, o_ref, *, pos0):
    tq = q_ref.shape[1]
    band = (WINDOW_CHUNKS + 1) * CHUNK
    stack = SWA_GROUP // 2
    rows = stack * CHUNK
    first = pos0 + pl.program_id(1) * tq
    kw = jnp.concatenate([kprev_ref[0].astype(BF16), kcur_ref[0].astype(BF16)], axis=0)
    vw = jnp.concatenate([vprev_ref[0].astype(BF16), vcur_ref[0].astype(BF16)], axis=0)
    k_by_half = (kw, _swap_halves(kw))
    v_by_half = (vw, _swap_halves(vw))
    q = q_ref[0]
    in_low = lax.broadcasted_iota(jnp.int32, q.shape, 1) % LANES < HEAD_DIM
    q_by_half = (jnp.where(in_low, q, jnp.zeros_like(q)), jnp.where(in_low, jnp.zeros_like(q), q))
    r_id = lax.broadcasted_iota(jnp.int32, (rows, band), 0)
    c_id = lax.broadcasted_iota(jnp.int32, (rows, band), 1)
    dist = jnp.abs(WINDOW_CHUNKS * CHUNK + r_id % CHUNK - c_id).astype(F32)
    head_in_stack = lax.broadcasted_iota(jnp.int32, (rows, 1), 0) // CHUNK
    col = lax.broadcasted_iota(jnp.int32, (1, band), 1)
    lane = lax.broadcasted_iota(jnp.int32, (CHUNK, LANES), 1)

    def per_head(values):
        out = jnp.full((rows, 1), values[-1], F32)
        for i in range(stack - 2, -1, -1):
            out = jnp.where(head_in_stack == i, values[i], out)
        return out

    heads = {(g, half): [g * SWA_GROUP + 2 * i + half for i in range(stack)]
             for g in range(N_SWA_KV_HEADS) for half in range(2)}
    bias = {key: per_head([2.0 ** (-8.0 * (h + 1) / N_SWA_HEADS) for h in hs]) * dist for key, hs in heads.items()}
    sink = {key: per_head([sink_ref[h] for h in hs]) for key, hs in heads.items()}

    for c in range(tq // CHUNK):
        in_range = first - WINDOW_CHUNKS * CHUNK + c * CHUNK + col >= 0
        win = slice(c * CHUNK, c * CHUNK + band)
        outs = {}
        for (g, half), hs in heads.items():
            qs = jnp.concatenate([q_by_half[half][c * CHUNK:(c + 1) * CHUNK, (h // 2) * LANES:(h // 2 + 1) * LANES]
                                  for h in hs], axis=0)
            s = _dot_nt(qs, k_by_half[g ^ half][win]) - bias[g, half]
            s = jnp.where(in_range, s, NEG_INF)
            m = jnp.maximum(jnp.max(s, axis=1, keepdims=True), sink[g, half])
            p = jnp.exp(s - m)
            l = jnp.sum(p, axis=1, keepdims=True) + jnp.exp(sink[g, half] - m)
            outs[g, half] = _dot(p.astype(BF16), v_by_half[g ^ half][win]) / l
        for g in range(N_SWA_KV_HEADS):
            for i in range(stack):
                pair = (g * SWA_GROUP + 2 * i) // 2
                o_ref[0, c * CHUNK:(c + 1) * CHUNK, pair * LANES:(pair + 1) * LANES] = jnp.where(
                    lane < HEAD_DIM, outs[g, 0][i * CHUNK:(i + 1) * CHUNK],
                    outs[g, 1][i * CHUNK:(i + 1) * CHUNK]).astype(o_ref.dtype)


def _swa(sinks, q, kprev, vprev, kcur, vcur, *, tq, pos0, prev_is_cache):
    b, t, _ = q.shape
    per = tq // SWA_CACHE
    prev_map = ((lambda bi, i, s: (bi, 0, 0)) if prev_is_cache
                else (lambda bi, i, s: (bi, jnp.maximum(i * per - 1, 0), 0)))
    cur = lambda w: pl.BlockSpec((1, tq, w), lambda bi, i, s: (bi, i, 0))
    prev = pl.BlockSpec((1, SWA_CACHE, SWA_KV_W), prev_map)
    return pl.pallas_call(
        functools.partial(_swa_kernel, pos0=pos0),
        grid_spec=pltpu.PrefetchScalarGridSpec(
            num_scalar_prefetch=1, grid=(b, t // tq),
            in_specs=[cur(SWA_Q_W), prev, prev, cur(SWA_KV_W), cur(SWA_KV_W)],
            out_specs=cur(SWA_Q_W)),
        out_shape=jax.ShapeDtypeStruct(q.shape, BF16),
        compiler_params=_params(("parallel", "arbitrary")),
        name="swa",
    )(sinks, q, kprev, vprev, kcur, vcur)


def _route_kernel(x_ref, o_ref, wo_ref, g_ref, wr_ref, x2_ref, xn_ref, route_ref):
    x2 = x_ref[...] + _dot(o_ref[...], wo_ref[...])
    xn = _rms(x2, g_ref[...])
    x2_ref[...] = x2
    _store_chunked(xn_ref, xn)
    hi = xn.astype(BF16)
    mid = (xn - hi.astype(F32)).astype(BF16)
    d = xn.shape[1]
    logits = _dot(hi, wr_ref[:d, :]) + _dot(hi, wr_ref[d:, :]) + _dot(mid, wr_ref[:d, :])
    lane = lax.broadcasted_iota(jnp.int32, logits.shape, 1).astype(F32)
    lg = jnp.where(lane < N_EXPERTS, logits, -jnp.inf)
    m1 = jnp.max(lg, axis=1, keepdims=True)
    i1 = jnp.min(jnp.where(lg == m1, lane, float(LANES)), axis=1, keepdims=True)
    lg2 = jnp.where(lane == i1, -jnp.inf, lg)
    m2 = jnp.max(lg2, axis=1, keepdims=True)
    i2 = jnp.min(jnp.where(lg2 == m2, lane, float(LANES)), axis=1, keepdims=True)
    e = jnp.exp(m2 - m1)
    g1 = 1.0 / (1.0 + e)
    g2 = e / (1.0 + e)
    route = jnp.where(lane == 0, i1, jnp.where(lane == 1, i2, jnp.where(lane == 2, g1, jnp.where(lane == 3, g2, 0.0))))
    route_ref[...] = route


def _route(x, o, wo, g, wr):
    n, d = x.shape
    tm = min(ROW_TILE, n)
    row = lambda c: pl.BlockSpec((tm, c), lambda i: (i, 0))
    xs = jax.ShapeDtypeStruct((n, d), F32)
    return pl.pallas_call(
        _route_kernel,
        grid=(n // tm,),
        in_specs=[row(d), row(SWA_Q_W), _resident(wo.shape), _resident((1, d)), _resident(wr.shape)],
        out_specs=[row(d), _chunked_spec(tm, d), row(LANES)],
        out_shape=[xs, _chunked_shape(n, d), jax.ShapeDtypeStruct((n, LANES), F32)],
        compiler_params=_params(("parallel",)),
        name="route",
    )(x, o, wo, g, wr)


def _chunked_shape(rows, d):
    return jax.ShapeDtypeStruct((d // GATHER_CHUNK, rows, GATHER_CHUNK), F32)


def _chunked_spec(tm, d, index_map=lambda i: (0, i, 0)):
    return pl.BlockSpec((d // GATHER_CHUNK, tm, GATHER_CHUNK), index_map)


def _store_chunked(ref, x):
    for j in range(x.shape[1] // GATHER_CHUNK):
        ref[j] = x[:, j * GATHER_CHUNK:(j + 1) * GATHER_CHUNK]


def _load_chunked(ref):
    return jnp.concatenate([ref[j] for j in range(ref.shape[0])], axis=1)


def _sc_gather_flat(flat, idx):
    m, w = idx.shape[1], flat.shape[1]
    mesh = plsc.VectorSubcoreMesh(core_axis_name="core", subcore_axis_name="subcore")

    @pl.kernel(out_type=jax.ShapeDtypeStruct((m, w), flat.dtype), mesh=mesh)
    def gather(x_hbm, i_hbm, o_hbm):
        def body(i_vmem, o_vmem):
            pltpu.sync_copy(x_hbm.at[i_vmem.at[0]], o_vmem)

        pltpu.emit_pipeline(
            body,
            grid=(m // SC_WINDOW,),
            in_specs=[pl.BlockSpec((1, SC_WINDOW), index_map=lambda i: (0, i))],
            out_specs=[pl.BlockSpec((SC_WINDOW, w), index_map=lambda i: (i, 0))],
            core_axis_name=("core", "subcore"),
            dimension_semantics=(pltpu.PARALLEL,),
        )(i_hbm, o_hbm)

    return gather(flat, idx)


def _gather_indices(rows, n_chunks, src_rows):
    return jnp.concatenate([rows + j * src_rows for j in range(n_chunks)]).reshape(1, -1)


def _sc_gather(src, idx):
    n_chunks, _, w = src.shape
    out = _sc_gather_flat(src.reshape(-1, w), idx)
    return out.reshape(n_chunks, idx.shape[1] // n_chunks, w)


def _expert_kernel(be_ref, nv_ref, x_ref, wg_ref, wu_ref, wd_ref, o_ref, xb_ref, acc_ref):
    blk, f = pl.program_id(0), pl.program_id(1)
    last = pl.num_programs(1) - 1
    valid = blk < nv_ref[0]

    @pl.when(valid & (f == 0))
    def _():
        for j in range(x_ref.shape[0]):
            xb_ref[:, j * GATHER_CHUNK:(j + 1) * GATHER_CHUNK] = x_ref[j].astype(BF16)
        acc_ref[...] = jnp.zeros_like(acc_ref)

    @pl.when(valid)
    def _():
        xb = xb_ref[...]
        h = (_silu(_dot(xb, wg_ref[0].astype(BF16))) * _dot(xb, wu_ref[0].astype(BF16))).astype(BF16)
        acc_ref[...] += _dot(h, wd_ref[0].astype(BF16))

    @pl.when(valid & (f == last))
    def _():
        _store_chunked(o_ref, acc_ref[...])

    @pl.when(jnp.logical_not(valid) & (f == last))
    def _():
        o_ref[...] = jnp.zeros_like(o_ref)


def _experts(block_expert, n_valid, x_slots, wg, wu, wd, *, tm, tf):
    d = wg.shape[1]
    cap = x_slots.shape[1]
    nf = wg.shape[2] // tf
    fsel = lambda b, f, nv: jnp.where(b < nv[0], f, nf - 1)
    slots = _chunked_spec(tm, d, lambda b, f, be, nv: (0, b, 0))
    return pl.pallas_call(
        _expert_kernel,
        grid_spec=pltpu.PrefetchScalarGridSpec(
            num_scalar_prefetch=2, grid=(cap // tm, nf),
            in_specs=[slots,
                      pl.BlockSpec((1, d, tf), lambda b, f, be, nv: (be[b], 0, fsel(b, f, nv))),
                      pl.BlockSpec((1, d, tf), lambda b, f, be, nv: (be[b], 0, fsel(b, f, nv))),
                      pl.BlockSpec((1, tf, d), lambda b, f, be, nv: (be[b], fsel(b, f, nv), 0))],
            out_specs=slots,
            scratch_shapes=[pltpu.VMEM((tm, d), BF16), pltpu.VMEM((tm, d), F32)]),
        out_shape=_chunked_shape(cap, d),
        compiler_params=_params(("arbitrary", "arbitrary")),
        name="moe_experts",
    )(block_expert, n_valid, x_slots, wg, wu, wd)


def _combine_kernel(y1_ref, y2_ref, x_ref, route_ref, g_ref, o_ref):
    route = route_ref[...]
    y = x_ref[...] + (_load_chunked(y1_ref) * route[:, 2:3] + _load_chunked(y2_ref) * route[:, 3:4])
    o_ref[...] = _rms(y, g_ref[...])


def _combine(y_pairs, x2, route, g):
    n, d = x2.shape
    tm = min(COMBINE_TILE, n)
    row = lambda c: pl.BlockSpec((tm, c), lambda i: (i, 0))
    second = n // tm
    return pl.pallas_call(
        _combine_kernel,
        grid=(n // tm,),
        in_specs=[_chunked_spec(tm, d), _chunked_spec(tm, d, lambda i: (0, i + second, 0)),
                  row(d), row(LANES), _resident((1, d))],
        out_specs=row(d),
        out_shape=jax.ShapeDtypeStruct((n, d), F32),
        compiler_params=_params(("parallel",)),
        name="moe_combine",
    )(y_pairs, y_pairs, x2, route, g)


def _dispatch(route, tm):
    n = route.shape[0]
    expert = route[:, :2].astype(jnp.int32).reshape(-1)
    onehot = (expert[:, None] == jnp.arange(N_EXPERTS, dtype=jnp.int32)[None, :]).astype(jnp.int32)
    csum = jnp.cumsum(onehot, axis=0)
    rank = jnp.sum(onehot * csum, axis=1) - 1
    counts = csum[-1]
    padded = (counts + tm - 1) // tm * tm
    pend = jnp.cumsum(padded)
    pstart = pend - padded
    dest = (jnp.sum(onehot * pstart[None, :], axis=1) + rank).astype(jnp.int32)
    step = max(tm, SLOT_ALIGN)
    cap = -(-(2 * n + N_EXPERTS * (tm - 1)) // step) * step
    n_blocks = cap // tm
    token = jnp.arange(2 * n, dtype=jnp.int32) // 2
    slot_token = jnp.zeros((n_blocks * tm,), jnp.int32).at[dest].set(token, unique_indices=True)
    block_expert = jnp.minimum(
        jnp.searchsorted(pend, jnp.arange(n_blocks, dtype=jnp.int32) * tm, side="right"), N_EXPERTS - 1
    ).astype(jnp.int32)
    n_valid = (pend[-1:] // tm).astype(jnp.int32)
    return slot_token, dest, block_expert, n_valid


def _run(x, caches, wts, *, tq_ab, tq_c, expert_rows):
    b, t, d = x.shape
    n = b * t
    xf = x.reshape(n, d)
    (qsb, ksb, vsb, qfx, kfx, vfx, ksb16, vsb16, kfx16, vfx16, logf) = _proj_ab(
        xf, wts["g_ab"], wts["w_in_ab"], wts["w_f"], wts["b_f"])
    r3 = lambda a: a.reshape(b, t, -1)
    logf3 = r3(logf)
    if caches is None:
        c = _cumsum_time(logf3.transpose(0, 2, 1)).transpose(0, 2, 1)
        c_q, c_p = c, c
        kp_sb, vp_sb, kp_fx, vp_fx = r3(ksb16), r3(vsb16), r3(kfx16), r3(vfx16)
        td = tk = min(ATTN_TILE, t)
        base, per_q = 0, 1
    else:
        p_sb_k, p_sb_v, p_fx_k, p_fx_v, p_logf = caches[:5]
        past = p_logf.shape[1]
        pad = (-(past + t)) % LANES
        allf = jnp.concatenate([p_logf.astype(F32), logf3, jnp.zeros((b, pad, N_FOX_HEADS), F32)], axis=1)
        c = _cumsum_time(allf.transpose(0, 2, 1)).transpose(0, 2, 1)
        c_q, c_p = c[:, past:past + t], c[:, :past]
        kp_sb, vp_sb, kp_fx, vp_fx = (a.reshape(b, past, -1) for a in (p_sb_k, p_sb_v, p_fx_k, p_fx_v))
        td, tk = t, min(SAMPLE_ATTN_TILE, past)
        base, per_q = past // tk, 0
    kw = dict(tq=tq_ab, td=td, tk=tk, base=base, per_q=per_q)
    o_sb = _attn_ab(_sb_kernel, r3(qsb), r3(ksb16), r3(vsb16), kp_sb, vp_sb, [], [], name="attn_sb", **kw)
    extra, extra_specs = _fox_bias_operands(c_q, c_p, tq_ab, td, tk)
    o_fx = _attn_ab(_fox_kernel, r3(qfx), r3(kfx16), r3(vfx16), kp_fx, vp_fx, extra, extra_specs,
                    name="attn_fox", **kw)
    x1 = _ffn_dense(xf, o_sb.reshape(n, SB_W), o_fx.reshape(n, FOX_W), wts["w_out_ab"], wts["g_ffn"],
                    wts["w_gate"], wts["w_up"], wts["w_down"])

    qc, kc, vc, kc16, vc16 = _proj_c(x1, wts["g_c"], wts["w_in_c"])
    if caches is None:
        kprev, vprev, pos0 = r3(kc16), r3(vc16), 0
        new_k, new_v = r3(kc)[:, t - SWA_CACHE:], r3(vc)[:, t - SWA_CACHE:]
    else:
        p_k, p_v = (a.reshape(b, SWA_CACHE, SWA_KV_W) for a in caches[5:])
        kprev, vprev, pos0 = p_k, p_v, caches[0].shape[1]
        new_k = jnp.concatenate([p_k, r3(kc)], axis=1)[:, t:]
        new_v = jnp.concatenate([p_v, r3(vc)], axis=1)[:, t:]
    o_c = _swa(wts["sinks"], r3(qc), kprev, vprev, r3(kc16), r3(vc16), tq=tq_c, pos0=pos0,
               prev_is_cache=caches is not None)

    x2, xn, route = _route(x1, o_c.reshape(n, SWA_Q_W), wts["w_out_c"], wts["g_moe"], wts["w_router"])
    slot_token, dest, block_expert, n_valid = _dispatch(route, expert_rows)
    heads = lambda a, h: a.reshape(1, b, -1, h, HEAD_DIM)
    states = (heads(ksb, N_SB_HEADS), heads(vsb, N_SB_HEADS), heads(kfx, N_FOX_HEADS), heads(vfx, N_FOX_HEADS),
              logf3[None], heads(new_k, N_SWA_KV_HEADS), heads(new_v, N_SWA_KV_HEADS))
    n_chunks = d // GATHER_CHUNK
    pair_rows = dest.reshape(n, 2).T.reshape(-1)
    return dict(xn=xn, slot_idx=_gather_indices(slot_token, n_chunks, n),
                pair_idx=_gather_indices(pair_rows, n_chunks, slot_token.shape[0]),
                block_expert=block_expert, n_valid=n_valid, x2=x2, route=route, states=states, shape=(b, t, d),
                expert_rows=expert_rows)


def _moe_experts(ctx, wts):
    x_slots = _sc_gather(ctx["xn"], ctx["slot_idx"])
    y_slots = _experts(ctx["block_expert"], ctx["n_valid"], x_slots, wts["w_gate_moe"], wts["w_up_moe"],
                       wts["w_down_moe"], tm=ctx["expert_rows"],
                       tf=EXPERT_COLS if ctx["expert_rows"] == EXPERT_ROWS else SAMPLE_EXPERT_COLS)
    return dict(ctx, y_slots=y_slots)


def _moe_combine(ctx, wts):
    y_pairs = _sc_gather(ctx["y_slots"], ctx["pair_idx"])
    y = _combine(y_pairs, ctx["x2"], ctx["route"], wts["g_final"])
    return y.reshape(ctx["shape"]), ctx["states"]


def _issue_after(first, later):
    return lax.optimization_barrier((first, later))


def kernel(x_prompt, x_sample, cache_sb_k, cache_sb_v, cache_fox_k, cache_fox_v, cache_fox_logf, cache_swa_k,
           cache_swa_v, norm_mix_ab, w_in_ab, b_forget, w_out_ab, norm_ffn_dense, w_gate_dense, w_up_dense,
           w_down_dense, norm_mix_c, w_in_c, sinks, w_out_c, norm_ffn_moe, w_router, w_gate_moe, w_up_moe,
           w_down_moe, norm_final):
    d = x_prompt.shape[-1]
    main_w = 3 * SB_W + 3 * FOX_W
    w_f = jnp.zeros((d, LANES), F32).at[:, :N_FOX_HEADS].set(w_in_ab[0][:, main_w:])
    wr = jnp.zeros((d, LANES), F32).at[:, :N_EXPERTS].set(w_router[0])
    wr_hi = wr.astype(BF16)
    wr_mid = (wr - wr_hi.astype(F32)).astype(BF16)
    wts = dict(
        g_ab=norm_mix_ab[0][None], w_in_ab=w_in_ab[0][:, :main_w].astype(BF16), w_f=w_f.astype(BF16),
        b_f=b_forget[0][None], w_out_ab=w_out_ab[0].astype(BF16), g_ffn=norm_ffn_dense[0][None],
        w_gate=w_gate_dense[0].astype(BF16), w_up=w_up_dense[0].astype(BF16), w_down=w_down_dense[0].astype(BF16),
        g_c=norm_mix_c[0][None], w_in_c=w_in_c[0].astype(BF16), sinks=sinks[0], w_out_c=w_out_c[0].astype(BF16),
        g_moe=norm_ffn_moe[0][None], w_router=jnp.concatenate([wr_hi, wr_mid], axis=0),
        w_gate_moe=w_gate_moe[0], w_up_moe=w_up_moe[0], w_down_moe=w_down_moe[0], g_final=norm_final[None])
    t_p, t_s = x_prompt.shape[1], x_sample.shape[1]
    caches = (cache_sb_k[0], cache_sb_v[0], cache_fox_k[0], cache_fox_v[0], cache_fox_logf[0],
              cache_swa_k[0], cache_swa_v[0])
    prompt = _run(x_prompt, None, wts, tq_ab=min(ATTN_ROWS, t_p), tq_c=min(SWA_TILE, t_p), expert_rows=EXPERT_ROWS)
    prompt["slot_idx"], (x_sample, caches) = _issue_after(prompt["slot_idx"], (x_sample, caches))
    sample = _run(x_sample, caches, wts, tq_ab=t_s, tq_c=t_s, expert_rows=SAMPLE_EXPERT_ROWS)
    prompt = _moe_experts(prompt, wts)
    prompt["y_slots"], sample["xn"] = _issue_after(prompt["y_slots"], sample["xn"])
    sample = _moe_experts(sample, wts)
    y_p, st_p = _moe_combine(prompt, wts)
    y_s, st_s = _moe_combine(sample, wts)
    return (y_p, y_s) + st_p + st_s
```
